```python
import math
import jax, jax.numpy as jnp
from jax import lax
import numpy as np

D_MODEL = 1024
BATCH = 32
SEQ = 2048
DEPTH = 2
DEC_BATCH = 2
DEC_SEQ = 16384
PAST_LEN = 128

N_HEADS_ATTN = 8
QK_NOPE_DIM = 64
QK_ROPE_DIM = 32
V_HEAD_DIM = 64
Q_LORA_RANK = 256
KV_LORA_RANK = 128
ATTN_WIDTH = N_HEADS_ATTN * V_HEAD_DIM
ROPE_BASE = 10000.0
Q_BLOCK = 128
CHUNK = 128
N_SGU_GROUPS = 8
SGU_WIDTH = 512
SGU_GROUP_DIM = SGU_WIDTH // N_SGU_GROUPS
IN_WIDTH = Q_LORA_RANK + KV_LORA_RANK + QK_ROPE_DIM + 2 * SGU_WIDTH + 2 * D_MODEL
N_EXPERTS = 32
TOP_K = 4
D_FF = D_MODEL
SWIGLU_LIMIT = 7.0
SWIGLU_ALPHA = 1.702
MOE_BLOCK = 128
DN_ALPHA = (2 * DEPTH) ** 0.25
DN_BETA = (8 * DEPTH) ** -0.25
LN_EPS = 1e-5
RMS_EPS = 1e-6

kernel_name = 'hybrid_mla_sgu_moe_encoder'


def layer_norm(x, g, b):
    xf = x.astype(jnp.float32)
    mu = jnp.mean(xf, axis=-1, keepdims=True)
    var = jnp.mean(jnp.square(xf - mu), axis=-1, keepdims=True)
    y = (xf - mu) * lax.rsqrt(var + LN_EPS)
    return (y * g.astype(jnp.float32) + b.astype(jnp.float32)).astype(x.dtype)


def rms_norm(x, g):
    xf = x.astype(jnp.float32)
    y = xf * lax.rsqrt(jnp.mean(jnp.square(xf), axis=-1, keepdims=True) + RMS_EPS)
    return (y * g.astype(jnp.float32)).astype(x.dtype)


def rope_tables(seq_len, dtype):
    inv = 1.0 / (ROPE_BASE ** (jnp.arange(0, QK_ROPE_DIM, 2, dtype=jnp.float32) / QK_ROPE_DIM))
    ang = jnp.arange(seq_len, dtype=jnp.float32)[:, None] * inv[None, :]
    return jnp.cos(ang).astype(dtype), jnp.sin(ang).astype(dtype)


def apply_rope(x, cos, sin):
    x1, x2 = jnp.split(x, 2, axis=-1)
    return jnp.concatenate([x1 * cos - x2 * sin, x1 * sin + x2 * cos], axis=-1)


def bidir_mla_attention(q_nope, q_rope, k_nope, k_rope, v):
    B, S, H, _ = q_nope.shape
    nb = S // Q_BLOCK
    scale = 1.0 / math.sqrt(QK_NOPE_DIM + QK_ROPE_DIM)
    qn = q_nope.reshape(B, nb, Q_BLOCK, H, QK_NOPE_DIM).swapaxes(0, 1)
    qr = q_rope.reshape(B, nb, Q_BLOCK, H, QK_ROPE_DIM).swapaxes(0, 1)

    def one_block(args):
        qn_b, qr_b = args
        s = (jnp.einsum('bqhd,bkhd->bhqk', qn_b, k_nope)
             + jnp.einsum('bqhd,bkd->bhqk', qr_b, k_rope))
        p = jax.nn.softmax(s.astype(jnp.float32) * scale, axis=-1).astype(v.dtype)
        return jnp.einsum('bhqk,bkhd->bqhd', p, v)

    o = lax.map(one_block, (qn, qr))
    return o.swapaxes(0, 1).reshape(B, S, H * V_HEAD_DIM)


def token_mixer(h, w_in, q_norm_g, kv_norm_g, w_uq, w_ukv, w_o_attn,
                sgu_ln_g, sgu_ln_b, w_s, b_s, w_o_sgu, w_out):
    B, S, _ = h.shape
    z = h @ w_in
    i1 = Q_LORA_RANK
    i2 = i1 + KV_LORA_RANK
    i3 = i2 + QK_ROPE_DIM
    i4 = i3 + 2 * SGU_WIDTH
    c_q, c_kv, k_rope, uv, gate_logits = jnp.split(z, [i1, i2, i3, i4], axis=-1)

    q = (rms_norm(c_q, q_norm_g) @ w_uq).reshape(B, S, N_HEADS_ATTN, QK_NOPE_DIM + QK_ROPE_DIM)
    q_nope, q_rope = q[..., :QK_NOPE_DIM], q[..., QK_NOPE_DIM:]
    kv = (rms_norm(c_kv, kv_norm_g) @ w_ukv).reshape(B, S, N_HEADS_ATTN, QK_NOPE_DIM + V_HEAD_DIM)
    k_nope, v = kv[..., :QK_NOPE_DIM], kv[..., QK_NOPE_DIM:]
    cos, sin = rope_tables(S, h.dtype)
    q_rope = apply_rope(q_rope, cos[:, None, :], sin[:, None, :])
    k_rope = apply_rope(k_rope, cos, sin)
    o_attn = bidir_mla_attention(q_nope, q_rope, k_nope, k_rope, v) @ w_o_attn

    u, vv = jnp.split(jax.nn.gelu(uv), 2, axis=-1)
    vv = layer_norm(vv, sgu_ln_g, sgu_ln_b)
    vv = vv.reshape(B, S // CHUNK, CHUNK, N_SGU_GROUPS, SGU_GROUP_DIM)
    vv = jnp.einsum('gpq,bnqgc->bnpgc', w_s, vv) + b_s.T[:, :, None]
    o_sgu = (u * vv.reshape(B, S, SGU_WIDTH)) @ w_o_sgu

    g_attn, g_sgu = jnp.split(gate_logits, 2, axis=-1)
    merged = jax.nn.sigmoid(g_attn) * o_attn + jax.nn.sigmoid(g_sgu) * o_sgu
    return merged @ w_out


def clamped_swiglu(x_glu, x_lin):
    x_glu = jnp.minimum(x_glu, SWIGLU_LIMIT)
    x_lin = jnp.clip(x_lin, -SWIGLU_LIMIT, SWIGLU_LIMIT)
    return x_glu * jax.nn.sigmoid(SWIGLU_ALPHA * x_glu) * (x_lin + 1.0)


def moe_ffn(h, router_w, router_b, w_gate_up, b_gate_up, w_down, b_down):
    T, D = h.shape
    logits = (h @ router_w + router_b).astype(jnp.float32)
    top_logits, top_idx = lax.top_k(logits, TOP_K)
    weights = jax.nn.softmax(top_logits, axis=-1)
    TK = T * TOP_K
    flat_e = top_idx.reshape(TK)
    flat_tok = jnp.arange(TK, dtype=jnp.int32) // TOP_K
    flat_w = weights.reshape(TK)
    order = jnp.argsort(flat_e)
    sorted_e = flat_e[order]
    counts = jnp.bincount(flat_e, length=N_EXPERTS)
    start = jnp.cumsum(counts) - counts
    padded = (counts + MOE_BLOCK - 1) // MOE_BLOCK * MOE_BLOCK
    pad_end = jnp.cumsum(padded)
    pad_start = pad_end - padded
    dest = pad_start[sorted_e] + (jnp.arange(TK, dtype=jnp.int32) - start[sorted_e])
    n_blocks = -(-TK // MOE_BLOCK) + N_EXPERTS
    n_rows = n_blocks * MOE_BLOCK
    row_tok = jnp.zeros((n_rows,), jnp.int32).at[dest].set(flat_tok[order])
    row_w = jnp.zeros((n_rows,), jnp.float32).at[dest].set(flat_w[order])
    block_start = jnp.arange(n_blocks, dtype=jnp.int32) * MOE_BLOCK
    block_e = jnp.minimum(jnp.searchsorted(pad_end, block_start, side='right'), N_EXPERTS - 1)
    xb = h[row_tok].reshape(n_blocks, MOE_BLOCK, D)

    def expert_block(args):
        xe, e = args
        gu = (xe @ w_gate_up[e] + b_gate_up[e]).reshape(MOE_BLOCK, D_FF, 2)
        return clamped_swiglu(gu[..., 0], gu[..., 1]) @ w_down[e] + b_down[e]

    yb = lax.map(expert_block, (xb, block_e)).reshape(n_rows, D)
    yb = yb * row_w[:, None].astype(yb.dtype)
    return jax.ops.segment_sum(yb, row_tok, num_segments=T)


def encoder_layers(x, c, p):
    B, S, D = x.shape
    for l in range(DEPTH):
        mod = jax.nn.silu(c) @ p['w_mod'][l] + p['b_mod'][l]
        sh1, sc1, g1, sh2, sc2, g2 = jnp.split(mod[:, None, :], 6, axis=-1)
        h = x * (1.0 + sc1) + sh1
        mix = token_mixer(h, p['w_in'][l], p['q_norm_g'][l], p['kv_norm_g'][l], p['w_uq'][l],
                          p['w_ukv'][l], p['w_o_attn'][l], p['sgu_ln_g'][l], p['sgu_ln_b'][l],
                          p['w_s'][l], p['b_s'][l], p['w_o_sgu'][l], p['w_out'][l])
        x = layer_norm(DN_ALPHA * x + (1.0 + g1) * mix, p['ln1_g'][l], p['ln1_b'][l])
        h = x * (1.0 + sc2) + sh2
        ff = moe_ffn(h.reshape(B * S, D), p['router_w'][l], p['router_b'][l], p['w_gate_up'][l],
                     p['b_gate_up'][l], p['w_down'][l], p['b_down'][l]).reshape(B, S, D)
        x = layer_norm(DN_ALPHA * x + (1.0 + g2) * ff, p['ln2_g'][l], p['ln2_b'][l])
    return x


def setup_inputs(seed: int = 0) -> dict:
    key = jax.random.key(seed)
    ks = jax.random.split(key, 32)
    f32 = jnp.float32

    def nrm(k, shape, scale):
        return jax.random.normal(k, shape, f32) * scale

    L, D, E, F = DEPTH, D_MODEL, N_EXPERTS, D_FF
    return {
        'x_prompt': nrm(ks[0], (BATCH, SEQ, D), 1.0),
        'x_sample': nrm(ks[1], (DEC_BATCH, DEC_SEQ, D), 1.0),
        'c_prompt': nrm(ks[2], (BATCH, D), 1.0),
        'c_sample': nrm(ks[3], (DEC_BATCH, D), 1.0),
        'w_mod': nrm(ks[4], (L, D, 6 * D), 0.1 * D ** -0.5),
        'b_mod': nrm(ks[5], (L, 6 * D), 0.01),
        'w_in': nrm(ks[6], (L, D, IN_WIDTH), D ** -0.5),
        'q_norm_g': 1.0 + nrm(ks[7], (L, Q_LORA_RANK), 0.01),
        'kv_norm_g': 1.0 + nrm(ks[8], (L, KV_LORA_RANK), 0.01),
        'w_uq': nrm(ks[9], (L, Q_LORA_RANK, N_HEADS_ATTN * (QK_NOPE_DIM + QK_ROPE_DIM)), Q_LORA_RANK ** -0.5),
        'w_ukv': nrm(ks[10], (L, KV_LORA_RANK, N_HEADS_ATTN * (QK_NOPE_DIM + V_HEAD_DIM)), KV_LORA_RANK ** -0.5),
        'w_o_attn': nrm(ks[11], (L, ATTN_WIDTH, D), DN_BETA * ATTN_WIDTH ** -0.5),
        'sgu_ln_g': 1.0 + nrm(ks[12], (L, SGU_WIDTH), 0.01),
        'sgu_ln_b': nrm(ks[13], (L, SGU_WIDTH), 0.01),
        'w_s': nrm(ks[14], (L, N_SGU_GROUPS, CHUNK, CHUNK), CHUNK ** -0.5),
        'b_s': 1.0 + nrm(ks[15], (L, N_SGU_GROUPS, CHUNK), 0.01),
        'w_o_sgu': nrm(ks[16], (L, SGU_WIDTH, D), DN_BETA * SGU_WIDTH ** -0.5),
        'w_out': nrm(ks[17], (L, D, D), DN_BETA * D ** -0.5),
        'ln1_g': 1.0 + nrm(ks[18], (L, D), 0.01),
        'ln1_b': nrm(ks[19], (L, D), 0.01),
        'router_w': nrm(ks[20], (L, D, E), D ** -0.5),
        'router_b': nrm(ks[21], (L, E), 0.01),
        'w_gate_up': nrm(ks[22], (L, E, D, 2 * F), D ** -0.5),
        'b_gate_up': nrm(ks[23], (L, E, 2 * F), 0.01),
        'w_down': nrm(ks[24], (L, E, F, D), DN_BETA * F ** -0.5),
        'b_down': nrm(ks[25], (L, E, D), 0.01),
        'ln2_g': 1.0 + nrm(ks[26], (L, D), 0.01),
        'ln2_b': nrm(ks[27], (L, D), 0.01),
    }


def reference(x_prompt, x_sample, c_prompt, c_sample, w_mod, b_mod, w_in, q_norm_g, kv_norm_g,
              w_uq, w_ukv, w_o_attn, sgu_ln_g, sgu_ln_b, w_s, b_s, w_o_sgu, w_out, ln1_g, ln1_b,
              router_w, router_b, w_gate_up, b_gate_up, w_down, b_down, ln2_g, ln2_b):
    params = dict(w_mod=w_mod, b_mod=b_mod, w_in=w_in, q_norm_g=q_norm_g, kv_norm_g=kv_norm_g,
                  w_uq=w_uq, w_ukv=w_ukv, w_o_attn=w_o_attn, sgu_ln_g=sgu_ln_g, sgu_ln_b=sgu_ln_b,
                  w_s=w_s, b_s=b_s, w_o_sgu=w_o_sgu, w_out=w_out, ln1_g=ln1_g, ln1_b=ln1_b,
                  router_w=router_w, router_b=router_b, w_gate_up=w_gate_up, b_gate_up=b_gate_up,
                  w_down=w_down, b_down=b_down, ln2_g=ln2_g, ln2_b=ln2_b)
    y_prompt = encoder_layers(x_prompt, c_prompt, params)
    y_sample = encoder_layers(x_sample, c_sample, params)
    return (y_prompt, y_sample)
```

```python
import functools
import math

import jax
import jax.numpy as jnp
from jax import lax
from jax.experimental import pallas as pl
from jax.experimental.pallas import tpu as pltpu
from jax.experimental.pallas import tpu_sc as plsc

F32 = jnp.float32
BF16 = jnp.bfloat16
U32 = jnp.uint32
I32 = jnp.int32

N_HEADS = 8
QK_NOPE = 64
QK_ROPE = 32
V_DIM = 64
Q_RANK = 256
KV_RANK = 128
ROPE_BASE = 10000.0
CHUNK = 128
SGU_WIDTH = 512
SGU_GROUPS = 8
N_EXPERTS = 32
TOP_K = 4
SWIGLU_LIMIT = 7.0
SWIGLU_ALPHA = 1.702
LN_EPS = 1e-5
RMS_EPS = 1e-6
ATTN_SCALE = 1.0 / math.sqrt(QK_NOPE + QK_ROPE)

LANES = 128
LAT = KV_RANK
QK_PAD = 2 * LANES
MOE_ROWS = 512
VMEM_LIMIT = 56 * 1024 * 1024
SC_CORES = 2
SC_SUBCORES = 16
SC_WORKERS = SC_CORES * SC_SUBCORES
SC_CHUNK = 64


def _cparams(sem, vmem=VMEM_LIMIT):
    return pltpu.CompilerParams(dimension_semantics=sem, vmem_limit_bytes=vmem)


def _pick(n, prefs):
    for p in prefs:
        if n % p == 0:
            return p
    raise ValueError(f"no tile in {prefs} divides {n}")


def _const_spec(shape):
    nd = len(shape)
    return pl.BlockSpec(shape, lambda *_: (0,) * nd)


def _dot(a, b):
    return jnp.dot(a, b, preferred_element_type=F32)


def _layer_norm(y, g, b):
    mu = jnp.mean(y, axis=-1, keepdims=True)
    d = y - mu
    var = jnp.mean(d * d, axis=-1, keepdims=True)
    return d * lax.rsqrt(var + LN_EPS) * g + b


def _rms_norm(y, g):
    return y * lax.rsqrt(jnp.mean(y * y, axis=-1, keepdims=True) + RMS_EPS) * g


def _pack_bf16_pairs(y):
    w = y.shape[1] // 2
    bits = lax.bitcast_convert_type(y.astype(BF16).astype(F32), U32)
    return (bits[:, w:] & jnp.uint32(0xFFFF0000)) | (bits[:, :w] >> 16)


def _unpack_bf16_pairs(p):
    lo = lax.bitcast_convert_type(p << 16, F32)
    hi = lax.bitcast_convert_type(p & jnp.uint32(0xFFFF0000), F32)
    return lo, hi


class _Groups:
    def __init__(self, bp, sp, bs, ss):
        self.bp, self.sp, self.bs, self.ss = bp, sp, bs, ss
        self.tp, self.ts = bp * sp, bs * ss
        self.t = self.tp + self.ts

    def seq_of_tile(self, i, tm):
        npt = self.tp // tm
        return jnp.where(i < npt, (i * tm) // self.sp, self.bp + ((i - npt) * tm) // self.ss)

    def pos_tile(self, i, tm):
        npt = self.tp // tm
        return jnp.where(i < npt, i % (self.sp // tm), (i - npt) % (self.ss // tm))


def _mod_kernel(c_ref, w_ref, b_ref, o_ref):
    c = c_ref[...]
    a = c * jax.nn.sigmoid(c)
    o_ref[0] = jnp.dot(a, w_ref[0], preferred_element_type=F32,
                       precision=lax.Precision.HIGHEST) + b_ref[0]


def _modulation(c_all, w_mod, b_mod):
    depth, d, six_d = w_mod.shape
    ns = c_all.shape[0]
    n_col = six_d // d
    out = pl.pallas_call(
        _mod_kernel,
        grid=(depth, n_col),
        in_specs=[
            pl.BlockSpec((ns, d), lambda l, j: (0, 0)),
            pl.BlockSpec((1, d, d), lambda l, j: (l, 0, j)),
            pl.BlockSpec((1, 1, d), lambda l, j: (l, 0, j)),
        ],
        out_specs=pl.BlockSpec((1, ns, d), lambda l, j: (l, 0, j)),
        out_shape=jax.ShapeDtypeStruct((depth, ns, six_d), F32),
        compiler_params=_cparams(("parallel", "parallel")),
        name="modulation",
    )(c_all, w_mod, b_mod.reshape(depth, 1, six_d))
    return out.reshape(depth, ns, n_col, d)


def _bmm_kernel(a_ref, b_ref, o_ref):
    o_ref[0] = jnp.dot(a_ref[0], b_ref[0], preferred_element_type=F32,
                       precision=lax.Precision.HIGHEST)


def _bmm(a, b):
    h, m, k = a.shape
    n = b.shape[2]
    return pl.pallas_call(
        _bmm_kernel,
        grid=(h,),
        in_specs=[pl.BlockSpec((1, m, k), lambda i: (i, 0, 0)),
                  pl.BlockSpec((1, k, n), lambda i: (i, 0, 0))],
        out_specs=pl.BlockSpec((1, m, n), lambda i: (i, 0, 0)),
        out_shape=jax.ShapeDtypeStruct((h, m, n), F32),
        compiler_params=_cparams(("parallel",)),
        name="weight_fold",
    )(a, b)


def _gelu_tanh(x):
    c = math.sqrt(2.0 / math.pi)
    return 0.5 * x * (1.0 + jnp.tanh(c * (x + 0.044715 * (x * x * x))))


def _inproj_kernel(x_ref, mod_ref, cos_ref, sin_ref, wa_ref, wuv_ref, wg_ref, gq_ref, gkv_ref,
                   wql_ref, wqr_ref, wqrs_ref, lng_ref, lnb_ref, wsp_ref, bs_ref,
                   q_ref, klat_ref, sgu_ref, sg_ref):
    tm = x_ref.shape[0]
    x = x_ref[...]
    sh1 = mod_ref[0:1, :]
    sc1 = mod_ref[1:2, :]
    h = (x * (1.0 + sc1) + sh1).astype(BF16)

    za = _dot(h, wa_ref[...])
    cq = za[:, 0:Q_RANK]
    ckv = za[:, Q_RANK:Q_RANK + KV_RANK]
    kr = za[:, Q_RANK + KV_RANK:Q_RANK + KV_RANK + LANES]
    krs = za[:, Q_RANK + KV_RANK + LANES:Q_RANK + KV_RANK + 2 * LANES]
    cos = cos_ref[...]
    sin = sin_ref[...]
    cqn = _rms_norm(cq, gq_ref[...]).astype(BF16)
    klat_ref[:, 0:LAT] = _rms_norm(ckv, gkv_ref[...]).astype(BF16)
    klat_ref[:, LAT:QK_PAD] = (kr * cos + krs * sin).astype(BF16)

    ql = _dot(cqn, wql_ref[...])
    qr = _dot(cqn, wqr_ref[...])
    qrs = _dot(cqn, wqrs_ref[...])
    cos_q = cos * ATTN_SCALE
    sin_q = sin * ATTN_SCALE
    for hd in range(N_HEADS):
        sl = slice(hd * LANES, (hd + 1) * LANES)
        q_ref[hd, :, 0:LAT] = (ql[:, sl] * ATTN_SCALE).astype(BF16)
        q_ref[hd, :, LAT:QK_PAD] = (qr[:, sl] * cos_q + qrs[:, sl] * sin_q).astype(BF16)

    sg_ref[...] = jax.nn.sigmoid(_dot(h, wg_ref[...])).astype(BF16)

    guv = _gelu_tanh(_dot(h, wuv_ref[...]))
    u = guv[:, :SGU_WIDTH]
    vn = _layer_norm(guv[:, SGU_WIDTH:], lng_ref[...], lnb_ref[...]).astype(BF16)
    lane = lax.broadcasted_iota(I32, (CHUNK, LANES), 1)
    low_half = lane < (LANES // 2)
    zero = jnp.zeros((CHUNK, LANES), BF16)
    for c in range(tm // CHUNK):
        rows = slice(c * CHUNK, (c + 1) * CHUNK)
        vc = vn[rows, :]
        mixed = []
        for j in range(SGU_WIDTH // LANES):
            blk = vc[:, j * LANES:(j + 1) * LANES]
            rhs = jnp.concatenate([jnp.where(low_half, blk, zero), jnp.where(low_half, zero, blk)], axis=0)
            mixed.append(_dot(wsp_ref[j], rhs))
        mixed = jnp.concatenate(mixed, axis=1) + bs_ref[...]
        sgu_ref[rows, :] = (u[rows, :] * mixed).astype(BF16)


def _inproj(x, mod_l, cos_t, sin_t, w, g, tm):
    t, d = x.shape
    n = t // tm
    seq = lambda i: (g.seq_of_tile(i, tm), 0, 0)
    pos = lambda i: (g.pos_tile(i, tm), 0)
    row = lambda i: (i, 0)
    consts = [w["wa"], w["wuv"], w["wg"], w["gq"], w["gkv"], w["wql"], w["wqr"], w["wqrs"],
              w["lng"], w["lnb"], w["wsp"], w["bs"]]
    return pl.pallas_call(
        _inproj_kernel,
        grid=(n,),
        in_specs=[pl.BlockSpec((tm, d), row),
                  pl.BlockSpec((None, mod_l.shape[1], d), seq),
                  pl.BlockSpec((tm, LANES), pos),
                  pl.BlockSpec((tm, LANES), pos)] + [_const_spec(c.shape) for c in consts],
        out_specs=[pl.BlockSpec((N_HEADS, tm, QK_PAD), lambda i: (0, i, 0)),
                   pl.BlockSpec((tm, QK_PAD), row),
                   pl.BlockSpec((tm, SGU_WIDTH), row),
                   pl.BlockSpec((tm, 2 * d), row)],
        out_shape=[jax.ShapeDtypeStruct((N_HEADS, t, QK_PAD), BF16),
                   jax.ShapeDtypeStruct((t, QK_PAD), BF16),
                   jax.ShapeDtypeStruct((t, SGU_WIDTH), BF16),
                   jax.ShapeDtypeStruct((t, 2 * d), BF16)],
        compiler_params=_cparams(("parallel",)),
        name="token_mixer_in",
    )(x, mod_l, cos_t, sin_t, *consts)


def _attn_kernel(q_ref, k_ref, o_ref, m_ref, l_ref, acc_ref):
    ki = pl.program_id(2)
    nh, tq, _ = q_ref.shape

    @pl.when(ki == 0)
    def _():
        m_ref[...] = jnp.full(m_ref.shape, -jnp.inf, F32)
        l_ref[...] = jnp.zeros(l_ref.shape, F32)
        acc_ref[...] = jnp.zeros(acc_ref.shape, F32)

    q = q_ref[...].reshape(nh * tq, QK_PAD)
    k = k_ref[...]
    s = lax.dot_general(q, k, (((1,), (1,)), ((), ())), preferred_element_type=F32)
    m_prev = m_ref[...]
    m_new = jnp.maximum(m_prev, jnp.max(s, axis=-1, keepdims=True))
    alpha = jnp.exp(m_prev - m_new)
    p = jnp.exp(s - m_new)
    l_ref[...] = alpha * l_ref[...] + jnp.sum(p, axis=-1, keepdims=True)
    acc_ref[...] = alpha * acc_ref[...] + _dot(p.astype(BF16), k[:, 0:LAT])
    m_ref[...] = m_new

    @pl.when(ki == pl.num_programs(2) - 1)
    def _():
        o = acc_ref[...] / l_ref[...]
        for hd in range(nh):
            o_ref[:, hd * LAT:(hd + 1) * LAT] = o[hd * tq:(hd + 1) * tq, :].astype(BF16)


def _attention(q, klat, row0, nb, s, tq, tk):
    nq, nk = s // tq, s // tk
    q0, k0 = row0 // tq, row0 // tk
    m_rows = N_HEADS * tq
    return pl.pallas_call(
        _attn_kernel,
        grid=(nb, nq, nk),
        in_specs=[pl.BlockSpec((N_HEADS, tq, QK_PAD), lambda b, i, j: (0, q0 + b * nq + i, 0)),
                  pl.BlockSpec((tk, QK_PAD), lambda b, i, j: (k0 + b * nk + j, 0))],
        out_specs=pl.BlockSpec((tq, N_HEADS * LAT), lambda b, i, j: (b * nq + i, 0)),
        out_shape=jax.ShapeDtypeStruct((nb * s, N_HEADS * LAT), BF16),
        scratch_shapes=[pltpu.VMEM((m_rows, 1), F32), pltpu.VMEM((m_rows, 1), F32),
                        pltpu.VMEM((m_rows, LAT), F32)],
        compiler_params=_cparams(("parallel", "parallel", "arbitrary")),
        name="latent_attention",
    )(q, klat)


def _outproj_kernel(npt, op_ref, os_ref, sgu_ref, sg_ref, x_ref, mod_ref, wfold_ref, wosgu_ref, wout_ref,
                    lng_ref, lnb_ref, rwt_ref, rb_ref, x1_ref, h2_ref, idx_ref, wts_ref, *, dn_alpha):
    i = pl.program_id(0)
    d = x_ref.shape[1]
    tm = x_ref.shape[0]
    o = jnp.where(i < npt, op_ref[...], os_ref[...])
    o_attn = _dot(o, wfold_ref[...])
    o_sgu = _dot(sgu_ref[...], wosgu_ref[...])
    sg = sg_ref[...].astype(F32)
    merged = (sg[:, :d] * o_attn + sg[:, d:] * o_sgu).astype(BF16)
    mix = _dot(merged, wout_ref[...])
    g1 = mod_ref[2:3, :]
    sh2 = mod_ref[3:4, :]
    sc2 = mod_ref[4:5, :]
    x1 = _layer_norm(dn_alpha * x_ref[...] + (1.0 + g1) * mix, lng_ref[...], lnb_ref[...])
    x1_ref[...] = x1
    h2 = x1 * (1.0 + sc2) + sh2
    h2_ref[...] = _pack_bf16_pairs(h2)

    logits = lax.dot_general(rwt_ref[...], h2, (((1,), (1,)), ((), ())), preferred_element_type=F32,
                             precision=lax.Precision.HIGHEST) + rb_ref[...]
    e_iota = lax.broadcasted_iota(I32, (N_EXPERTS, tm), 0)
    vals, idxs = [], []
    for _ in range(TOP_K):
        mx = jnp.max(logits, axis=0, keepdims=True)
        ix = jnp.min(jnp.where(logits == mx, e_iota, N_EXPERTS), axis=0, keepdims=True)
        vals.append(mx)
        idxs.append(ix)
        logits = jnp.where(e_iota == ix, -jnp.inf, logits)
    v = jnp.concatenate(vals, axis=0)
    w = jnp.exp(v - v[0:1, :])
    wts_ref[...] = w / jnp.sum(w, axis=0, keepdims=True)
    idx_ref[...] = jnp.concatenate(idxs, axis=0)


def _outproj(o_p, o_s, sgu, sg, x, mod_l, w, g, tm, dn_alpha):
    t, d = x.shape
    n = t // tm
    npt = g.tp // tm
    nst = g.ts // tm
    seq = lambda i: (g.seq_of_tile(i, tm), 0, 0)
    row = lambda i: (i, 0)
    col = lambda i: (0, i)
    consts = [w["wfold"], w["wosgu"], w["wout"], w["ln1g"], w["ln1b"], w["rwt"], w["rb"]]
    return pl.pallas_call(
        functools.partial(_outproj_kernel, npt, dn_alpha=dn_alpha),
        grid=(n,),
        in_specs=[pl.BlockSpec((tm, N_HEADS * LAT), lambda i: (jnp.minimum(i, npt - 1), 0)),
                  pl.BlockSpec((tm, N_HEADS * LAT), lambda i: (jnp.clip(i - npt, 0, nst - 1), 0)),
                  pl.BlockSpec((tm, SGU_WIDTH), row),
                  pl.BlockSpec((tm, 2 * d), row),
                  pl.BlockSpec((tm, d), row),
                  pl.BlockSpec((None, mod_l.shape[1], d), seq)] + [_const_spec(c.shape) for c in consts],
        out_specs=[pl.BlockSpec((tm, d), row),
                   pl.BlockSpec((tm, d // 2), row),
                   pl.BlockSpec((TOP_K, tm), col),
                   pl.BlockSpec((TOP_K, tm), col)],
        out_shape=[jax.ShapeDtypeStruct((t, d), F32),
                   jax.ShapeDtypeStruct((t, d // 2), U32),
                   jax.ShapeDtypeStruct((TOP_K, t), I32),
                   jax.ShapeDtypeStruct((TOP_K, t), F32)],
        compiler_params=_cparams(("parallel",)),
        name="token_mixer_out",
    )(o_p, o_s, sgu, sg, x, mod_l, *consts)


def _one_hots(idx, tm):
    e_iota = lax.broadcasted_iota(I32, (N_EXPERTS, tm), 0)
    return [(e_iota == idx[k:k + 1, :]).astype(F32) for k in range(TOP_K)]


def _count_kernel(idx_ref, cnt_ref):
    @pl.when(pl.program_id(0) == 0)
    def _():
        cnt_ref[...] = jnp.zeros(cnt_ref.shape, F32)

    oh = sum(_one_hots(idx_ref[...], idx_ref.shape[1]))
    cnt_ref[...] += jnp.sum(oh, axis=1, keepdims=True)


def _dest_kernel(idx_ref, start_ref, dest_ref, carry_ref):
    tm = idx_ref.shape[1]

    @pl.when(pl.program_id(0) == 0)
    def _():
        carry_ref[...] = start_ref[...]

    ohs = _one_hots(idx_ref[...], tm)
    oh = sum(ohs)
    r = lax.broadcasted_iota(I32, (tm, tm), 0)
    c = lax.broadcasted_iota(I32, (tm, tm), 1)
    before = (r < c).astype(BF16)
    rank = _dot(oh.astype(BF16), before) + carry_ref[:, 0:1]
    dest_ref[...] = jnp.concatenate(
        [jnp.sum(o * rank, axis=0, keepdims=True) for o in ohs], axis=0).astype(I32)
    carry_ref[...] += jnp.sum(oh, axis=1, keepdims=True)


def _route(idx, tm):
    k, t = idx.shape
    n = t // tm
    col = lambda i: (0, i)
    cnt = pl.pallas_call(
        _count_kernel,
        grid=(n,),
        in_specs=[pl.BlockSpec((k, tm), col)],
        out_specs=_const_spec((N_EXPERTS, LANES)),
        out_shape=jax.ShapeDtypeStruct((N_EXPERTS, LANES), F32),
        compiler_params=_cparams(("arbitrary",)),
        name="expert_counts",
    )(idx)
    counts = cnt[:, 0].astype(I32)
    padded = (counts + MOE_ROWS - 1) // MOE_ROWS * MOE_ROWS
    pad_end = jnp.cumsum(padded)
    pad_start = pad_end - padded
    start = jnp.broadcast_to(pad_start.astype(F32)[:, None], (N_EXPERTS, LANES))
    dest = pl.pallas_call(
        _dest_kernel,
        grid=(n,),
        in_specs=[pl.BlockSpec((k, tm), col), _const_spec((N_EXPERTS, LANES))],
        out_specs=pl.BlockSpec((k, tm), col),
        out_shape=jax.ShapeDtypeStruct((k, t), I32),
        scratch_shapes=[pltpu.VMEM((N_EXPERTS, LANES), F32)],
        compiler_params=_cparams(("arbitrary",)),
        name="expert_dest_rows",
    )(idx, start)
    return dest, counts, padded, pad_start, pad_end


def _padding_rows(counts, padded, pad_start, pad_end, n_pad):
    per = padded - counts
    cum = jnp.cumsum(per)
    j = jnp.arange(n_pad, dtype=I32)
    e = jnp.searchsorted(cum, j, side="right").astype(I32)
    ec = jnp.minimum(e, N_EXPERTS - 1)
    inside = pad_start[ec] + counts[ec] + (j - (cum[ec] - per[ec]))
    tail = pad_end[-1] + (j - cum[-1])
    return jnp.where(e < N_EXPERTS, inside, tail)


def _sc_mesh():
    return plsc.VectorSubcoreMesh(core_axis_name="c", subcore_axis_name="s")


def _sc_scatter_rows(src, idx_main, idx_pad, n_out):
    nw, cpk, c = idx_main.shape
    width = src.shape[1]
    cpw = src.shape[0] // (nw * c)
    copies = cpk // cpw
    ppw = idx_pad.shape[1]

    @functools.partial(
        pl.kernel, mesh=_sc_mesh(),
        out_type=jax.ShapeDtypeStruct((n_out, width), src.dtype),
        scratch_types=[pltpu.VMEM((cpk, c), I32), pltpu.VMEM((ppw, c), I32),
                       pltpu.VMEM((c, width), src.dtype), pltpu.SemaphoreType.DMA],
        name="dispatch_rows")
    def run(src_hbm, idx_hbm, pad_hbm, out_hbm, idx_v, pad_v, rows_v, sem):
        wid = lax.axis_index("s") * SC_CORES + lax.axis_index("c")
        pltpu.sync_copy(idx_hbm.at[wid], idx_v)
        pltpu.sync_copy(pad_hbm.at[wid], pad_v)

        @pl.loop(0, cpw)
        def _(j):
            pltpu.sync_copy(src_hbm.at[pl.ds((wid * cpw + j) * c, c)], rows_v)
            cps = [pltpu.async_copy(rows_v, out_hbm.at[idx_v.at[j * copies + kk]], sem)
                   for kk in range(copies)]
            for cp in cps:
                cp.wait()

        n_src_chunks = src.shape[0] // c

        @pl.loop(0, ppw)
        def _(j):
            pltpu.sync_copy(src_hbm.at[pl.ds(((wid * ppw + j) % n_src_chunks) * c, c)], rows_v)
            pltpu.async_copy(rows_v, out_hbm.at[pad_v.at[j]], sem).wait()

    return run(src, idx_main, idx_pad)


def _sc_gather_rows(src, idx):
    nw, cpw, c = idx.shape
    width = src.shape[1]

    @functools.partial(
        pl.kernel, mesh=_sc_mesh(),
        out_type=jax.ShapeDtypeStruct((nw * cpw * c, width), src.dtype),
        scratch_types=[pltpu.VMEM((cpw, c), I32), pltpu.VMEM((c, width), src.dtype),
                       pltpu.SemaphoreType.DMA],
        name="collect_rows")
    def run(src_hbm, idx_hbm, out_hbm, idx_v, rows_v, sem):
        wid = lax.axis_index("s") * SC_CORES + lax.axis_index("c")
        pltpu.sync_copy(idx_hbm.at[wid], idx_v)

        @pl.loop(0, cpw)
        def _(j):
            pltpu.async_copy(src_hbm.at[idx_v.at[j]], rows_v, sem).wait()
            pltpu.sync_copy(rows_v, out_hbm.at[pl.ds((wid * cpw + j) * c, c)])

    return run(src, idx)


def _dispatch(h2, dest, pad_rows, n_rows):
    k, t = dest.shape
    cpw = t // (SC_WORKERS * SC_CHUNK)
    idx_main = dest.reshape(k, SC_WORKERS, cpw, SC_CHUNK).transpose(1, 2, 0, 3)
    idx_main = idx_main.reshape(SC_WORKERS, cpw * k, SC_CHUNK)
    idx_pad = pad_rows.reshape(SC_WORKERS, -1, SC_CHUNK)
    src = lax.bitcast_convert_type(h2, I32)
    out = _sc_scatter_rows(src, idx_main, idx_pad, n_rows)
    return lax.bitcast_convert_type(out, U32)


def _collect(y, dest):
    k, t = dest.shape
    idx = dest.reshape(SC_WORKERS, (k * t) // (SC_WORKERS * SC_CHUNK), SC_CHUNK)
    out = _sc_gather_rows(lax.bitcast_convert_type(y, I32), idx)
    return lax.bitcast_convert_type(out, U32).reshape(k, t, y.shape[1])


def _moe_kernel(be_ref, x_ref, wg_ref, wu_ref, bg_ref, bu_ref, wd_ref, bd_ref, y_ref):
    lo, hi = _unpack_bf16_pairs(x_ref[...])
    x = jnp.concatenate([lo, hi], axis=1).astype(BF16)
    glu = jnp.minimum(_dot(x, wg_ref[0]) + bg_ref[0], SWIGLU_LIMIT)
    lin = jnp.clip(_dot(x, wu_ref[0]) + bu_ref[0], -SWIGLU_LIMIT, SWIGLU_LIMIT)
    act = (glu * jax.nn.sigmoid(SWIGLU_ALPHA * glu) * (lin + 1.0)).astype(BF16)
    y_ref[...] = _pack_bf16_pairs(_dot(act, wd_ref[0]) + bd_ref[0])


def _moe_ffn(xs, block_e, w):
    n_rows, half = xs.shape
    d = 2 * half
    f = w["wgate"].shape[2]
    n_blocks = n_rows // MOE_ROWS
    row = lambda i, be: (i, 0)
    exp = lambda i, be: (be[i], 0, 0)
    return pl.pallas_call(
        _moe_kernel,
        grid_spec=pltpu.PrefetchScalarGridSpec(
            num_scalar_prefetch=1,
            grid=(n_blocks,),
            in_specs=[pl.BlockSpec((MOE_ROWS, half), row),
                      pl.BlockSpec((1, d, f), exp), pl.BlockSpec((1, d, f), exp),
                      pl.BlockSpec((1, 1, f), exp), pl.BlockSpec((1, 1, f), exp),
                      pl.BlockSpec((1, f, d), exp), pl.BlockSpec((1, 1, d), exp)],
            out_specs=pl.BlockSpec((MOE_ROWS, half), row)),
        out_shape=jax.ShapeDtypeStruct((n_rows, half), U32),
        compiler_params=_cparams(("arbitrary",)),
        name="expert_ffn",
    )(block_e, xs, w["wgate"], w["wup"], w["bgate"], w["bup"], w["wdown"], w["bdown"])


def _combine_kernel(yg_ref, wts_ref, x1_ref, mod_ref, lng_ref, lnb_ref, x2_ref, *, dn_alpha):
    wts = wts_ref[...]
    lo = hi = None
    for k in range(TOP_K):
        l_k, h_k = _unpack_bf16_pairs(yg_ref[k])
        w_k = wts[:, k:k + 1]
        lo = l_k * w_k if lo is None else lo + l_k * w_k
        hi = h_k * w_k if hi is None else hi + h_k * w_k
    ff = jnp.concatenate([lo, hi], axis=1)
    g2 = mod_ref[5:6, :]
    x2_ref[...] = _layer_norm(dn_alpha * x1_ref[...] + (1.0 + g2) * ff, lng_ref[...], lnb_ref[...])


def _combine(yg, wts_t, x1, mod_l, ln_g, ln_b, g, tm, dn_alpha):
    t, d = x1.shape
    row = lambda i: (i, 0)
    seq = lambda i: (g.seq_of_tile(i, tm), 0, 0)
    return pl.pallas_call(
        functools.partial(_combine_kernel, dn_alpha=dn_alpha),
        grid=(t // tm,),
        in_specs=[pl.BlockSpec((TOP_K, tm, d // 2), lambda i: (0, i, 0)),
                  pl.BlockSpec((tm, TOP_K), row),
                  pl.BlockSpec((tm, d), row),
                  pl.BlockSpec((None, mod_l.shape[1], d), seq),
                  _const_spec(ln_g.shape), _const_spec(ln_b.shape)],
        out_specs=pl.BlockSpec((tm, d), row),
        out_shape=jax.ShapeDtypeStruct((t, d), F32),
        compiler_params=_cparams(("parallel",)),
        name="moe_combine",
    )(yg, wts_t, x1, mod_l, ln_g, ln_b)


def _rope_tables(seq_len):
    inv = 1.0 / (ROPE_BASE ** (jnp.arange(0, QK_ROPE, 2, dtype=F32) / QK_ROPE))
    ang = jnp.arange(seq_len, dtype=F32)[:, None] * inv[None, :]
    pad = jnp.zeros((seq_len, LANES - QK_ROPE), F32)
    cos = jnp.concatenate([jnp.cos(ang), jnp.cos(ang), pad], axis=1)
    sin = jnp.concatenate([jnp.sin(ang), jnp.sin(ang), pad], axis=1)
    return cos, sin


def _rot_pair(w):
    half = w.shape[-1] // 2
    return jnp.concatenate([-w[..., half:], w[..., :half]], axis=-1)


def _pad_lanes(w):
    return jnp.pad(w, [(0, 0)] * (w.ndim - 1) + [(0, LANES - w.shape[-1])])


def _prep_layer(l, p):
    d = p["w_in"].shape[1]
    w_in = p["w_in"][l]
    i1 = Q_RANK
    i2 = i1 + KV_RANK
    i3 = i2 + QK_ROPE
    i4 = i3 + 2 * SGU_WIDTH
    w_kr = w_in[:, i2:i3]
    w = {}
    w["wa"] = jnp.concatenate([w_in[:, :i2], _pad_lanes(w_kr), _pad_lanes(_rot_pair(w_kr))], axis=1).astype(BF16)
    w["wuv"] = w_in[:, i3:i4].astype(BF16)
    w["wg"] = w_in[:, i4:].astype(BF16)
    w["gq"] = p["q_norm_g"][l][None, :]
    w["gkv"] = p["kv_norm_g"][l][None, :]

    w_uq = p["w_uq"][l].reshape(Q_RANK, N_HEADS, QK_NOPE + QK_ROPE)
    w_ukv = p["w_ukv"][l].reshape(KV_RANK, N_HEADS, QK_NOPE + V_DIM)
    uq_nope = w_uq[:, :, :QK_NOPE].transpose(1, 0, 2)
    uk_t = w_ukv[:, :, :QK_NOPE].transpose(1, 2, 0)
    w["wql"] = _bmm(uq_nope, uk_t).transpose(1, 0, 2).reshape(Q_RANK, N_HEADS * LAT).astype(BF16)
    uq_rope = w_uq[:, :, QK_NOPE:]
    w["wqr"] = _pad_lanes(uq_rope).reshape(Q_RANK, N_HEADS * LANES).astype(BF16)
    w["wqrs"] = _pad_lanes(_rot_pair(uq_rope)).reshape(Q_RANK, N_HEADS * LANES).astype(BF16)
    uv = w_ukv[:, :, QK_NOPE:].transpose(1, 0, 2)
    wo = p["w_o_attn"][l].reshape(N_HEADS, V_DIM, d)
    w["wfold"] = _bmm(uv, wo).reshape(N_HEADS * LAT, d).astype(BF16)

    w["lng"] = p["sgu_ln_g"][l][None, :]
    w["lnb"] = p["sgu_ln_b"][l][None, :]
    gpl = LANES // (SGU_WIDTH // SGU_GROUPS)
    ws = p["w_s"][l].reshape(SGU_GROUPS // gpl, gpl, CHUNK, CHUNK)
    w["wsp"] = ws.transpose(0, 2, 1, 3).reshape(SGU_GROUPS // gpl, CHUNK, gpl * CHUNK).astype(BF16)
    w["bs"] = jnp.repeat(p["b_s"][l].T, SGU_WIDTH // SGU_GROUPS, axis=1)
    w["wosgu"] = p["w_o_sgu"][l].astype(BF16)
    w["wout"] = p["w_out"][l].astype(BF16)
    w["ln1g"] = p["ln1_g"][l][None, :]
    w["ln1b"] = p["ln1_b"][l][None, :]
    w["rwt"] = p["router_w"][l].T
    w["rb"] = p["router_b"][l][:, None]
    wgu = p["w_gate_up"][l]
    w["wgate"] = wgu[:, :, 0::2].astype(BF16)
    w["wup"] = wgu[:, :, 1::2].astype(BF16)
    bgu = p["b_gate_up"][l]
    w["bgate"] = bgu[:, None, 0::2]
    w["bup"] = bgu[:, None, 1::2]
    w["wdown"] = p["w_down"][l].astype(BF16)
    w["bdown"] = p["b_down"][l][:, None, :]
    w["ln2g"] = p["ln2_g"][l][None, :]
    w["ln2b"] = p["ln2_b"][l][None, :]
    return w


def _moe_layer(h2, idx, wts, x1, mod_l, w, g, tm, dn_alpha):
    t = x1.shape[0]
    n_pad = N_EXPERTS * MOE_ROWS
    n_rows = TOP_K * t + n_pad
    dest, counts, padded, pad_start, pad_end = _route(idx, _pick(t, (512, 256, 128)))
    pad_rows = _padding_rows(counts, padded, pad_start, pad_end, n_pad)
    block_start = jnp.arange(n_rows // MOE_ROWS, dtype=I32) * MOE_ROWS
    block_e = jnp.minimum(jnp.searchsorted(pad_end, block_start, side="right"), N_EXPERTS - 1).astype(I32)
    xs = _dispatch(h2, dest, pad_rows, n_rows)
    y = _moe_ffn(xs, block_e, w)
    yg = _collect(y, dest)
    return _combine(yg, wts.T, x1, mod_l, w["ln2g"], w["ln2b"], g, tm, dn_alpha)


def kernel(x_prompt, x_sample, c_prompt, c_sample, w_mod, b_mod, w_in, q_norm_g, kv_norm_g, w_uq, w_ukv, w_o_attn, sgu_ln_g, sgu_ln_b, w_s, b_s, w_o_sgu, w_out, ln1_g, ln1_b, router_w, router_b, w_gate_up, b_gate_up, w_down, b_down, ln2_g, ln2_b):
    p = dict(w_in=w_in, q_norm_g=q_norm_g, kv_norm_g=kv_norm_g, w_uq=w_uq, w_ukv=w_ukv, w_o_attn=w_o_attn,
             sgu_ln_g=sgu_ln_g, sgu_ln_b=sgu_ln_b, w_s=w_s, b_s=b_s, w_o_sgu=w_o_sgu, w_out=w_out,
             ln1_g=ln1_g, ln1_b=ln1_b, router_w=router_w, router_b=router_b, w_gate_up=w_gate_up,
             b_gate_up=b_gate_up, w_down=w_down, b_down=b_down, ln2_g=ln2_g, ln2_b=ln2_b)
    depth = w_mod.shape[0]
    bp, sp, d = x_prompt.shape
    bs, ss, _ = x_sample.shape
    g = _Groups(bp, sp, bs, ss)
    dn_alpha = (2 * depth) ** 0.25

    tm = _pick(math.gcd(sp, ss), (512, 256, 128))
    tq = _pick(math.gcd(sp, ss), (256, 128))
    tk_p = _pick(sp, (1024, 512, 256, 128))
    tk_s = _pick(ss, (1024, 512, 256, 128))
    assert g.t % (SC_WORKERS * SC_CHUNK) == 0 and (N_EXPERTS * MOE_ROWS) % (SC_WORKERS * SC_CHUNK) == 0

    n_seq = bp + bs
    ns_pad = -(-n_seq // 8) * 8
    c_all = jnp.pad(jnp.concatenate([c_prompt, c_sample], axis=0), ((0, ns_pad - n_seq), (0, 0)))
    mod = _modulation(c_all, w_mod, b_mod)
    cos_t, sin_t = _rope_tables(max(sp, ss))
    x = jnp.concatenate([x_prompt.reshape(g.tp, d), x_sample.reshape(g.ts, d)], axis=0)

    for l in range(depth):
        w = _prep_layer(l, p)
        q, klat, sgu, sg = _inproj(x, mod[l], cos_t, sin_t, w, g, tm)
        o_p = _attention(q, klat, 0, bp, sp, tq, tk_p)
        o_s = _attention(q, klat, g.tp, bs, ss, tq, tk_s)
        x1, h2, idx, wts = _outproj(o_p, o_s, sgu, sg, x, mod[l], w, g, tm, dn_alpha)
        x = _moe_layer(h2, idx, wts, x1, mod[l], w, g, tm, dn_alpha)

    return x[:g.tp].reshape(bp, sp, d), x[g.tp:].reshape(bs, ss, d)
```

```python
import functools
import math

import jax
import jax.numpy as jnp
from jax import lax
from jax.experimental import pallas as pl
from jax.experimental.pallas import tpu as pltpu
from jax.experimental.pallas import tpu_sc as plsc

F32 = jnp.float32
BF16 = jnp.bfloat16
I32 = jnp.int32

N_HEADS = 8
QK_NOPE = 64
QK_ROPE = 32
V_DIM = 64
Q_RANK = 256
KV_RANK = 128
ROPE_BASE = 10000.0
CHUNK = 128
SGU_WIDTH = 512
SGU_GROUPS = 8
N_EXPERTS = 32
TOP_K = 4
SWIGLU_LIMIT = 7.0
SWIGLU_ALPHA = 1.702
LN_EPS = 1e-5
RMS_EPS = 1e-6
ATTN_SCALE = 1.0 / math.sqrt(QK_NOPE + QK_ROPE)

LANES = 128
LAT = KV_RANK
QK_PAD = 2 * LANES
MOE_ROWS = 512
GU_BLOCK = 2 * LANES
VMEM_LIMIT = 56 * 1024 * 1024
SC_CORES = 2
SC_SUBCORES = 16
SC_WORKERS = SC_CORES * SC_SUBCORES
SC_CHUNK = 64


def _cparams(sem, vmem=VMEM_LIMIT):
    return pltpu.CompilerParams(dimension_semantics=sem, vmem_limit_bytes=vmem)


def _pick(n, prefs):
    for p in prefs:
        if n % p == 0:
            return p
    raise ValueError(f"no tile in {prefs} divides {n}")


def _const_spec(shape):
    nd = len(shape)
    return pl.BlockSpec(shape, lambda *_: (0,) * nd)


def _dot(a, b):
    return jnp.dot(a, b, preferred_element_type=F32)


def _layer_norm(y, g, b):
    mu = jnp.mean(y, axis=-1, keepdims=True)
    d = y - mu
    var = jnp.mean(d * d, axis=-1, keepdims=True)
    return d * lax.rsqrt(var + LN_EPS) * g + b


def _rms_norm(y, g):
    return y * lax.rsqrt(jnp.mean(y * y, axis=-1, keepdims=True) + RMS_EPS) * g


HIGH_HALF = -65536


def _pack_bf16_pairs(y):
    w = y.shape[1] // 2
    bits = lax.bitcast_convert_type(y.astype(BF16).astype(F32), I32)
    return (bits[:, w:] & HIGH_HALF) | lax.shift_right_logical(bits[:, :w], 16)


def _unpack_bf16_pairs(p):
    lo = lax.bitcast_convert_type(lax.shift_left(p, 16), F32)
    hi = lax.bitcast_convert_type(p & HIGH_HALF, F32)
    return lo, hi


class _Groups:
    def __init__(self, bp, sp, bs, ss):
        self.bp, self.sp, self.bs, self.ss = bp, sp, bs, ss
        self.tp, self.ts = bp * sp, bs * ss
        self.t = self.tp + self.ts

    def seq_of_tile(self, i, tm):
        npt = self.tp // tm
        return jnp.where(i < npt, (i * tm) // self.sp, self.bp + ((i - npt) * tm) // self.ss)

    def pos_tile(self, i, tm):
        npt = self.tp // tm
        return jnp.where(i < npt, i % (self.sp // tm), (i - npt) % (self.ss // tm))


def _row_part_specs(parts, tm, g):
    width = parts[0].shape[1]
    if len(parts) == 1:
        return [pl.BlockSpec((tm, width), lambda i: (i, 0))]
    npt, nst = g.tp // tm, g.ts // tm
    return [pl.BlockSpec((tm, width), lambda i: (jnp.minimum(i, npt - 1), 0)),
            pl.BlockSpec((tm, width), lambda i: (jnp.clip(i - npt, 0, nst - 1), 0))]


def _read_rows(refs, npt):
    if len(refs) == 1:
        return refs[0][...]
    return jnp.where(pl.program_id(0) < npt, refs[0][...], refs[1][...])


def _write_rows(refs, npt, val):
    if len(refs) == 1:
        refs[0][...] = val
        return
    i = pl.program_id(0)

    @pl.when(i < npt)
    def _():
        refs[0][...] = val

    @pl.when(i >= npt)
    def _():
        refs[1][...] = val


def _mod_kernel(c_ref, w_ref, b_ref, o_ref):
    c = c_ref[...]
    a = c * jax.nn.sigmoid(c)
    o_ref[0] = jnp.dot(a, w_ref[0], preferred_element_type=F32,
                       precision=lax.Precision.HIGHEST) + b_ref[0]


def _modulation(c_all, w_mod, b_mod):
    depth, d, six_d = w_mod.shape
    ns = c_all.shape[0]
    n_col = six_d // d
    out = pl.pallas_call(
        _mod_kernel,
        grid=(depth, n_col),
        in_specs=[
            pl.BlockSpec((ns, d), lambda l, j: (0, 0)),
            pl.BlockSpec((1, d, d), lambda l, j: (l, 0, j)),
            pl.BlockSpec((1, 1, d), lambda l, j: (l, 0, j)),
        ],
        out_specs=pl.BlockSpec((1, ns, d), lambda l, j: (l, 0, j)),
        out_shape=jax.ShapeDtypeStruct((depth, ns, six_d), F32),
        compiler_params=_cparams(("parallel", "parallel")),
        name="modulation",
    )(c_all, w_mod, b_mod.reshape(depth, 1, six_d))
    return out.reshape(depth, ns, n_col, d)


def _bmm_kernel(a_ref, b_ref, o_ref):
    o_ref[0] = jnp.dot(a_ref[0], b_ref[0], preferred_element_type=F32,
                       precision=lax.Precision.HIGHEST)


def _bmm(a, b):
    h, m, k = a.shape
    n = b.shape[2]
    return pl.pallas_call(
        _bmm_kernel,
        grid=(h,),
        in_specs=[pl.BlockSpec((1, m, k), lambda i: (i, 0, 0)),
                  pl.BlockSpec((1, k, n), lambda i: (i, 0, 0))],
        out_specs=pl.BlockSpec((1, m, n), lambda i: (i, 0, 0)),
        out_shape=jax.ShapeDtypeStruct((h, m, n), F32),
        compiler_params=_cparams(("parallel",)),
        name="weight_fold",
    )(a, b)


def _gelu_tanh(x):
    c = math.sqrt(2.0 / math.pi)
    return 0.5 * x * (1.0 + jnp.tanh(c * (x + 0.044715 * (x * x * x))))


def _inproj_kernel(n_x, npt, *refs):
    x_refs, refs = refs[:n_x], refs[n_x:]
    (mod_ref, cos_ref, sin_ref, wa_ref, wuv_ref, wg_ref, gq_ref, gkv_ref,
     wql_ref, wqr_ref, wqrs_ref, lng_ref, lnb_ref, wsp_ref, bs_ref,
     q_ref, klat_ref, sgu_ref, sg_ref) = refs
    tm = x_refs[0].shape[0]
    x = _read_rows(x_refs, npt)
    sh1 = mod_ref[0:1, :]
    sc1 = mod_ref[1:2, :]
    h = (x * (1.0 + sc1) + sh1).astype(BF16)

    za = _dot(h, wa_ref[...])
    cq = za[:, 0:Q_RANK]
    ckv = za[:, Q_RANK:Q_RANK + KV_RANK]
    kr = za[:, Q_RANK + KV_RANK:Q_RANK + KV_RANK + LANES]
    krs = za[:, Q_RANK + KV_RANK + LANES:Q_RANK + KV_RANK + 2 * LANES]
    cos = cos_ref[...]
    sin = sin_ref[...]
    cqn = _rms_norm(cq, gq_ref[...]).astype(BF16)
    klat_ref[:, 0:LAT] = _rms_norm(ckv, gkv_ref[...]).astype(BF16)
    klat_ref[:, LAT:QK_PAD] = (kr * cos + krs * sin).astype(BF16)

    ql = _dot(cqn, wql_ref[...])
    qr = _dot(cqn, wqr_ref[...])
    qrs = _dot(cqn, wqrs_ref[...])
    cos_q = cos * ATTN_SCALE
    sin_q = sin * ATTN_SCALE
    for hd in range(N_HEADS):
        sl = slice(hd * LANES, (hd + 1) * LANES)
        q_ref[hd, :, 0:LAT] = (ql[:, sl] * ATTN_SCALE).astype(BF16)
        q_ref[hd, :, LAT:QK_PAD] = (qr[:, sl] * cos_q + qrs[:, sl] * sin_q).astype(BF16)

    sg_ref[...] = jax.nn.sigmoid(_dot(h, wg_ref[...])).astype(BF16)

    guv = _gelu_tanh(_dot(h, wuv_ref[...]))
    u = guv[:, :SGU_WIDTH]
    vn = _layer_norm(guv[:, SGU_WIDTH:], lng_ref[...], lnb_ref[...]).astype(BF16)
    lane = lax.broadcasted_iota(I32, (CHUNK, LANES), 1)
    low_half = lane < (LANES // 2)
    zero = jnp.zeros((CHUNK, LANES), BF16)
    for c in range(tm // CHUNK):
        rows = slice(c * CHUNK, (c + 1) * CHUNK)
        vc = vn[rows, :]
        mixed = []
        for j in range(SGU_WIDTH // LANES):
            blk = vc[:, j * LANES:(j + 1) * LANES]
            rhs = jnp.concatenate([jnp.where(low_half, blk, zero), jnp.where(low_half, zero, blk)], axis=0)
            mixed.append(_dot(wsp_ref[j], rhs))
        mixed = jnp.concatenate(mixed, axis=1) + bs_ref[...]
        sgu_ref[rows, :] = (u[rows, :] * mixed).astype(BF16)


def _inproj(x_parts, mod_l, cos_t, sin_t, w, g, tm):
    t, d = g.t, x_parts[0].shape[1]
    n = t // tm
    seq = lambda i: (g.seq_of_tile(i, tm), 0, 0)
    pos = lambda i: (g.pos_tile(i, tm), 0)
    row = lambda i: (i, 0)
    consts = [w["wa"], w["wuv"], w["wg"], w["gq"], w["gkv"], w["wql"], w["wqr"], w["wqrs"],
              w["lng"], w["lnb"], w["wsp"], w["bs"]]
    return pl.pallas_call(
        functools.partial(_inproj_kernel, len(x_parts), g.tp // tm),
        grid=(n,),
        in_specs=_row_part_specs(x_parts, tm, g) + [
                  pl.BlockSpec((None, mod_l.shape[1], d), seq),
                  pl.BlockSpec((tm, LANES), pos),
                  pl.BlockSpec((tm, LANES), pos)] + [_const_spec(c.shape) for c in consts],
        out_specs=[pl.BlockSpec((N_HEADS, tm, QK_PAD), lambda i: (0, i, 0)),
                   pl.BlockSpec((tm, QK_PAD), row),
                   pl.BlockSpec((tm, SGU_WIDTH), row),
                   pl.BlockSpec((tm, 2 * d), row)],
        out_shape=[jax.ShapeDtypeStruct((N_HEADS, t, QK_PAD), BF16),
                   jax.ShapeDtypeStruct((t, QK_PAD), BF16),
                   jax.ShapeDtypeStruct((t, SGU_WIDTH), BF16),
                   jax.ShapeDtypeStruct((t, 2 * d), BF16)],
        compiler_params=_cparams(("parallel",)),
        name="token_mixer_in",
    )(*x_parts, mod_l, cos_t, sin_t, *consts)


def _attn_kernel(q_ref, k_ref, o_ref, m_ref, l_ref, acc_ref):
    ki = pl.program_id(2)
    nh, tq, _ = q_ref.shape

    @pl.when(ki == 0)
    def _():
        m_ref[...] = jnp.full(m_ref.shape, -jnp.inf, F32)
        l_ref[...] = jnp.zeros(l_ref.shape, F32)
        acc_ref[...] = jnp.zeros(acc_ref.shape, F32)

    q = q_ref[...].reshape(nh * tq, QK_PAD)
    k = k_ref[...]
    s = lax.dot_general(q, k, (((1,), (1,)), ((), ())), preferred_element_type=F32)
    m_prev = m_ref[...]
    m_new = jnp.maximum(m_prev, jnp.max(s, axis=-1, keepdims=True))
    alpha = jnp.exp(m_prev - m_new)
    p = jnp.exp(s - m_new)
    l_ref[...] = alpha * l_ref[...] + jnp.sum(p, axis=-1, keepdims=True)
    acc_ref[...] = alpha * acc_ref[...] + _dot(p.astype(BF16), k[:, 0:LAT])
    m_ref[...] = m_new

    @pl.when(ki == pl.num_programs(2) - 1)
    def _():
        o = acc_ref[...] / l_ref[...]
        for hd in range(nh):
            o_ref[:, hd * LAT:(hd + 1) * LAT] = o[hd * tq:(hd + 1) * tq, :].astype(BF16)


def _attention(q, klat, row0, nb, s, tq, tk):
    nq, nk = s // tq, s // tk
    q0, k0 = row0 // tq, row0 // tk
    m_rows = N_HEADS * tq
    return pl.pallas_call(
        _attn_kernel,
        grid=(nb, nq, nk),
        in_specs=[pl.BlockSpec((N_HEADS, tq, QK_PAD), lambda b, i, j: (0, q0 + b * nq + i, 0)),
                  pl.BlockSpec((tk, QK_PAD), lambda b, i, j: (k0 + b * nk + j, 0))],
        out_specs=pl.BlockSpec((tq, N_HEADS * LAT), lambda b, i, j: (b * nq + i, 0)),
        out_shape=jax.ShapeDtypeStruct((nb * s, N_HEADS * LAT), BF16),
        scratch_shapes=[pltpu.VMEM((m_rows, 1), F32), pltpu.VMEM((m_rows, 1), F32),
                        pltpu.VMEM((m_rows, LAT), F32)],
        compiler_params=_cparams(("parallel", "parallel", "arbitrary")),
        name="latent_attention",
    )(q, klat)


def _outproj_kernel(n_x, npt, dn_alpha, op_ref, os_ref, sgu_ref, sg_ref, *refs):
    x_refs, refs = refs[:n_x], refs[n_x:]
    (mod_ref, wfold_ref, wosgu_ref, wout_ref, lng_ref, lnb_ref, rwt_ref, rb_ref,
     x1_ref, h2_ref, idx_ref, wts_ref) = refs
    tm, d = x1_ref.shape
    o = _read_rows((op_ref, os_ref), npt)
    o_attn = _dot(o, wfold_ref[...])
    o_sgu = _dot(sgu_ref[...], wosgu_ref[...])
    sg = sg_ref[...].astype(F32)
    merged = (sg[:, :d] * o_attn + sg[:, d:] * o_sgu).astype(BF16)
    mix = _dot(merged, wout_ref[...])
    g1 = mod_ref[2:3, :]
    sh2 = mod_ref[3:4, :]
    sc2 = mod_ref[4:5, :]
    x1 = _layer_norm(dn_alpha * _read_rows(x_refs, npt) + (1.0 + g1) * mix, lng_ref[...], lnb_ref[...])
    x1_ref[...] = x1
    h2 = x1 * (1.0 + sc2) + sh2
    h2_ref[...] = _pack_bf16_pairs(h2)

    logits = lax.dot_general(rwt_ref[...], h2, (((1,), (1,)), ((), ())), preferred_element_type=F32,
                             precision=lax.Precision.HIGHEST) + rb_ref[...]
    e_iota = lax.broadcasted_iota(I32, (N_EXPERTS, tm), 0)
    vals, idxs = [], []
    for _ in range(TOP_K):
        mx = jnp.max(logits, axis=0, keepdims=True)
        ix = jnp.min(jnp.where(logits == mx, e_iota, N_EXPERTS), axis=0, keepdims=True)
        vals.append(mx)
        idxs.append(ix)
        logits = jnp.where(e_iota == ix, -jnp.inf, logits)
    v = jnp.concatenate(vals, axis=0)
    w = jnp.exp(v - v[0:1, :])
    wts_ref[...] = w / jnp.sum(w, axis=0, keepdims=True)
    idx_ref[...] = jnp.concatenate(idxs, axis=0)


def _outproj(o_p, o_s, sgu, sg, x_parts, mod_l, w, g, tm, dn_alpha):
    t, d = g.t, x_parts[0].shape[1]
    n = t // tm
    seq = lambda i: (g.seq_of_tile(i, tm), 0, 0)
    row = lambda i: (i, 0)
    col = lambda i: (0, i)
    consts = [w["wfold"], w["wosgu"], w["wout"], w["ln1g"], w["ln1b"], w["rwt"], w["rb"]]
    return pl.pallas_call(
        functools.partial(_outproj_kernel, len(x_parts), g.tp // tm, dn_alpha),
        grid=(n,),
        in_specs=_row_part_specs((o_p, o_s), tm, g) + [
                  pl.BlockSpec((tm, SGU_WIDTH), row),
                  pl.BlockSpec((tm, 2 * d), row)] + _row_part_specs(x_parts, tm, g) + [
                  pl.BlockSpec((None, mod_l.shape[1], d), seq)] + [_const_spec(c.shape) for c in consts],
        out_specs=[pl.BlockSpec((tm, d), row),
                   pl.BlockSpec((tm, d // 2), row),
                   pl.BlockSpec((TOP_K, tm), col),
                   pl.BlockSpec((TOP_K, tm), col)],
        out_shape=[jax.ShapeDtypeStruct((t, d), F32),
                   jax.ShapeDtypeStruct((t, d // 2), I32),
                   jax.ShapeDtypeStruct((TOP_K, t), I32),
                   jax.ShapeDtypeStruct((TOP_K, t), F32)],
        compiler_params=_cparams(("parallel",)),
        name="token_mixer_out",
    )(o_p, o_s, sgu, sg, *x_parts, mod_l, *consts)


def _one_hots(idx, tm):
    e_iota = lax.broadcasted_iota(I32, (N_EXPERTS, tm), 0)
    return [(e_iota == idx[k:k + 1, :]).astype(F32) for k in range(TOP_K)]


def _count_kernel(idx_ref, cnt_ref):
    @pl.when(pl.program_id(0) == 0)
    def _():
        cnt_ref[...] = jnp.zeros(cnt_ref.shape, F32)

    oh = sum(_one_hots(idx_ref[...], idx_ref.shape[1]))
    cnt_ref[...] += jnp.sum(oh, axis=1, keepdims=True)


def _dest_kernel(idx_ref, start_ref, dest_ref, carry_ref):
    tm = idx_ref.shape[1]

    @pl.when(pl.program_id(0) == 0)
    def _():
        carry_ref[...] = start_ref[...]

    ohs = _one_hots(idx_ref[...], tm)
    oh = sum(ohs)
    r = lax.broadcasted_iota(I32, (tm, tm), 0)
    c = lax.broadcasted_iota(I32, (tm, tm), 1)
    before = (r < c).astype(BF16)
    rank = _dot(oh.astype(BF16), before) + carry_ref[:, 0:1]
    dest_ref[...] = jnp.concatenate(
        [jnp.sum(o * rank, axis=0, keepdims=True) for o in ohs], axis=0).astype(I32)
    carry_ref[...] += jnp.sum(oh, axis=1, keepdims=True)


def _route(idx, tm):
    k, t = idx.shape
    n = t // tm
    col = lambda i: (0, i)
    cnt = pl.pallas_call(
        _count_kernel,
        grid=(n,),
        in_specs=[pl.BlockSpec((k, tm), col)],
        out_specs=_const_spec((N_EXPERTS, LANES)),
        out_shape=jax.ShapeDtypeStruct((N_EXPERTS, LANES), F32),
        compiler_params=_cparams(("arbitrary",)),
        name="expert_counts",
    )(idx)
    counts = cnt[:, 0].astype(I32)
    padded = (counts + MOE_ROWS - 1) // MOE_ROWS * MOE_ROWS
    pad_end = jnp.cumsum(padded)
    pad_start = pad_end - padded
    start = jnp.broadcast_to(pad_start.astype(F32)[:, None], (N_EXPERTS, LANES))
    dest = pl.pallas_call(
        _dest_kernel,
        grid=(n,),
        in_specs=[pl.BlockSpec((k, tm), col), _const_spec((N_EXPERTS, LANES))],
        out_specs=pl.BlockSpec((k, tm), col),
        out_shape=jax.ShapeDtypeStruct((k, t), I32),
        scratch_shapes=[pltpu.VMEM((N_EXPERTS, LANES), F32)],
        compiler_params=_cparams(("arbitrary",)),
        name="expert_dest_rows",
    )(idx, start)
    return dest, counts, padded, pad_start, pad_end


def _count_le(sorted_vals, v):
    return jnp.sum((sorted_vals[None, :] <= v[:, None]).astype(I32), axis=1)


def _padding_rows(counts, padded, pad_start, pad_end, n_pad):
    per = padded - counts
    cum = jnp.cumsum(per)
    j = jnp.arange(n_pad, dtype=I32)
    e = _count_le(cum, j)
    ec = jnp.minimum(e, N_EXPERTS - 1)
    inside = pad_start[ec] + counts[ec] + (j - (cum[ec] - per[ec]))
    tail = pad_end[-1] + (j - cum[-1])
    return jnp.where(e < N_EXPERTS, inside, tail)


def _sc_mesh():
    return plsc.VectorSubcoreMesh(core_axis_name="c", subcore_axis_name="s")


def _sc_scatter_rows(src, idx_main, idx_pad, n_out):
    nw, cpk, c = idx_main.shape
    width = src.shape[1]
    cpw = src.shape[0] // (nw * c)
    copies = cpk // cpw
    ppw = idx_pad.shape[1]

    @functools.partial(
        pl.kernel, mesh=_sc_mesh(),
        out_type=jax.ShapeDtypeStruct((n_out, width), src.dtype),
        scratch_types=[pltpu.VMEM((cpk, c), I32), pltpu.VMEM((ppw, c), I32),
                       pltpu.VMEM((c, width), src.dtype), pltpu.SemaphoreType.DMA],
        name="dispatch_rows")
    def run(src_hbm, idx_hbm, pad_hbm, out_hbm, idx_v, pad_v, rows_v, sem):
        wid = lax.axis_index("s") * SC_CORES + lax.axis_index("c")
        pltpu.sync_copy(idx_hbm.at[wid], idx_v)
        pltpu.sync_copy(pad_hbm.at[wid], pad_v)

        @pl.loop(0, cpw)
        def _(j):
            pltpu.sync_copy(src_hbm.at[pl.ds((wid * cpw + j) * c, c)], rows_v)
            cps = [pltpu.async_copy(rows_v, out_hbm.at[idx_v.at[j * copies + kk]], sem)
                   for kk in range(copies)]
            for cp in cps:
                cp.wait()

        n_src_chunks = src.shape[0] // c

        @pl.loop(0, ppw)
        def _(j):
            pltpu.sync_copy(src_hbm.at[pl.ds(((wid * ppw + j) % n_src_chunks) * c, c)], rows_v)
            pltpu.async_copy(rows_v, out_hbm.at[pad_v.at[j]], sem).wait()

    return run(src, idx_main, idx_pad)


def _sc_gather_rows(src, idx):
    nw, cpw, c = idx.shape
    width = src.shape[1]

    @functools.partial(
        pl.kernel, mesh=_sc_mesh(),
        out_type=jax.ShapeDtypeStruct((nw * cpw * c, width), src.dtype),
        scratch_types=[pltpu.VMEM((cpw, c), I32), pltpu.VMEM((c, width), src.dtype),
                       pltpu.SemaphoreType.DMA],
        name="collect_rows")
    def run(src_hbm, idx_hbm, out_hbm, idx_v, rows_v, sem):
        wid = lax.axis_index("s") * SC_CORES + lax.axis_index("c")
        pltpu.sync_copy(idx_hbm.at[wid], idx_v)

        @pl.loop(0, cpw)
        def _(j):
            pltpu.async_copy(src_hbm.at[idx_v.at[j]], rows_v, sem).wait()
            pltpu.sync_copy(rows_v, out_hbm.at[pl.ds((wid * cpw + j) * c, c)])

    return run(src, idx)


def _dispatch(h2, dest, pad_rows, n_rows):
    k, t = dest.shape
    cpw = t // (SC_WORKERS * SC_CHUNK)
    idx_main = dest.reshape(k, SC_WORKERS, cpw, SC_CHUNK).transpose(1, 2, 0, 3)
    idx_main = idx_main.reshape(SC_WORKERS, cpw * k, SC_CHUNK)
    idx_pad = pad_rows.reshape(SC_WORKERS, -1, SC_CHUNK)
    return _sc_scatter_rows(h2, idx_main, idx_pad, n_rows)


def _collect(y, dest):
    k, t = dest.shape
    idx = dest.reshape(SC_WORKERS, (k * t) // (SC_WORKERS * SC_CHUNK), SC_CHUNK)
    return _sc_gather_rows(y, idx).reshape(k, t, y.shape[1])


def _moe_kernel(be_ref, x_ref, wgu_ref, bgu_ref, wd_ref, bd_ref, y_ref):
    lo, hi = _unpack_bf16_pairs(x_ref[...])
    x = jnp.concatenate([lo, hi], axis=1).astype(BF16)
    gu = _dot(x, wgu_ref[...]) + bgu_ref[...]
    acts = []
    for b in range(gu.shape[1] // GU_BLOCK):
        glu = jnp.minimum(gu[:, b * GU_BLOCK:b * GU_BLOCK + LANES], SWIGLU_LIMIT)
        lin = jnp.clip(gu[:, b * GU_BLOCK + LANES:(b + 1) * GU_BLOCK], -SWIGLU_LIMIT, SWIGLU_LIMIT)
        acts.append(glu * jax.nn.sigmoid(SWIGLU_ALPHA * glu) * (lin + 1.0))
    act = jnp.concatenate(acts, axis=1).astype(BF16)
    y_ref[...] = _pack_bf16_pairs(_dot(act, wd_ref[...]) + bd_ref[...])


def _moe_ffn(xs, block_e, w, l):
    n_rows, half = xs.shape
    d = 2 * half
    f2 = w["wgu"].shape[3]
    n_blocks = n_rows // MOE_ROWS
    row = lambda i, be: (i, 0)
    exp = lambda i, be: (l, be[i], 0, 0)
    return pl.pallas_call(
        _moe_kernel,
        grid_spec=pltpu.PrefetchScalarGridSpec(
            num_scalar_prefetch=1,
            grid=(n_blocks,),
            in_specs=[pl.BlockSpec((MOE_ROWS, half), row),
                      pl.BlockSpec((None, None, d, f2), exp), pl.BlockSpec((None, None, 1, f2), exp),
                      pl.BlockSpec((None, None, f2 // 2, d), exp), pl.BlockSpec((None, None, 1, d), exp)],
            out_specs=pl.BlockSpec((MOE_ROWS, half), row)),
        out_shape=jax.ShapeDtypeStruct((n_rows, half), I32),
        compiler_params=_cparams(("arbitrary",)),
        name="expert_ffn",
    )(block_e, xs, w["wgu"], w["bgu"], w["wdown"], w["bdown"])


def _combine_kernel(n_out, npt, dn_alpha, yg_ref, wts_ref, x1_ref, mod_ref, lng_ref, lnb_ref, *x2_refs):
    wts = wts_ref[...]
    lo = hi = None
    for k in range(TOP_K):
        l_k, h_k = _unpack_bf16_pairs(yg_ref[k])
        w_k = wts[:, k:k + 1]
        lo = l_k * w_k if lo is None else lo + l_k * w_k
        hi = h_k * w_k if hi is None else hi + h_k * w_k
    ff = jnp.concatenate([lo, hi], axis=1)
    g2 = mod_ref[5:6, :]
    x2 = _layer_norm(dn_alpha * x1_ref[...] + (1.0 + g2) * ff, lng_ref[...], lnb_ref[...])
    _write_rows(x2_refs[:n_out], npt, x2)


def _combine(yg, wts_t, x1, mod_l, ln_g, ln_b, g, tm, dn_alpha, split_out):
    t, d = x1.shape
    row = lambda i: (i, 0)
    seq = lambda i: (g.seq_of_tile(i, tm), 0, 0)
    if split_out:
        outs = [jax.ShapeDtypeStruct((g.tp, d), F32), jax.ShapeDtypeStruct((g.ts, d), F32)]
    else:
        outs = [jax.ShapeDtypeStruct((t, d), F32)]
    return pl.pallas_call(
        functools.partial(_combine_kernel, len(outs), g.tp // tm, dn_alpha),
        grid=(t // tm,),
        in_specs=[pl.BlockSpec((TOP_K, tm, d // 2), lambda i: (0, i, 0)),
                  pl.BlockSpec((tm, TOP_K), row),
                  pl.BlockSpec((tm, d), row),
                  pl.BlockSpec((None, mod_l.shape[1], d), seq),
                  _const_spec(ln_g.shape), _const_spec(ln_b.shape)],
        out_specs=_row_part_specs(outs, tm, g),
        out_shape=outs,
        compiler_params=_cparams(("arbitrary",)),
        name="moe_combine",
    )(yg, wts_t, x1, mod_l, ln_g, ln_b)


def _deinterleave_kernel(w_ref, perm_ref, o_ref):
    w = w_ref[...].astype(BF16)
    for b in range(w.shape[1] // GU_BLOCK):
        cols = slice(b * GU_BLOCK, (b + 1) * GU_BLOCK)
        o_ref[:, cols] = _dot(w[:, cols], perm_ref[...]).astype(BF16)


def _deinterleave_gate_up(w_gate_up):
    depth, ne, d, f2 = w_gate_up.shape
    tn = _pick(f2, (1024, 512, GU_BLOCK))
    j = jnp.arange(GU_BLOCK)
    target = (j % 2) * LANES + j // 2
    perm = (target[:, None] == jnp.arange(GU_BLOCK)[None, :]).astype(BF16)
    out = pl.pallas_call(
        _deinterleave_kernel,
        grid=(depth * ne, f2 // tn),
        in_specs=[pl.BlockSpec((None, d, tn), lambda e, j: (e, 0, j)), _const_spec((GU_BLOCK, GU_BLOCK))],
        out_specs=pl.BlockSpec((None, d, tn), lambda e, j: (e, 0, j)),
        out_shape=jax.ShapeDtypeStruct((depth * ne, d, f2), BF16),
        compiler_params=_cparams(("parallel", "parallel")),
        name="gate_up_columns",
    )(w_gate_up.reshape(depth * ne, d, f2), perm)
    return out.reshape(depth, ne, d, f2)


def _deinterleave_bias(b_gate_up):
    depth, ne, f2 = b_gate_up.shape
    b = b_gate_up.reshape(depth, ne, f2 // GU_BLOCK, LANES, 2).transpose(0, 1, 2, 4, 3)
    return b.reshape(depth, ne, 1, f2)


def _rope_tables(seq_len):
    inv = 1.0 / (ROPE_BASE ** (jnp.arange(0, QK_ROPE, 2, dtype=F32) / QK_ROPE))
    ang = jnp.arange(seq_len, dtype=F32)[:, None] * inv[None, :]
    pad = jnp.zeros((seq_len, LANES - QK_ROPE), F32)
    cos = jnp.concatenate([jnp.cos(ang), jnp.cos(ang), pad], axis=1)
    sin = jnp.concatenate([jnp.sin(ang), jnp.sin(ang), pad], axis=1)
    return cos, sin


def _rot_pair(w):
    half = w.shape[-1] // 2
    return jnp.concatenate([-w[..., half:], w[..., :half]], axis=-1)


def _pad_lanes(w):
    return jnp.pad(w, [(0, 0)] * (w.ndim - 1) + [(0, LANES - w.shape[-1])])


def _prep_layer(l, p):
    d = p["w_in"].shape[1]
    w_in = p["w_in"][l]
    i1 = Q_RANK
    i2 = i1 + KV_RANK
    i3 = i2 + QK_ROPE
    i4 = i3 + 2 * SGU_WIDTH
    w_kr = w_in[:, i2:i3]
    w = {}
    w["wa"] = jnp.concatenate([w_in[:, :i2], _pad_lanes(w_kr), _pad_lanes(_rot_pair(w_kr))], axis=1).astype(BF16)
    w["wuv"] = w_in[:, i3:i4].astype(BF16)
    w["wg"] = w_in[:, i4:].astype(BF16)
    w["gq"] = p["q_norm_g"][l][None, :]
    w["gkv"] = p["kv_norm_g"][l][None, :]

    w_uq = p["w_uq"][l].reshape(Q_RANK, N_HEADS, QK_NOPE + QK_ROPE)
    w_ukv = p["w_ukv"][l].reshape(KV_RANK, N_HEADS, QK_NOPE + V_DIM)
    uq_nope = w_uq[:, :, :QK_NOPE].transpose(1, 0, 2)
    uk_t = w_ukv[:, :, :QK_NOPE].transpose(1, 2, 0)
    w["wql"] = _bmm(uq_nope, uk_t).transpose(1, 0, 2).reshape(Q_RANK, N_HEADS * LAT).astype(BF16)
    uq_rope = w_uq[:, :, QK_NOPE:]
    w["wqr"] = _pad_lanes(uq_rope).reshape(Q_RANK, N_HEADS * LANES).astype(BF16)
    w["wqrs"] = _pad_lanes(_rot_pair(uq_rope)).reshape(Q_RANK, N_HEADS * LANES).astype(BF16)
    uv = w_ukv[:, :, QK_NOPE:].transpose(1, 0, 2)
    wo = p["w_o_attn"][l].reshape(N_HEADS, V_DIM, d)
    w["wfold"] = _bmm(uv, wo).reshape(N_HEADS * LAT, d).astype(BF16)

    w["lng"] = p["sgu_ln_g"][l][None, :]
    w["lnb"] = p["sgu_ln_b"][l][None, :]
    gpl = LANES // (SGU_WIDTH // SGU_GROUPS)
    ws = p["w_s"][l].reshape(SGU_GROUPS // gpl, gpl, CHUNK, CHUNK)
    w["wsp"] = ws.transpose(0, 2, 1, 3).reshape(SGU_GROUPS // gpl, CHUNK, gpl * CHUNK).astype(BF16)
    w["bs"] = jnp.repeat(p["b_s"][l].T, SGU_WIDTH // SGU_GROUPS, axis=1)
    w["wosgu"] = p["w_o_sgu"][l].astype(BF16)
    w["wout"] = p["w_out"][l].astype(BF16)
    w["ln1g"] = p["ln1_g"][l][None, :]
    w["ln1b"] = p["ln1_b"][l][None, :]
    w["rwt"] = p["router_w"][l].T
    w["rb"] = p["router_b"][l][:, None]
    w["ln2g"] = p["ln2_g"][l][None, :]
    w["ln2b"] = p["ln2_b"][l][None, :]
    return w


def _moe_layer(l, h2, idx, wts, x1, mod_l, w, g, tm, dn_alpha, split_out):
    t = x1.shape[0]
    n_pad = N_EXPERTS * MOE_ROWS
    n_rows = TOP_K * t + n_pad
    dest, counts, padded, pad_start, pad_end = _route(idx, _pick(t, (512, 256, 128)))
    pad_rows = _padding_rows(counts, padded, pad_start, pad_end, n_pad)
    block_start = jnp.arange(n_rows // MOE_ROWS, dtype=I32) * MOE_ROWS
    block_e = jnp.minimum(_count_le(pad_end, block_start), N_EXPERTS - 1)
    xs = _dispatch(h2, dest, pad_rows, n_rows)
    y = _moe_ffn(xs, block_e, w, l)
    yg = _collect(y, dest)
    return _combine(yg, wts.T, x1, mod_l, w["ln2g"], w["ln2b"], g, tm, dn_alpha, split_out)


def kernel(x_prompt, x_sample, c_prompt, c_sample, w_mod, b_mod, w_in, q_norm_g, kv_norm_g, w_uq, w_ukv, w_o_attn, sgu_ln_g, sgu_ln_b, w_s, b_s, w_o_sgu, w_out, ln1_g, ln1_b, router_w, router_b, w_gate_up, b_gate_up, w_down, b_down, ln2_g, ln2_b):
    p = dict(w_in=w_in, q_norm_g=q_norm_g, kv_norm_g=kv_norm_g, w_uq=w_uq, w_ukv=w_ukv, w_o_attn=w_o_attn,
             sgu_ln_g=sgu_ln_g, sgu_ln_b=sgu_ln_b, w_s=w_s, b_s=b_s, w_o_sgu=w_o_sgu, w_out=w_out,
             ln1_g=ln1_g, ln1_b=ln1_b, router_w=router_w, router_b=router_b, ln2_g=ln2_g, ln2_b=ln2_b)
    depth = w_mod.shape[0]
    bp, sp, d = x_prompt.shape
    bs, ss, _ = x_sample.shape
    g = _Groups(bp, sp, bs, ss)
    dn_alpha = (2 * depth) ** 0.25

    tm = _pick(math.gcd(sp, ss), (512, 256, 128))
    tq = _pick(math.gcd(sp, ss), (256, 128))
    tk_p = _pick(sp, (1024, 512, 256, 128))
    tk_s = _pick(ss, (1024, 512, 256, 128))
    assert g.t % (SC_WORKERS * SC_CHUNK) == 0 and (N_EXPERTS * MOE_ROWS) % (SC_WORKERS * SC_CHUNK) == 0

    n_seq = bp + bs
    ns_pad = -(-n_seq // 8) * 8
    c_all = jnp.pad(jnp.concatenate([c_prompt, c_sample], axis=0), ((0, ns_pad - n_seq), (0, 0)))
    mod = _modulation(c_all, w_mod, b_mod)
    cos_t, sin_t = _rope_tables(max(sp, ss))
    moe_w = dict(wgu=_deinterleave_gate_up(w_gate_up), bgu=_deinterleave_bias(b_gate_up),
                 wdown=w_down.astype(BF16), bdown=b_down[:, :, None, :])
    x_parts = [x_prompt.reshape(g.tp, d), x_sample.reshape(g.ts, d)]

    for l in range(depth):
        w = dict(_prep_layer(l, p), **moe_w)
        q, klat, sgu, sg = _inproj(x_parts, mod[l], cos_t, sin_t, w, g, tm)
        o_p = _attention(q, klat, 0, bp, sp, tq, tk_p)
        o_s = _attention(q, klat, g.tp, bs, ss, tq, tk_s)
        x1, h2, idx, wts = _outproj(o_p, o_s, sgu, sg, x_parts, mod[l], w, g, tm, dn_alpha)
        x_parts = _moe_layer(l, h2, idx, wts, x1, mod[l], w, g, tm, dn_alpha, split_out=(l == depth - 1))

    return x_parts[0].reshape(bp, sp, d), x_parts[1].reshape(bs, ss, d)
```

```python
import functools
import math

import jax
import jax.numpy as jnp
from jax import lax
from jax.experimental import pallas as pl
from jax.experimental.pallas import tpu as pltpu
from jax.experimental.pallas import tpu_sc as plsc

F32 = jnp.float32
BF16 = jnp.bfloat16
I32 = jnp.int32

N_HEADS = 8
QK_NOPE = 64
QK_ROPE = 32
V_DIM = 64
Q_RANK = 256
KV_RANK = 128
ROPE_BASE = 10000.0
CHUNK = 128
SGU_WIDTH = 512
SGU_GROUPS = 8
N_EXPERTS = 32
TOP_K = 4
SWIGLU_LIMIT = 7.0
SWIGLU_ALPHA = 1.702
LN_EPS = 1e-5
RMS_EPS = 1e-6
ATTN_SCALE = 1.0 / math.sqrt(QK_NOPE + QK_ROPE)
Q_SCALE = ATTN_SCALE * math.log2(math.e)

LANES = 128
LAT = KV_RANK
QK_PAD = 2 * LANES
V_ROWS = LAT + 16
KEY_CHUNK = 256
MOE_ROWS = 512
GU_BLOCK = 2 * LANES
VMEM_LIMIT = 56 * 1024 * 1024
SC_CORES = 2
SC_SUBCORES = 16
SC_WORKERS = SC_CORES * SC_SUBCORES
SC_CHUNK = 64


def _cparams(sem, vmem=VMEM_LIMIT):
    return pltpu.CompilerParams(dimension_semantics=sem, vmem_limit_bytes=vmem)


def _pick(n, prefs):
    for p in prefs:
        if n % p == 0:
            return p
    raise ValueError(f"no tile in {prefs} divides {n}")


def _const_spec(shape):
    nd = len(shape)
    return pl.BlockSpec(shape, lambda *_: (0,) * nd)


def _dot(a, b):
    return jnp.dot(a, b, preferred_element_type=F32)


def _layer_norm(y, g, b):
    mu = jnp.mean(y, axis=-1, keepdims=True)
    d = y - mu
    var = jnp.mean(d * d, axis=-1, keepdims=True)
    return d * lax.rsqrt(var + LN_EPS) * g + b


def _rms_norm(y, g):
    return y * lax.rsqrt(jnp.mean(y * y, axis=-1, keepdims=True) + RMS_EPS) * g


HIGH_HALF = -65536


def _pack_bf16_pairs(y):
    w = y.shape[1] // 2
    bits = lax.bitcast_convert_type(y.astype(BF16).astype(F32), I32)
    return (bits[:, w:] & HIGH_HALF) | lax.shift_right_logical(bits[:, :w], 16)


def _unpack_bf16_pairs(p):
    lo = lax.bitcast_convert_type(lax.shift_left(p, 16), F32)
    hi = lax.bitcast_convert_type(p & HIGH_HALF, F32)
    return lo, hi


class _Groups:
    def __init__(self, bp, sp, bs, ss):
        self.bp, self.sp, self.bs, self.ss = bp, sp, bs, ss
        self.tp, self.ts = bp * sp, bs * ss
        self.t = self.tp + self.ts

    def seq_of_tile(self, i, tm):
        npt = self.tp // tm
        return jnp.where(i < npt, (i * tm) // self.sp, self.bp + ((i - npt) * tm) // self.ss)

    def pos_tile(self, i, tm):
        npt = self.tp // tm
        return jnp.where(i < npt, i % (self.sp // tm), (i - npt) % (self.ss // tm))


def _row_part_specs(parts, tm, g):
    width = parts[0].shape[1]
    if len(parts) == 1:
        return [pl.BlockSpec((tm, width), lambda i: (i, 0))]
    npt, nst = g.tp // tm, g.ts // tm
    return [pl.BlockSpec((tm, width), lambda i: (jnp.minimum(i, npt - 1), 0)),
            pl.BlockSpec((tm, width), lambda i: (jnp.clip(i - npt, 0, nst - 1), 0))]


def _read_rows(refs, npt):
    if len(refs) == 1:
        return refs[0][...]
    return jnp.where(pl.program_id(0) < npt, refs[0][...], refs[1][...])


def _write_rows(refs, npt, val):
    if len(refs) == 1:
        refs[0][...] = val
        return
    i = pl.program_id(0)

    @pl.when(i < npt)
    def _():
        refs[0][...] = val

    @pl.when(i >= npt)
    def _():
        refs[1][...] = val


def _mod_kernel(c_ref, w_ref, b_ref, o_ref):
    c = c_ref[...]
    a = c * jax.nn.sigmoid(c)
    o_ref[0] = jnp.dot(a, w_ref[0], preferred_element_type=F32,
                       precision=lax.Precision.HIGHEST) + b_ref[0]


def _modulation(c_all, w_mod, b_mod):
    depth, d, six_d = w_mod.shape
    ns = c_all.shape[0]
    n_col = six_d // d
    out = pl.pallas_call(
        _mod_kernel,
        grid=(depth, n_col),
        in_specs=[
            pl.BlockSpec((ns, d), lambda l, j: (0, 0)),
            pl.BlockSpec((1, d, d), lambda l, j: (l, 0, j)),
            pl.BlockSpec((1, 1, d), lambda l, j: (l, 0, j)),
        ],
        out_specs=pl.BlockSpec((1, ns, d), lambda l, j: (l, 0, j)),
        out_shape=jax.ShapeDtypeStruct((depth, ns, six_d), F32),
        compiler_params=_cparams(("parallel", "parallel")),
        name="modulation",
    )(c_all, w_mod, b_mod.reshape(depth, 1, six_d))
    return out.reshape(depth, ns, n_col, d)


def _bmm_kernel(a_ref, b_ref, o_ref):
    o_ref[0] = jnp.dot(a_ref[0], b_ref[0], preferred_element_type=F32,
                       precision=lax.Precision.HIGHEST)


def _bmm(a, b):
    h, m, k = a.shape
    n = b.shape[2]
    return pl.pallas_call(
        _bmm_kernel,
        grid=(h,),
        in_specs=[pl.BlockSpec((1, m, k), lambda i: (i, 0, 0)),
                  pl.BlockSpec((1, k, n), lambda i: (i, 0, 0))],
        out_specs=pl.BlockSpec((1, m, n), lambda i: (i, 0, 0)),
        out_shape=jax.ShapeDtypeStruct((h, m, n), F32),
        compiler_params=_cparams(("parallel",)),
        name="weight_fold",
    )(a, b)


def _gelu_tanh(x):
    c = math.sqrt(2.0 / math.pi)
    return 0.5 * x * (1.0 + jnp.tanh(c * (x + 0.044715 * (x * x * x))))


def _inproj_kernel(n_x, npt, *refs):
    x_refs, refs = refs[:n_x], refs[n_x:]
    (mod_ref, cos_ref, sin_ref, cost_ref, sint_ref, wa_ref, wuv_ref, wg_ref, gq_ref, gkv_ref,
     wqlt_ref, wqrt_ref, wqrst_ref, lng_ref, lnb_ref, wsp_ref, bs_ref,
     q_ref, klat_ref, vt_ref, sgu_ref, sg_ref) = refs
    tm = x_refs[0].shape[0]
    x = _read_rows(x_refs, npt)
    sh1 = mod_ref[0:1, :]
    sc1 = mod_ref[1:2, :]
    h = (x * (1.0 + sc1) + sh1).astype(BF16)

    za = _dot(h, wa_ref[...])
    cq = za[:, 0:Q_RANK]
    ckv = za[:, Q_RANK:Q_RANK + KV_RANK]
    kr = za[:, Q_RANK + KV_RANK:Q_RANK + KV_RANK + LANES]
    krs = za[:, Q_RANK + KV_RANK + LANES:Q_RANK + KV_RANK + 2 * LANES]
    cos = cos_ref[...]
    sin = sin_ref[...]
    cqn = _rms_norm(cq, gq_ref[...]).astype(BF16)
    ckvn = _rms_norm(ckv, gkv_ref[...])
    klat_ref[:, 0:LAT] = ckvn.astype(BF16)
    klat_ref[:, LAT:QK_PAD] = (kr * cos + krs * sin).astype(BF16)
    vt_ref[0:LAT, :] = ckvn.T.astype(BF16)
    extra = lax.broadcasted_iota(I32, (V_ROWS - LAT, tm), 0)
    vt_ref[LAT:V_ROWS, :] = jnp.where(extra == 0, 1.0, 0.0).astype(BF16)

    nt = (((1,), (1,)), ((), ()))
    qlt = lax.dot_general(wqlt_ref[...], cqn, nt, preferred_element_type=F32)
    qrt = lax.dot_general(wqrt_ref[...], cqn, nt, preferred_element_type=F32)
    qrst = lax.dot_general(wqrst_ref[...], cqn, nt, preferred_element_type=F32)
    cos_q = cost_ref[...] * Q_SCALE
    sin_q = sint_ref[...] * Q_SCALE
    pad = jnp.zeros((QK_PAD - LAT - QK_ROPE, tm), BF16)
    for hd in range(N_HEADS):
        q_ref[hd, 0:LAT, :] = (qlt[hd * LAT:(hd + 1) * LAT, :] * Q_SCALE).astype(BF16)
        rows = slice(hd * QK_ROPE, (hd + 1) * QK_ROPE)
        q_ref[hd, LAT:LAT + QK_ROPE, :] = (qrt[rows, :] * cos_q + qrst[rows, :] * sin_q).astype(BF16)
        q_ref[hd, LAT + QK_ROPE:QK_PAD, :] = pad

    sg_ref[...] = jax.nn.sigmoid(_dot(h, wg_ref[...])).astype(BF16)

    guv = _gelu_tanh(_dot(h, wuv_ref[...]))
    u = guv[:, :SGU_WIDTH]
    vn = _layer_norm(guv[:, SGU_WIDTH:], lng_ref[...], lnb_ref[...]).astype(BF16)
    lane = lax.broadcasted_iota(I32, (CHUNK, LANES), 1)
    low_half = lane < (LANES // 2)
    zero = jnp.zeros((CHUNK, LANES), BF16)
    for c in range(tm // CHUNK):
        rows = slice(c * CHUNK, (c + 1) * CHUNK)
        vc = vn[rows, :]
        mixed = []
        for j in range(SGU_WIDTH // LANES):
            blk = vc[:, j * LANES:(j + 1) * LANES]
            rhs = jnp.concatenate([jnp.where(low_half, blk, zero), jnp.where(low_half, zero, blk)], axis=0)
            mixed.append(_dot(wsp_ref[j], rhs))
        mixed = jnp.concatenate(mixed, axis=1) + bs_ref[...]
        sgu_ref[rows, :] = (u[rows, :] * mixed).astype(BF16)


def _inproj(x_parts, mod_l, rope, w, g, tm):
    t, d = g.t, x_parts[0].shape[1]
    n = t // tm
    seq = lambda i: (g.seq_of_tile(i, tm), 0, 0)
    pos = lambda i: (g.pos_tile(i, tm), 0)
    pos_t = lambda i: (0, g.pos_tile(i, tm))
    row = lambda i: (i, 0)
    col = lambda i: (0, i)
    cos_t, sin_t, cos_tt, sin_tt = rope
    consts = [w["wa"], w["wuv"], w["wg"], w["gq"], w["gkv"], w["wqlt"], w["wqrt"], w["wqrst"],
              w["lng"], w["lnb"], w["wsp"], w["bs"]]
    return pl.pallas_call(
        functools.partial(_inproj_kernel, len(x_parts), g.tp // tm),
        grid=(n,),
        in_specs=_row_part_specs(x_parts, tm, g) + [
                  pl.BlockSpec((None, mod_l.shape[1], d), seq),
                  pl.BlockSpec((tm, LANES), pos),
                  pl.BlockSpec((tm, LANES), pos),
                  pl.BlockSpec((QK_ROPE, tm), pos_t),
                  pl.BlockSpec((QK_ROPE, tm), pos_t)] + [_const_spec(c.shape) for c in consts],
        out_specs=[pl.BlockSpec((N_HEADS, QK_PAD, tm), lambda i: (0, 0, i)),
                   pl.BlockSpec((tm, QK_PAD), row),
                   pl.BlockSpec((V_ROWS, tm), col),
                   pl.BlockSpec((tm, SGU_WIDTH), row),
                   pl.BlockSpec((tm, 2 * d), row)],
        out_shape=[jax.ShapeDtypeStruct((N_HEADS, QK_PAD, t), BF16),
                   jax.ShapeDtypeStruct((t, QK_PAD), BF16),
                   jax.ShapeDtypeStruct((V_ROWS, t), BF16),
                   jax.ShapeDtypeStruct((t, SGU_WIDTH), BF16),
                   jax.ShapeDtypeStruct((t, 2 * d), BF16)],
        compiler_params=_cparams(("parallel",)),
        name="token_mixer_in",
    )(*x_parts, mod_l, cos_t, sin_t, cos_tt, sin_tt, *consts)


def _attn_kernel(qt_ref, k_ref, vt_ref, o_ref, m_ref, acc_ref, s_ref):
    ki = pl.program_id(2)
    nh = qt_ref.shape[0]

    @pl.when(ki == 0)
    def _():
        m_ref[...] = jnp.full(m_ref.shape, -jnp.inf, F32)
        acc_ref[...] = jnp.zeros(acc_ref.shape, F32)

    k = k_ref[...]
    tk = k.shape[0]
    def scores(hd):
        st = _dot(k, qt_ref[hd])
        s_ref[hd % 2] = st
        return jnp.max(st, axis=0, keepdims=True)

    mx_next = scores(0)
    for hd in range(nh):
        mx = mx_next
        if hd + 1 < nh:
            mx_next = scores(hd + 1)
        m_prev = m_ref[hd:hd + 1, :]
        m_new = jnp.maximum(m_prev, mx)
        alpha = jnp.exp2(m_prev - m_new)
        pv = None
        for kt in range(tk // KEY_CHUNK):
            keys = slice(kt * KEY_CHUNK, (kt + 1) * KEY_CHUNK)
            p = jnp.exp2(s_ref[hd % 2, keys, :] - m_new).astype(BF16)
            part = _dot(vt_ref[:, keys], p)
            pv = part if pv is None else pv + part
        acc_ref[hd] = alpha * acc_ref[hd] + pv
        m_ref[hd:hd + 1, :] = m_new

    @pl.when(ki == pl.num_programs(2) - 1)
    def _():
        for hd in range(nh):
            a = acc_ref[hd]
            o = a[0:LAT, :] / a[LAT:LAT + 1, :]
            o_ref[:, hd * LAT:(hd + 1) * LAT] = o.T.astype(BF16)


def _attention(qt, klat, vt, row0, nb, s, tq, tk):
    nq, nk = s // tq, s // tk
    q0, k0 = row0 // tq, row0 // tk
    return pl.pallas_call(
        _attn_kernel,
        grid=(nb, nq, nk),
        in_specs=[pl.BlockSpec((N_HEADS, QK_PAD, tq), lambda b, i, j: (0, 0, q0 + b * nq + i)),
                  pl.BlockSpec((tk, QK_PAD), lambda b, i, j: (k0 + b * nk + j, 0)),
                  pl.BlockSpec((V_ROWS, tk), lambda b, i, j: (0, k0 + b * nk + j))],
        out_specs=pl.BlockSpec((tq, N_HEADS * LAT), lambda b, i, j: (b * nq + i, 0)),
        out_shape=jax.ShapeDtypeStruct((nb * s, N_HEADS * LAT), BF16),
        scratch_shapes=[pltpu.VMEM((N_HEADS, tq), F32), pltpu.VMEM((N_HEADS, V_ROWS, tq), F32),
                        pltpu.VMEM((2, tk, tq), F32)],
        compiler_params=_cparams(("parallel", "parallel", "arbitrary")),
        name="latent_attention",
    )(qt, klat, vt)


def _outproj_kernel(n_x, npt, dn_alpha, op_ref, os_ref, sgu_ref, sg_ref, *refs):
    x_refs, refs = refs[:n_x], refs[n_x:]
    (mod_ref, wfold_ref, wosgu_ref, wout_ref, lng_ref, lnb_ref, rwt_ref, rb_ref,
     x1_ref, h2_ref, idx_ref, wts_ref) = refs
    tm, d = x1_ref.shape
    o = _read_rows((op_ref, os_ref), npt)
    o_attn = _dot(o, wfold_ref[...])
    o_sgu = _dot(sgu_ref[...], wosgu_ref[...])
    sg = sg_ref[...].astype(F32)
    merged = (sg[:, :d] * o_attn + sg[:, d:] * o_sgu).astype(BF16)
    mix = _dot(merged, wout_ref[...])
    g1 = mod_ref[2:3, :]
    sh2 = mod_ref[3:4, :]
    sc2 = mod_ref[4:5, :]
    x1 = _layer_norm(dn_alpha * _read_rows(x_refs, npt) + (1.0 + g1) * mix, lng_ref[...], lnb_ref[...])
    x1_ref[...] = x1
    h2 = x1 * (1.0 + sc2) + sh2
    h2_ref[...] = _pack_bf16_pairs(h2)

    logits = lax.dot_general(rwt_ref[...], h2, (((1,), (1,)), ((), ())), preferred_element_type=F32,
                             precision=lax.Precision.HIGHEST) + rb_ref[...]
    e_iota = lax.broadcasted_iota(I32, (N_EXPERTS, tm), 0)
    vals, idxs = [], []
    for _ in range(TOP_K):
        mx = jnp.max(logits, axis=0, keepdims=True)
        ix = jnp.min(jnp.where(logits == mx, e_iota, N_EXPERTS), axis=0, keepdims=True)
        vals.append(mx)
        idxs.append(ix)
        logits = jnp.where(e_iota == ix, -jnp.inf, logits)
    v = jnp.concatenate(vals, axis=0)
    w = jnp.exp(v - v[0:1, :])
    wts_ref[...] = w / jnp.sum(w, axis=0, keepdims=True)
    idx_ref[...] = jnp.concatenate(idxs, axis=0)


def _outproj(o_p, o_s, sgu, sg, x_parts, mod_l, w, g, tm, dn_alpha):
    t, d = g.t, x_parts[0].shape[1]
    n = t // tm
    seq = lambda i: (g.seq_of_tile(i, tm), 0, 0)
    row = lambda i: (i, 0)
    col = lambda i: (0, i)
    consts = [w["wfold"], w["wosgu"], w["wout"], w["ln1g"], w["ln1b"], w["rwt"], w["rb"]]
    return pl.pallas_call(
        functools.partial(_outproj_kernel, len(x_parts), g.tp // tm, dn_alpha),
        grid=(n,),
        in_specs=_row_part_specs((o_p, o_s), tm, g) + [
                  pl.BlockSpec((tm, SGU_WIDTH), row),
                  pl.BlockSpec((tm, 2 * d), row)] + _row_part_specs(x_parts, tm, g) + [
                  pl.BlockSpec((None, mod_l.shape[1], d), seq)] + [_const_spec(c.shape) for c in consts],
        out_specs=[pl.BlockSpec((tm, d), row),
                   pl.BlockSpec((tm, d // 2), row),
                   pl.BlockSpec((TOP_K, tm), col),
                   pl.BlockSpec((TOP_K, tm), col)],
        out_shape=[jax.ShapeDtypeStruct((t, d), F32),
                   jax.ShapeDtypeStruct((t, d // 2), I32),
                   jax.ShapeDtypeStruct((TOP_K, t), I32),
                   jax.ShapeDtypeStruct((TOP_K, t), F32)],
        compiler_params=_cparams(("parallel",)),
        name="token_mixer_out",
    )(o_p, o_s, sgu, sg, *x_parts, mod_l, *consts)


def _one_hots(idx, tm):
    e_iota = lax.broadcasted_iota(I32, (N_EXPERTS, tm), 0)
    return [(e_iota == idx[k:k + 1, :]).astype(F32) for k in range(TOP_K)]


def _count_kernel(idx_ref, cnt_ref):
    @pl.when(pl.program_id(0) == 0)
    def _():
        cnt_ref[...] = jnp.zeros(cnt_ref.shape, F32)

    oh = sum(_one_hots(idx_ref[...], idx_ref.shape[1]))
    cnt_ref[...] += jnp.sum(oh, axis=1, keepdims=True)


def _dest_kernel(idx_ref, start_ref, dest_ref, carry_ref):
    tm = idx_ref.shape[1]

    @pl.when(pl.program_id(0) == 0)
    def _():
        carry_ref[...] = start_ref[...]

    ohs = _one_hots(idx_ref[...], tm)
    oh = sum(ohs)
    r = lax.broadcasted_iota(I32, (tm, tm), 0)
    c = lax.broadcasted_iota(I32, (tm, tm), 1)
    before = (r < c).astype(BF16)
    rank = _dot(oh.astype(BF16), before) + carry_ref[:, 0:1]
    dest_ref[...] = jnp.concatenate(
        [jnp.sum(o * rank, axis=0, keepdims=True) for o in ohs], axis=0).astype(I32)
    carry_ref[...] += jnp.sum(oh, axis=1, keepdims=True)


def _route(idx, tm):
    k, t = idx.shape
    n = t // tm
    col = lambda i: (0, i)
    cnt = pl.pallas_call(
        _count_kernel,
        grid=(n,),
        in_specs=[pl.BlockSpec((k, tm), col)],
        out_specs=_const_spec((N_EXPERTS, LANES)),
        out_shape=jax.ShapeDtypeStruct((N_EXPERTS, LANES), F32),
        compiler_params=_cparams(("arbitrary",)),
        name="expert_counts",
    )(idx)
    counts = cnt[:, 0].astype(I32)
    padded = (counts + MOE_ROWS - 1) // MOE_ROWS * MOE_ROWS
    pad_end = jnp.cumsum(padded)
    pad_start = pad_end - padded
    start = jnp.broadcast_to(pad_start.astype(F32)[:, None], (N_EXPERTS, LANES))
    dest = pl.pallas_call(
        _dest_kernel,
        grid=(n,),
        in_specs=[pl.BlockSpec((k, tm), col), _const_spec((N_EXPERTS, LANES))],
        out_specs=pl.BlockSpec((k, tm), col),
        out_shape=jax.ShapeDtypeStruct((k, t), I32),
        scratch_shapes=[pltpu.VMEM((N_EXPERTS, LANES), F32)],
        compiler_params=_cparams(("arbitrary",)),
        name="expert_dest_rows",
    )(idx, start)
    return dest, counts, padded, pad_start, pad_end


def _count_le(sorted_vals, v):
    return jnp.sum((sorted_vals[None, :] <= v[:, None]).astype(I32), axis=1)


def _padding_rows(counts, padded, pad_start, pad_end, n_pad):
    per = padded - counts
    cum = jnp.cumsum(per)
    j = jnp.arange(n_pad, dtype=I32)
    e = _count_le(cum, j)
    ec = jnp.minimum(e, N_EXPERTS - 1)
    inside = pad_start[ec] + counts[ec] + (j - (cum[ec] - per[ec]))
    tail = pad_end[-1] + (j - cum[-1])
    return jnp.where(e < N_EXPERTS, inside, tail)


def _sc_mesh():
    return plsc.VectorSubcoreMesh(core_axis_name="c", subcore_axis_name="s")


def _sc_scatter_rows(src, idx_main, idx_pad, n_out):
    nw, cpk, c = idx_main.shape
    width = src.shape[1]
    cpw = src.shape[0] // (nw * c)
    copies = cpk // cpw
    ppw = idx_pad.shape[1]

    @functools.partial(
        pl.kernel, mesh=_sc_mesh(),
        out_type=jax.ShapeDtypeStruct((n_out, width), src.dtype),
        scratch_types=[pltpu.VMEM((cpk, c), I32), pltpu.VMEM((ppw, c), I32),
                       pltpu.VMEM((c, width), src.dtype), pltpu.SemaphoreType.DMA],
        name="dispatch_rows")
    def run(src_hbm, idx_hbm, pad_hbm, out_hbm, idx_v, pad_v, rows_v, sem):
        wid = lax.axis_index("s") * SC_CORES + lax.axis_index("c")
        pltpu.sync_copy(idx_hbm.at[wid], idx_v)
        pltpu.sync_copy(pad_hbm.at[wid], pad_v)

        @pl.loop(0, cpw)
        def _(j):
            pltpu.sync_copy(src_hbm.at[pl.ds((wid * cpw + j) * c, c)], rows_v)
            cps = [pltpu.async_copy(rows_v, out_hbm.at[idx_v.at[j * copies + kk]], sem)
                   for kk in range(copies)]
            for cp in cps:
                cp.wait()

        n_src_chunks = src.shape[0] // c

        @pl.loop(0, ppw)
        def _(j):
            pltpu.sync_copy(src_hbm.at[pl.ds(((wid * ppw + j) % n_src_chunks) * c, c)], rows_v)
            pltpu.async_copy(rows_v, out_hbm.at[pad_v.at[j]], sem).wait()

    return run(src, idx_main, idx_pad)


def _sc_gather_rows(src, idx):
    nw, cpw, c = idx.shape
    width = src.shape[1]

    @functools.partial(
        pl.kernel, mesh=_sc_mesh(),
        out_type=jax.ShapeDtypeStruct((nw * cpw * c, width), src.dtype),
        scratch_types=[pltpu.VMEM((cpw, c), I32), pltpu.VMEM((c, width), src.dtype),
                       pltpu.SemaphoreType.DMA],
        name="collect_rows")
    def run(src_hbm, idx_hbm, out_hbm, idx_v, rows_v, sem):
        wid = lax.axis_index("s") * SC_CORES + lax.axis_index("c")
        pltpu.sync_copy(idx_hbm.at[wid], idx_v)

        @pl.loop(0, cpw)
        def _(j):
            pltpu.async_copy(src_hbm.at[idx_v.at[j]], rows_v, sem).wait()
            pltpu.sync_copy(rows_v, out_hbm.at[pl.ds((wid * cpw + j) * c, c)])

    return run(src, idx)


def _dispatch(h2, dest, pad_rows, n_rows):
    k, t = dest.shape
    cpw = t // (SC_WORKERS * SC_CHUNK)
    idx_main = dest.reshape(k, SC_WORKERS, cpw, SC_CHUNK).transpose(1, 2, 0, 3)
    idx_main = idx_main.reshape(SC_WORKERS, cpw * k, SC_CHUNK)
    idx_pad = pad_rows.reshape(SC_WORKERS, -1, SC_CHUNK)
    return _sc_scatter_rows(h2, idx_main, idx_pad, n_rows)


def _collect(y, dest):
    k, t = dest.shape
    idx = dest.reshape(SC_WORKERS, (k * t) // (SC_WORKERS * SC_CHUNK), SC_CHUNK)
    return _sc_gather_rows(y, idx).reshape(k, t, y.shape[1])


def _moe_kernel(be_ref, x_ref, wgu_ref, bgu_ref, wd_ref, bd_ref, y_ref):
    lo, hi = _unpack_bf16_pairs(x_ref[...])
    x = jnp.concatenate([lo, hi], axis=1).astype(BF16)
    gu = _dot(x, wgu_ref[...]) + bgu_ref[...]
    acts = []
    for b in range(gu.shape[1] // GU_BLOCK):
        glu = jnp.minimum(gu[:, b * GU_BLOCK:b * GU_BLOCK + LANES], SWIGLU_LIMIT)
        lin = jnp.clip(gu[:, b * GU_BLOCK + LANES:(b + 1) * GU_BLOCK], -SWIGLU_LIMIT, SWIGLU_LIMIT)
        acts.append(glu * jax.nn.sigmoid(SWIGLU_ALPHA * glu) * (lin + 1.0))
    act = jnp.concatenate(acts, axis=1).astype(BF16)
    y_ref[...] = _pack_bf16_pairs(_dot(act, wd_ref[...]) + bd_ref[...])


def _moe_ffn(xs, block_e, w, l):
    n_rows, half = xs.shape
    d = 2 * half
    f2 = w["wgu"].shape[3]
    n_blocks = n_rows // MOE_ROWS
    row = lambda i, be: (i, 0)
    exp = lambda i, be: (l, be[i], 0, 0)
    return pl.pallas_call(
        _moe_kernel,
        grid_spec=pltpu.PrefetchScalarGridSpec(
            num_scalar_prefetch=1,
            grid=(n_blocks,),
            in_specs=[pl.BlockSpec((MOE_ROWS, half), row),
                      pl.BlockSpec((None, None, d, f2), exp), pl.BlockSpec((None, None, 1, f2), exp),
                      pl.BlockSpec((None, None, f2 // 2, d), exp), pl.BlockSpec((None, None, 1, d), exp)],
            out_specs=pl.BlockSpec((MOE_ROWS, half), row)),
        out_shape=jax.ShapeDtypeStruct((n_rows, half), I32),
        compiler_params=_cparams(("arbitrary",)),
        name="expert_ffn",
    )(block_e, xs, w["wgu"], w["bgu"], w["wdown"], w["bdown"])


def _combine_kernel(n_out, npt, dn_alpha, yg_ref, wts_ref, x1_ref, mod_ref, lng_ref, lnb_ref, *x2_refs):
    wts = wts_ref[...]
    lo = hi = None
    for k in range(TOP_K):
        l_k, h_k = _unpack_bf16_pairs(yg_ref[k])
        w_k = wts[:, k:k + 1]
        lo = l_k * w_k if lo is None else lo + l_k * w_k
        hi = h_k * w_k if hi is None else hi + h_k * w_k
    ff = jnp.concatenate([lo, hi], axis=1)
    g2 = mod_ref[5:6, :]
    x2 = _layer_norm(dn_alpha * x1_ref[...] + (1.0 + g2) * ff, lng_ref[...], lnb_ref[...])
    _write_rows(x2_refs[:n_out], npt, x2)


def _combine(yg, wts_t, x1, mod_l, ln_g, ln_b, g, tm, dn_alpha, split_out):
    t, d = x1.shape
    row = lambda i: (i, 0)
    seq = lambda i: (g.seq_of_tile(i, tm), 0, 0)
    if split_out:
        outs = [jax.ShapeDtypeStruct((g.tp, d), F32), jax.ShapeDtypeStruct((g.ts, d), F32)]
    else:
        outs = [jax.ShapeDtypeStruct((t, d), F32)]
    return pl.pallas_call(
        functools.partial(_combine_kernel, len(outs), g.tp // tm, dn_alpha),
        grid=(t // tm,),
        in_specs=[pl.BlockSpec((TOP_K, tm, d // 2), lambda i: (0, i, 0)),
                  pl.BlockSpec((tm, TOP_K), row),
                  pl.BlockSpec((tm, d), row),
                  pl.BlockSpec((None, mod_l.shape[1], d), seq),
                  _const_spec(ln_g.shape), _const_spec(ln_b.shape)],
        out_specs=_row_part_specs(outs, tm, g),
        out_shape=outs,
        compiler_params=_cparams(("arbitrary",)),
        name="moe_combine",
    )(yg, wts_t, x1, mod_l, ln_g, ln_b)


def _deinterleave_kernel(w_ref, perm_ref, o_ref):
    w = w_ref[...].astype(BF16)
    for b in range(w.shape[1] // GU_BLOCK):
        cols = slice(b * GU_BLOCK, (b + 1) * GU_BLOCK)
        o_ref[:, cols] = _dot(w[:, cols], perm_ref[...]).astype(BF16)


def _deinterleave_gate_up(w_gate_up):
    depth, ne, d, f2 = w_gate_up.shape
    tn = _pick(f2, (1024, 512, GU_BLOCK))
    j = jnp.arange(GU_BLOCK)
    target = (j % 2) * LANES + j // 2
    perm = (target[:, None] == jnp.arange(GU_BLOCK)[None, :]).astype(BF16)
    out = pl.pallas_call(
        _deinterleave_kernel,
        grid=(depth * ne, f2 // tn),
        in_specs=[pl.BlockSpec((None, d, tn), lambda e, j: (e, 0, j)), _const_spec((GU_BLOCK, GU_BLOCK))],
        out_specs=pl.BlockSpec((None, d, tn), lambda e, j: (e, 0, j)),
        out_shape=jax.ShapeDtypeStruct((depth * ne, d, f2), BF16),
        compiler_params=_cparams(("parallel", "parallel")),
        name="gate_up_columns",
    )(w_gate_up.reshape(depth * ne, d, f2), perm)
    return out.reshape(depth, ne, d, f2)


def _deinterleave_bias(b_gate_up):
    depth, ne, f2 = b_gate_up.shape
    b = b_gate_up.reshape(depth, ne, f2 // GU_BLOCK, LANES, 2).transpose(0, 1, 2, 4, 3)
    return b.reshape(depth, ne, 1, f2)


def _rope_tables(seq_len):
    inv = 1.0 / (ROPE_BASE ** (jnp.arange(0, QK_ROPE, 2, dtype=F32) / QK_ROPE))
    ang = jnp.arange(seq_len, dtype=F32)[:, None] * inv[None, :]
    pad = jnp.zeros((seq_len, LANES - QK_ROPE), F32)
    cos2 = jnp.concatenate([jnp.cos(ang), jnp.cos(ang)], axis=1)
    sin2 = jnp.concatenate([jnp.sin(ang), jnp.sin(ang)], axis=1)
    return (jnp.concatenate([cos2, pad], axis=1), jnp.concatenate([sin2, pad], axis=1), cos2.T, sin2.T)


def _rot_pair(w):
    half = w.shape[-1] // 2
    return jnp.concatenate([-w[..., half:], w[..., :half]], axis=-1)


def _pad_lanes(w):
    return jnp.pad(w, [(0, 0)] * (w.ndim - 1) + [(0, LANES - w.shape[-1])])


def _prep_layer(l, p):
    d = p["w_in"].shape[1]
    w_in = p["w_in"][l]
    i1 = Q_RANK
    i2 = i1 + KV_RANK
    i3 = i2 + QK_ROPE
    i4 = i3 + 2 * SGU_WIDTH
    w_kr = w_in[:, i2:i3]
    w = {}
    w["wa"] = jnp.concatenate([w_in[:, :i2], _pad_lanes(w_kr), _pad_lanes(_rot_pair(w_kr))], axis=1).astype(BF16)
    w["wuv"] = w_in[:, i3:i4].astype(BF16)
    w["wg"] = w_in[:, i4:].astype(BF16)
    w["gq"] = p["q_norm_g"][l][None, :]
    w["gkv"] = p["kv_norm_g"][l][None, :]

    w_uq = p["w_uq"][l].reshape(Q_RANK, N_HEADS, QK_NOPE + QK_ROPE)
    w_ukv = p["w_ukv"][l].reshape(KV_RANK, N_HEADS, QK_NOPE + V_DIM)
    uq_nope = w_uq[:, :, :QK_NOPE].transpose(1, 0, 2)
    uk_t = w_ukv[:, :, :QK_NOPE].transpose(1, 2, 0)
    w["wqlt"] = _bmm(uq_nope, uk_t).transpose(0, 2, 1).reshape(N_HEADS * LAT, Q_RANK).astype(BF16)
    uq_rope = w_uq[:, :, QK_NOPE:]
    w["wqrt"] = uq_rope.reshape(Q_RANK, N_HEADS * QK_ROPE).T.astype(BF16)
    w["wqrst"] = _rot_pair(uq_rope).reshape(Q_RANK, N_HEADS * QK_ROPE).T.astype(BF16)
    uv = w_ukv[:, :, QK_NOPE:].transpose(1, 0, 2)
    wo = p["w_o_attn"][l].reshape(N_HEADS, V_DIM, d)
    w["wfold"] = _bmm(uv, wo).reshape(N_HEADS * LAT, d).astype(BF16)

    w["lng"] = p["sgu_ln_g"][l][None, :]
    w["lnb"] = p["sgu_ln_b"][l][None, :]
    gpl = LANES // (SGU_WIDTH // SGU_GROUPS)
    ws = p["w_s"][l].reshape(SGU_GROUPS // gpl, gpl, CHUNK, CHUNK)
    w["wsp"] = ws.transpose(0, 2, 1, 3).reshape(SGU_GROUPS // gpl, CHUNK, gpl * CHUNK).astype(BF16)
    w["bs"] = jnp.repeat(p["b_s"][l].T, SGU_WIDTH // SGU_GROUPS, axis=1)
    w["wosgu"] = p["w_o_sgu"][l].astype(BF16)
    w["wout"] = p["w_out"][l].astype(BF16)
    w["ln1g"] = p["ln1_g"][l][None, :]
    w["ln1b"] = p["ln1_b"][l][None, :]
    w["rwt"] = p["router_w"][l].T
    w["rb"] = p["router_b"][l][:, None]
    w["ln2g"] = p["ln2_g"][l][None, :]
    w["ln2b"] = p["ln2_b"][l][None, :]
    return w


def _moe_layer(l, h2, idx, wts, x1, mod_l, w, g, tm, dn_alpha, split_out):
    t = x1.shape[0]
    n_pad = N_EXPERTS * MOE_ROWS
    n_rows = TOP_K * t + n_pad
    dest, counts, padded, pad_start, pad_end = _route(idx, _pick(t, (512, 256, 128)))
    pad_rows = _padding_rows(counts, padded, pad_start, pad_end, n_pad)
    block_start = jnp.arange(n_rows // MOE_ROWS, dtype=I32) * MOE_ROWS
    block_e = jnp.minimum(_count_le(pad_end, block_start), N_EXPERTS - 1)
    xs = _dispatch(h2, dest, pad_rows, n_rows)
    y = _moe_ffn(xs, block_e, w, l)
    yg = _collect(y, dest)
    return _combine(yg, wts.T, x1, mod_l, w["ln2g"], w["ln2b"], g, tm, dn_alpha, split_out)


def kernel(x_prompt, x_sample, c_prompt, c_sample, w_mod, b_mod, w_in, q_norm_g, kv_norm_g, w_uq, w_ukv, w_o_attn, sgu_ln_g, sgu_ln_b, w_s, b_s, w_o_sgu, w_out, ln1_g, ln1_b, router_w, router_b, w_gate_up, b_gate_up, w_down, b_down, ln2_g, ln2_b):
    p = dict(w_in=w_in, q_norm_g=q_norm_g, kv_norm_g=kv_norm_g, w_uq=w_uq, w_ukv=w_ukv, w_o_attn=w_o_attn,
             sgu_ln_g=sgu_ln_g, sgu_ln_b=sgu_ln_b, w_s=w_s, b_s=b_s, w_o_sgu=w_o_sgu, w_out=w_out,
             ln1_g=ln1_g, ln1_b=ln1_b, router_w=router_w, router_b=router_b, ln2_g=ln2_g, ln2_b=ln2_b)
    depth = w_mod.shape[0]
    bp, sp, d = x_prompt.shape
    bs, ss, _ = x_sample.shape
    g = _Groups(bp, sp, bs, ss)
    dn_alpha = (2 * depth) ** 0.25

    tm = _pick(math.gcd(sp, ss), (512, 256, 128))
    tq = _pick(math.gcd(sp, ss), (512, 256, 128))
    tk_p = _pick(sp, (2048, 1024, 512, 256))
    tk_s = _pick(ss, (2048, 1024, 512, 256))
    assert g.t % (SC_WORKERS * SC_CHUNK) == 0 and (N_EXPERTS * MOE_ROWS) % (SC_WORKERS * SC_CHUNK) == 0

    n_seq = bp + bs
    ns_pad = -(-n_seq // 8) * 8
    c_all = jnp.pad(jnp.concatenate([c_prompt, c_sample], axis=0), ((0, ns_pad - n_seq), (0, 0)))
    mod = _modulation(c_all, w_mod, b_mod)
    rope = _rope_tables(max(sp, ss))
    moe_w = dict(wgu=_deinterleave_gate_up(w_gate_up), bgu=_deinterleave_bias(b_gate_up),
                 wdown=w_down.astype(BF16), bdown=b_down[:, :, None, :])
    x_parts = [x_prompt.reshape(g.tp, d), x_sample.reshape(g.ts, d)]

    for l in range(depth):
        w = dict(_prep_layer(l, p), **moe_w)
        qt, klat, vt, sgu, sg = _inproj(x_parts, mod[l], rope, w, g, tm)
        o_p = _attention(qt, klat, vt, 0, bp, sp, tq, tk_p)
        o_s = _attention(qt, klat, vt, g.tp, bs, ss, tq, tk_s)
        x1, h2, idx, wts = _outproj(o_p, o_s, sgu, sg, x_parts, mod[l], w, g, tm, dn_alpha)
        x_parts = _moe_layer(l, h2, idx, wts, x1, mod[l], w, g, tm, dn_alpha, split_out=(l == depth - 1))

    return x_parts[0].reshape(bp, sp, d), x_parts[1].reshape(bs, ss, d)
```

```python
import functools
import math

import jax
import jax.numpy as jnp
from jax import lax
from jax.experimental import pallas as pl
from jax.experimental.pallas import tpu as pltpu
from jax.experimental.pallas import tpu_sc as plsc

F32 = jnp.float32
BF16 = jnp.bfloat16
I32 = jnp.int32

N_HEADS = 8
QK_NOPE = 64
QK_ROPE = 32
V_DIM = 64
Q_RANK = 256
KV_RANK = 128
ROPE_BASE = 10000.0
CHUNK = 128
SGU_WIDTH = 512
SGU_GROUPS = 8
N_EXPERTS = 32
TOP_K = 4
SWIGLU_LIMIT = 7.0
SWIGLU_ALPHA = 1.702
LN_EPS = 1e-5
RMS_EPS = 1e-6
ATTN_SCALE = 1.0 / math.sqrt(QK_NOPE + QK_ROPE)
Q_SCALE = ATTN_SCALE * math.log2(math.e)

LANES = 128
LAT = KV_RANK
QK_PAD = 2 * LANES
V_ROWS = LAT + 16
KEY_CHUNK = 256
MOE_ROWS = 512
GU_BLOCK = 2 * LANES
VMEM_LIMIT = 56 * 1024 * 1024
SC_CORES = 2
SC_SUBCORES = 16
SC_WORKERS = SC_CORES * SC_SUBCORES
SC_CHUNK = 64


def _cparams(sem, vmem=VMEM_LIMIT):
    return pltpu.CompilerParams(dimension_semantics=sem, vmem_limit_bytes=vmem)


def _pick(n, prefs):
    for p in prefs:
        if n % p == 0:
            return p
    raise ValueError(f"no tile in {prefs} divides {n}")


def _const_spec(shape):
    nd = len(shape)
    return pl.BlockSpec(shape, lambda *_: (0,) * nd)


def _dot(a, b):
    return jnp.dot(a, b, preferred_element_type=F32)


def _layer_norm(y, g, b):
    mu = jnp.mean(y, axis=-1, keepdims=True)
    d = y - mu
    var = jnp.mean(d * d, axis=-1, keepdims=True)
    return d * lax.rsqrt(var + LN_EPS) * g + b


def _rms_norm(y, g):
    return y * lax.rsqrt(jnp.mean(y * y, axis=-1, keepdims=True) + RMS_EPS) * g


HIGH_HALF = -65536


def _pack_bf16_pairs(y):
    w = y.shape[1] // 2
    bits = lax.bitcast_convert_type(y.astype(BF16).astype(F32), I32)
    return (bits[:, w:] & HIGH_HALF) | lax.shift_right_logical(bits[:, :w], 16)


def _unpack_bf16_pairs(p):
    lo = lax.bitcast_convert_type(lax.shift_left(p, 16), F32)
    hi = lax.bitcast_convert_type(p & HIGH_HALF, F32)
    return lo, hi


class _Seqs:
    def __init__(self, b, s, seq0):
        self.b, self.s, self.seq0 = b, s, seq0
        self.t = b * s

    def seq_of_tile(self, i, tm):
        return self.seq0 + (i * tm) // self.s

    def pos_tile(self, i, tm):
        return i % (self.s // tm)


def _mod_kernel(c_ref, w_ref, b_ref, o_ref):
    c = c_ref[...]
    a = c * jax.nn.sigmoid(c)
    o_ref[0] = jnp.dot(a, w_ref[0], preferred_element_type=F32,
                       precision=lax.Precision.HIGHEST) + b_ref[0]


def _modulation(c_all, w_mod, b_mod):
    depth, d, six_d = w_mod.shape
    ns = c_all.shape[0]
    n_col = six_d // d
    out = pl.pallas_call(
        _mod_kernel,
        grid=(depth, n_col),
        in_specs=[
            pl.BlockSpec((ns, d), lambda l, j: (0, 0)),
            pl.BlockSpec((1, d, d), lambda l, j: (l, 0, j)),
            pl.BlockSpec((1, 1, d), lambda l, j: (l, 0, j)),
        ],
        out_specs=pl.BlockSpec((1, ns, d), lambda l, j: (l, 0, j)),
        out_shape=jax.ShapeDtypeStruct((depth, ns, six_d), F32),
        compiler_params=_cparams(("parallel", "parallel")),
        name="modulation",
    )(c_all, w_mod, b_mod.reshape(depth, 1, six_d))
    return out.reshape(depth, ns, n_col, d)


def _bmm_kernel(a_ref, b_ref, o_ref):
    o_ref[0] = jnp.dot(a_ref[0], b_ref[0], preferred_element_type=F32,
                       precision=lax.Precision.HIGHEST)


def _bmm(a, b):
    h, m, k = a.shape
    n = b.shape[2]
    return pl.pallas_call(
        _bmm_kernel,
        grid=(h,),
        in_specs=[pl.BlockSpec((1, m, k), lambda i: (i, 0, 0)),
                  pl.BlockSpec((1, k, n), lambda i: (i, 0, 0))],
        out_specs=pl.BlockSpec((1, m, n), lambda i: (i, 0, 0)),
        out_shape=jax.ShapeDtypeStruct((h, m, n), F32),
        compiler_params=_cparams(("parallel",)),
        name="weight_fold",
    )(a, b)


def _gelu_tanh(x):
    c = math.sqrt(2.0 / math.pi)
    return 0.5 * x * (1.0 + jnp.tanh(c * (x + 0.044715 * (x * x * x))))


def _inproj_kernel(x_ref, mod_ref, cos_ref, sin_ref, cost_ref, sint_ref, wa_ref, wuv_ref, wg_ref, gq_ref,
                   gkv_ref, wqlt_ref, wqrt_ref, wqrst_ref, lng_ref, lnb_ref, wsp_ref, bs_ref,
                   q_ref, klat_ref, vt_ref, sgu_ref, sg_ref):
    tm = x_ref.shape[0]
    x = x_ref[...]
    sh1 = mod_ref[0:1, :]
    sc1 = mod_ref[1:2, :]
    h = (x * (1.0 + sc1) + sh1).astype(BF16)

    za = _dot(h, wa_ref[...])
    cq = za[:, 0:Q_RANK]
    ckv = za[:, Q_RANK:Q_RANK + KV_RANK]
    kr = za[:, Q_RANK + KV_RANK:Q_RANK + KV_RANK + LANES]
    krs = za[:, Q_RANK + KV_RANK + LANES:Q_RANK + KV_RANK + 2 * LANES]
    cos = cos_ref[...]
    sin = sin_ref[...]
    cqn = _rms_norm(cq, gq_ref[...]).astype(BF16)
    ckvn = _rms_norm(ckv, gkv_ref[...])
    klat_ref[:, 0:LAT] = ckvn.astype(BF16)
    klat_ref[:, LAT:QK_PAD] = (kr * cos + krs * sin).astype(BF16)
    vt_ref[0:LAT, :] = ckvn.T.astype(BF16)
    extra = lax.broadcasted_iota(I32, (V_ROWS - LAT, tm), 0)
    vt_ref[LAT:V_ROWS, :] = jnp.where(extra == 0, 1.0, 0.0).astype(BF16)

    nt = (((1,), (1,)), ((), ()))
    qlt = lax.dot_general(wqlt_ref[...], cqn, nt, preferred_element_type=F32)
    qrt = lax.dot_general(wqrt_ref[...], cqn, nt, preferred_element_type=F32)
    qrst = lax.dot_general(wqrst_ref[...], cqn, nt, preferred_element_type=F32)
    cos_q = cost_ref[...] * Q_SCALE
    sin_q = sint_ref[...] * Q_SCALE
    pad = jnp.zeros((QK_PAD - LAT - QK_ROPE, tm), BF16)
    for hd in range(N_HEADS):
        q_ref[hd, 0:LAT, :] = (qlt[hd * LAT:(hd + 1) * LAT, :] * Q_SCALE).astype(BF16)
        rows = slice(hd * QK_ROPE, (hd + 1) * QK_ROPE)
        q_ref[hd, LAT:LAT + QK_ROPE, :] = (qrt[rows, :] * cos_q + qrst[rows, :] * sin_q).astype(BF16)
        q_ref[hd, LAT + QK_ROPE:QK_PAD, :] = pad

    sg_ref[...] = jax.nn.sigmoid(_dot(h, wg_ref[...])).astype(BF16)

    guv = _gelu_tanh(_dot(h, wuv_ref[...]))
    u = guv[:, :SGU_WIDTH]
    vn = _layer_norm(guv[:, SGU_WIDTH:], lng_ref[...], lnb_ref[...]).astype(BF16)
    lane = lax.broadcasted_iota(I32, (CHUNK, LANES), 1)
    low_half = lane < (LANES // 2)
    zero = jnp.zeros((CHUNK, LANES), BF16)
    for c in range(tm // CHUNK):
        rows = slice(c * CHUNK, (c + 1) * CHUNK)
        vc = vn[rows, :]
        mixed = []
        for j in range(SGU_WIDTH // LANES):
            blk = vc[:, j * LANES:(j + 1) * LANES]
            rhs = jnp.concatenate([jnp.where(low_half, blk, zero), jnp.where(low_half, zero, blk)], axis=0)
            mixed.append(_dot(wsp_ref[j], rhs))
        mixed = jnp.concatenate(mixed, axis=1) + bs_ref[...]
        sgu_ref[rows, :] = (u[rows, :] * mixed).astype(BF16)


def _inproj(x, mod_l, rope, w, g, tm):
    t, d = x.shape
    n = t // tm
    seq = lambda i: (g.seq_of_tile(i, tm), 0, 0)
    pos = lambda i: (g.pos_tile(i, tm), 0)
    pos_t = lambda i: (0, g.pos_tile(i, tm))
    row = lambda i: (i, 0)
    col = lambda i: (0, i)
    cos_t, sin_t, cos_tt, sin_tt = rope
    consts = [w["wa"], w["wuv"], w["wg"], w["gq"], w["gkv"], w["wqlt"], w["wqrt"], w["wqrst"],
              w["lng"], w["lnb"], w["wsp"], w["bs"]]
    return pl.pallas_call(
        _inproj_kernel,
        grid=(n,),
        in_specs=[pl.BlockSpec((tm, d), row),
                  pl.BlockSpec((None, mod_l.shape[1], d), seq),
                  pl.BlockSpec((tm, LANES), pos),
                  pl.BlockSpec((tm, LANES), pos),
                  pl.BlockSpec((QK_ROPE, tm), pos_t),
                  pl.BlockSpec((QK_ROPE, tm), pos_t)] + [_const_spec(c.shape) for c in consts],
        out_specs=[pl.BlockSpec((N_HEADS, QK_PAD, tm), lambda i: (0, 0, i)),
                   pl.BlockSpec((tm, QK_PAD), row),
                   pl.BlockSpec((V_ROWS, tm), col),
                   pl.BlockSpec((tm, SGU_WIDTH), row),
                   pl.BlockSpec((tm, 2 * d), row)],
        out_shape=[jax.ShapeDtypeStruct((N_HEADS, QK_PAD, t), BF16),
                   jax.ShapeDtypeStruct((t, QK_PAD), BF16),
                   jax.ShapeDtypeStruct((V_ROWS, t), BF16),
                   jax.ShapeDtypeStruct((t, SGU_WIDTH), BF16),
                   jax.ShapeDtypeStruct((t, 2 * d), BF16)],
        compiler_params=_cparams(("parallel",)),
        name="token_mixer_in",
    )(x, mod_l, cos_t, sin_t, cos_tt, sin_tt, *consts)


def _attn_kernel(qt_ref, k_ref, vt_ref, o_ref, m_ref, acc_ref, s_ref):
    ki = pl.program_id(2)
    nh = qt_ref.shape[0]

    @pl.when(ki == 0)
    def _():
        m_ref[...] = jnp.full(m_ref.shape, -jnp.inf, F32)
        acc_ref[...] = jnp.zeros(acc_ref.shape, F32)

    k = k_ref[...]
    tk = k.shape[0]
    def scores(hd):
        st = _dot(k, qt_ref[hd])
        s_ref[hd % 2] = st
        return jnp.max(st, axis=0, keepdims=True)

    mx_next = scores(0)
    for hd in range(nh):
        mx = mx_next
        if hd + 1 < nh:
            mx_next = scores(hd + 1)
        m_prev = m_ref[hd:hd + 1, :]
        m_new = jnp.maximum(m_prev, mx)
        alpha = jnp.exp2(m_prev - m_new)
        pv = None
        for kt in range(tk // KEY_CHUNK):
            keys = slice(kt * KEY_CHUNK, (kt + 1) * KEY_CHUNK)
            p = jnp.exp2(s_ref[hd % 2, keys, :] - m_new).astype(BF16)
            part = _dot(vt_ref[:, keys], p)
            pv = part if pv is None else pv + part
        acc_ref[hd] = alpha * acc_ref[hd] + pv
        m_ref[hd:hd + 1, :] = m_new

    @pl.when(ki == pl.num_programs(2) - 1)
    def _():
        for hd in range(nh):
            a = acc_ref[hd]
            o = a[0:LAT, :] / a[LAT:LAT + 1, :]
            o_ref[:, hd * LAT:(hd + 1) * LAT] = o.T.astype(BF16)


def _attention(qt, klat, vt, nb, s, tq, tk):
    nq, nk = s // tq, s // tk
    return pl.pallas_call(
        _attn_kernel,
        grid=(nb, nq, nk),
        in_specs=[pl.BlockSpec((N_HEADS, QK_PAD, tq), lambda b, i, j: (0, 0, b * nq + i)),
                  pl.BlockSpec((tk, QK_PAD), lambda b, i, j: (b * nk + j, 0)),
                  pl.BlockSpec((V_ROWS, tk), lambda b, i, j: (0, b * nk + j))],
        out_specs=pl.BlockSpec((tq, N_HEADS * LAT), lambda b, i, j: (b * nq + i, 0)),
        out_shape=jax.ShapeDtypeStruct((nb * s, N_HEADS * LAT), BF16),
        scratch_shapes=[pltpu.VMEM((N_HEADS, tq), F32), pltpu.VMEM((N_HEADS, V_ROWS, tq), F32),
                        pltpu.VMEM((2, tk, tq), F32)],
        compiler_params=_cparams(("parallel", "parallel", "arbitrary")),
        name="latent_attention",
    )(qt, klat, vt)


def _outproj_kernel(dn_alpha, o_ref, sgu_ref, sg_ref, x_ref, mod_ref, wfold_ref, wosgu_ref, wout_ref,
                    lng_ref, lnb_ref, rwt_ref, rb_ref, x1_ref, h2_ref, idx_ref, wts_ref):
    d = x_ref.shape[1]
    o_attn = _dot(o_ref[...], wfold_ref[...])
    o_sgu = _dot(sgu_ref[...], wosgu_ref[...])
    sg = sg_ref[...].astype(F32)
    merged = (sg[:, :d] * o_attn + sg[:, d:] * o_sgu).astype(BF16)
    mix = _dot(merged, wout_ref[...])
    g1 = mod_ref[2:3, :]
    sh2 = mod_ref[3:4, :]
    sc2 = mod_ref[4:5, :]
    x1 = _layer_norm(dn_alpha * x_ref[...] + (1.0 + g1) * mix, lng_ref[...], lnb_ref[...])
    x1_ref[...] = x1
    h2 = x1 * (1.0 + sc2) + sh2
    h2_ref[...] = _pack_bf16_pairs(h2)

    logits = lax.dot_general(rwt_ref[...], h2, (((1,), (1,)), ((), ())), preferred_element_type=F32,
                             precision=lax.Precision.HIGHEST) + rb_ref[...]
    e_iota = lax.broadcasted_iota(I32, logits.shape, 0)
    vals, idxs = [], []
    for _ in range(TOP_K):
        mx = jnp.max(logits, axis=0, keepdims=True)
        ix = jnp.min(jnp.where(logits == mx, e_iota, N_EXPERTS), axis=0, keepdims=True)
        vals.append(mx)
        idxs.append(ix)
        logits = jnp.where(e_iota == ix, -jnp.inf, logits)
    v = jnp.concatenate(vals, axis=0)
    w = jnp.exp(v - v[0:1, :])
    wts_ref[...] = w / jnp.sum(w, axis=0, keepdims=True)
    idx_ref[...] = jnp.concatenate(idxs, axis=0)


def _outproj(o, sgu, sg, x, mod_l, w, g, tm, dn_alpha):
    t, d = x.shape
    n = t // tm
    seq = lambda i: (g.seq_of_tile(i, tm), 0, 0)
    row = lambda i: (i, 0)
    col = lambda i: (0, i)
    consts = [w["wfold"], w["wosgu"], w["wout"], w["ln1g"], w["ln1b"], w["rwt"], w["rb"]]
    return pl.pallas_call(
        functools.partial(_outproj_kernel, dn_alpha),
        grid=(n,),
        in_specs=[pl.BlockSpec((tm, N_HEADS * LAT), row),
                  pl.BlockSpec((tm, SGU_WIDTH), row),
                  pl.BlockSpec((tm, 2 * d), row),
                  pl.BlockSpec((tm, d), row),
                  pl.BlockSpec((None, mod_l.shape[1], d), seq)] + [_const_spec(c.shape) for c in consts],
        out_specs=[pl.BlockSpec((tm, d), row),
                   pl.BlockSpec((tm, d // 2), row),
                   pl.BlockSpec((TOP_K, tm), col),
                   pl.BlockSpec((TOP_K, tm), col)],
        out_shape=[jax.ShapeDtypeStruct((t, d), F32),
                   jax.ShapeDtypeStruct((t, d // 2), I32),
                   jax.ShapeDtypeStruct((TOP_K, t), I32),
                   jax.ShapeDtypeStruct((TOP_K, t), F32)],
        compiler_params=_cparams(("parallel",)),
        name="token_mixer_out",
    )(o, sgu, sg, x, mod_l, *consts)


def _one_hots(idx, tm):
    e_iota = lax.broadcasted_iota(I32, (N_EXPERTS, tm), 0)
    return [(e_iota == idx[k:k + 1, :]).astype(F32) for k in range(TOP_K)]


def _count_kernel(idx_ref, cnt_ref):
    @pl.when(pl.program_id(0) == 0)
    def _():
        cnt_ref[...] = jnp.zeros(cnt_ref.shape, F32)

    oh = sum(_one_hots(idx_ref[...], idx_ref.shape[1]))
    cnt_ref[...] += jnp.sum(oh, axis=1, keepdims=True)


def _dest_kernel(idx_ref, start_ref, dest_ref, carry_ref):
    tm = idx_ref.shape[1]

    @pl.when(pl.program_id(0) == 0)
    def _():
        carry_ref[...] = start_ref[...]

    ohs = _one_hots(idx_ref[...], tm)
    oh = sum(ohs)
    r = lax.broadcasted_iota(I32, (tm, tm), 0)
    c = lax.broadcasted_iota(I32, (tm, tm), 1)
    before = (r < c).astype(BF16)
    rank = _dot(oh.astype(BF16), before) + carry_ref[:, 0:1]
    dest_ref[...] = jnp.concatenate(
        [jnp.sum(o * rank, axis=0, keepdims=True) for o in ohs], axis=0).astype(I32)
    carry_ref[...] += jnp.sum(oh, axis=1, keepdims=True)


def _route(idx, tm):
    k, t = idx.shape
    n = t // tm
    col = lambda i: (0, i)
    cnt = pl.pallas_call(
        _count_kernel,
        grid=(n,),
        in_specs=[pl.BlockSpec((k, tm), col)],
        out_specs=_const_spec((N_EXPERTS, LANES)),
        out_shape=jax.ShapeDtypeStruct((N_EXPERTS, LANES), F32),
        compiler_params=_cparams(("arbitrary",)),
        name="expert_counts",
    )(idx)
    counts = cnt[:, 0].astype(I32)
    padded = (counts + MOE_ROWS - 1) // MOE_ROWS * MOE_ROWS
    pad_end = jnp.cumsum(padded)
    pad_start = pad_end - padded
    start = jnp.broadcast_to(pad_start.astype(F32)[:, None], (N_EXPERTS, LANES))
    dest = pl.pallas_call(
        _dest_kernel,
        grid=(n,),
        in_specs=[pl.BlockSpec((k, tm), col), _const_spec((N_EXPERTS, LANES))],
        out_specs=pl.BlockSpec((k, tm), col),
        out_shape=jax.ShapeDtypeStruct((k, t), I32),
        scratch_shapes=[pltpu.VMEM((N_EXPERTS, LANES), F32)],
        compiler_params=_cparams(("arbitrary",)),
        name="expert_dest_rows",
    )(idx, start)
    return dest, counts, padded, pad_start, pad_end


def _count_le(sorted_vals, v):
    return jnp.sum((sorted_vals[None, :] <= v[:, None]).astype(I32), axis=1)


def _padding_rows(counts, padded, pad_start, pad_end, n_pad):
    per = padded - counts
    cum = jnp.cumsum(per)
    j = jnp.arange(n_pad, dtype=I32)
    e = _count_le(cum, j)
    ec = jnp.minimum(e, N_EXPERTS - 1)
    inside = pad_start[ec] + counts[ec] + (j - (cum[ec] - per[ec]))
    tail = pad_end[-1] + (j - cum[-1])
    return jnp.where(e < N_EXPERTS, inside, tail)


def _sc_mesh():
    return plsc.VectorSubcoreMesh(core_axis_name="c", subcore_axis_name="s")


def _sc_scatter_rows(src, idx_main, idx_pad, n_out):
    nw, cpk, c = idx_main.shape
    width = src.shape[1]
    cpw = src.shape[0] // (nw * c)
    copies = cpk // cpw
    ppw = idx_pad.shape[1]

    @functools.partial(
        pl.kernel, mesh=_sc_mesh(),
        out_type=jax.ShapeDtypeStruct((n_out, width), src.dtype),
        scratch_types=[pltpu.VMEM((cpk, c), I32), pltpu.VMEM((ppw, c), I32),
                       pltpu.VMEM((c, width), src.dtype), pltpu.SemaphoreType.DMA],
        name="dispatch_rows")
    def run(src_hbm, idx_hbm, pad_hbm, out_hbm, idx_v, pad_v, rows_v, sem):
        wid = lax.axis_index("s") * SC_CORES + lax.axis_index("c")
        pltpu.sync_copy(idx_hbm.at[wid], idx_v)
        pltpu.sync_copy(pad_hbm.at[wid], pad_v)

        @pl.loop(0, cpw)
        def _(j):
            pltpu.sync_copy(src_hbm.at[pl.ds((wid * cpw + j) * c, c)], rows_v)
            cps = [pltpu.async_copy(rows_v, out_hbm.at[idx_v.at[j * copies + kk]], sem)
                   for kk in range(copies)]
            for cp in cps:
                cp.wait()

        n_src_chunks = src.shape[0] // c

        @pl.loop(0, ppw)
        def _(j):
            pltpu.sync_copy(src_hbm.at[pl.ds(((wid * ppw + j) % n_src_chunks) * c, c)], rows_v)
            pltpu.async_copy(rows_v, out_hbm.at[pad_v.at[j]], sem).wait()

    return run(src, idx_main, idx_pad)


def _sc_gather_rows(src, idx):
    nw, cpw, c = idx.shape
    width = src.shape[1]

    @functools.partial(
        pl.kernel, mesh=_sc_mesh(),
        out_type=jax.ShapeDtypeStruct((nw * cpw * c, width), src.dtype),
        scratch_types=[pltpu.VMEM((cpw, c), I32), pltpu.VMEM((c, width), src.dtype),
                       pltpu.SemaphoreType.DMA],
        name="collect_rows")
    def run(src_hbm, idx_hbm, out_hbm, idx_v, rows_v, sem):
        wid = lax.axis_index("s") * SC_CORES + lax.axis_index("c")
        pltpu.sync_copy(idx_hbm.at[wid], idx_v)

        @pl.loop(0, cpw)
        def _(j):
            pltpu.async_copy(src_hbm.at[idx_v.at[j]], rows_v, sem).wait()
            pltpu.sync_copy(rows_v, out_hbm.at[pl.ds((wid * cpw + j) * c, c)])

    return run(src, idx)


def _dispatch(h2, dest, pad_rows, n_rows):
    k, t = dest.shape
    cpw = t // (SC_WORKERS * SC_CHUNK)
    idx_main = dest.reshape(k, SC_WORKERS, cpw, SC_CHUNK).transpose(1, 2, 0, 3)
    idx_main = idx_main.reshape(SC_WORKERS, cpw * k, SC_CHUNK)
    idx_pad = pad_rows.reshape(SC_WORKERS, -1, SC_CHUNK)
    return _sc_scatter_rows(h2, idx_main, idx_pad, n_rows)


def _collect(y, dest):
    k, t = dest.shape
    idx = dest.reshape(SC_WORKERS, (k * t) // (SC_WORKERS * SC_CHUNK), SC_CHUNK)
    return _sc_gather_rows(y, idx).reshape(k, t, y.shape[1])


def _moe_kernel(be_ref, x_ref, wgu_ref, bgu_ref, wd_ref, bd_ref, y_ref):
    lo, hi = _unpack_bf16_pairs(x_ref[...])
    x = jnp.concatenate([lo, hi], axis=1).astype(BF16)
    gu = _dot(x, wgu_ref[...]) + bgu_ref[...]
    acts = []
    for b in range(gu.shape[1] // GU_BLOCK):
        glu = jnp.minimum(gu[:, b * GU_BLOCK:b * GU_BLOCK + LANES], SWIGLU_LIMIT)
        lin = jnp.clip(gu[:, b * GU_BLOCK + LANES:(b + 1) * GU_BLOCK], -SWIGLU_LIMIT, SWIGLU_LIMIT)
        acts.append(glu * jax.nn.sigmoid(SWIGLU_ALPHA * glu) * (lin + 1.0))
    act = jnp.concatenate(acts, axis=1).astype(BF16)
    y_ref[...] = _pack_bf16_pairs(_dot(act, wd_ref[...]) + bd_ref[...])


def _moe_ffn(xs, block_e, w, l):
    n_rows, half = xs.shape
    d = 2 * half
    f2 = w["wgu"].shape[3]
    n_blocks = n_rows // MOE_ROWS
    row = lambda i, be: (i, 0)
    exp = lambda i, be: (l, be[i], 0, 0)
    return pl.pallas_call(
        _moe_kernel,
        grid_spec=pltpu.PrefetchScalarGridSpec(
            num_scalar_prefetch=1,
            grid=(n_blocks,),
            in_specs=[pl.BlockSpec((MOE_ROWS, half), row),
                      pl.BlockSpec((None, None, d, f2), exp), pl.BlockSpec((None, None, 1, f2), exp),
                      pl.BlockSpec((None, None, f2 // 2, d), exp), pl.BlockSpec((None, None, 1, d), exp)],
            out_specs=pl.BlockSpec((MOE_ROWS, half), row)),
        out_shape=jax.ShapeDtypeStruct((n_rows, half), I32),
        compiler_params=_cparams(("arbitrary",)),
        name="expert_ffn",
    )(block_e, xs, w["wgu"], w["bgu"], w["wdown"], w["bdown"])


def _combine_kernel(dn_alpha, yg_ref, wts_ref, x1_ref, mod_ref, lng_ref, lnb_ref, x2_ref):
    wts = wts_ref[...]
    lo = hi = None
    for k in range(TOP_K):
        l_k, h_k = _unpack_bf16_pairs(yg_ref[k])
        w_k = wts[:, k:k + 1]
        lo = l_k * w_k if lo is None else lo + l_k * w_k
        hi = h_k * w_k if hi is None else hi + h_k * w_k
    ff = jnp.concatenate([lo, hi], axis=1)
    g2 = mod_ref[5:6, :]
    x2_ref[...] = _layer_norm(dn_alpha * x1_ref[...] + (1.0 + g2) * ff, lng_ref[...], lnb_ref[...])


def _combine(yg, wts_t, x1, mod_l, ln_g, ln_b, g, tm, dn_alpha):
    t, d = x1.shape
    row = lambda i: (i, 0)
    seq = lambda i: (g.seq_of_tile(i, tm), 0, 0)
    return pl.pallas_call(
        functools.partial(_combine_kernel, dn_alpha),
        grid=(t // tm,),
        in_specs=[pl.BlockSpec((TOP_K, tm, d // 2), lambda i: (0, i, 0)),
                  pl.BlockSpec((tm, TOP_K), row),
                  pl.BlockSpec((tm, d), row),
                  pl.BlockSpec((None, mod_l.shape[1], d), seq),
                  _const_spec(ln_g.shape), _const_spec(ln_b.shape)],
        out_specs=pl.BlockSpec((tm, d), row),
        out_shape=jax.ShapeDtypeStruct((t, d), F32),
        compiler_params=_cparams(("parallel",)),
        name="moe_combine",
    )(yg, wts_t, x1, mod_l, ln_g, ln_b)


def _deinterleave_kernel(w_ref, perm_ref, o_ref):
    w = w_ref[...].astype(BF16)
    for b in range(w.shape[1] // GU_BLOCK):
        cols = slice(b * GU_BLOCK, (b + 1) * GU_BLOCK)
        o_ref[:, cols] = _dot(w[:, cols], perm_ref[...]).astype(BF16)


def _deinterleave_gate_up(w_gate_up):
    depth, ne, d, f2 = w_gate_up.shape
    tn = _pick(f2, (1024, 512, GU_BLOCK))
    j = jnp.arange(GU_BLOCK)
    target = (j % 2) * LANES + j // 2
    perm = (target[:, None] == jnp.arange(GU_BLOCK)[None, :]).astype(BF16)
    out = pl.pallas_call(
        _deinterleave_kernel,
        grid=(depth * ne, f2 // tn),
        in_specs=[pl.BlockSpec((None, d, tn), lambda e, j: (e, 0, j)), _const_spec((GU_BLOCK, GU_BLOCK))],
        out_specs=pl.BlockSpec((None, d, tn), lambda e, j: (e, 0, j)),
        out_shape=jax.ShapeDtypeStruct((depth * ne, d, f2), BF16),
        compiler_params=_cparams(("parallel", "parallel")),
        name="gate_up_columns",
    )(w_gate_up.reshape(depth * ne, d, f2), perm)
    return out.reshape(depth, ne, d, f2)


def _deinterleave_bias(b_gate_up):
    depth, ne, f2 = b_gate_up.shape
    b = b_gate_up.reshape(depth, ne, f2 // GU_BLOCK, LANES, 2).transpose(0, 1, 2, 4, 3)
    return b.reshape(depth, ne, 1, f2)


def _rope_tables(seq_len):
    inv = 1.0 / (ROPE_BASE ** (jnp.arange(0, QK_ROPE, 2, dtype=F32) / QK_ROPE))
    ang = jnp.arange(seq_len, dtype=F32)[:, None] * inv[None, :]
    pad = jnp.zeros((seq_len, LANES - QK_ROPE), F32)
    cos2 = jnp.concatenate([jnp.cos(ang), jnp.cos(ang)], axis=1)
    sin2 = jnp.concatenate([jnp.sin(ang), jnp.sin(ang)], axis=1)
    return (jnp.concatenate([cos2, pad], axis=1), jnp.concatenate([sin2, pad], axis=1), cos2.T, sin2.T)


def _rot_pair(w):
    half = w.shape[-1] // 2
    return jnp.concatenate([-w[..., half:], w[..., :half]], axis=-1)


def _pad_lanes(w):
    return jnp.pad(w, [(0, 0)] * (w.ndim - 1) + [(0, LANES - w.shape[-1])])


def _prep_layer(l, p):
    d = p["w_in"].shape[1]
    w_in = p["w_in"][l]
    i1 = Q_RANK
    i2 = i1 + KV_RANK
    i3 = i2 + QK_ROPE
    i4 = i3 + 2 * SGU_WIDTH
    w_kr = w_in[:, i2:i3]
    w = {}
    w["wa"] = jnp.concatenate([w_in[:, :i2], _pad_lanes(w_kr), _pad_lanes(_rot_pair(w_kr))], axis=1).astype(BF16)
    w["wuv"] = w_in[:, i3:i4].astype(BF16)
    w["wg"] = w_in[:, i4:].astype(BF16)
    w["gq"] = p["q_norm_g"][l][None, :]
    w["gkv"] = p["kv_norm_g"][l][None, :]

    w_uq = p["w_uq"][l].reshape(Q_RANK, N_HEADS, QK_NOPE + QK_ROPE)
    w_ukv = p["w_ukv"][l].reshape(KV_RANK, N_HEADS, QK_NOPE + V_DIM)
    uq_nope = w_uq[:, :, :QK_NOPE].transpose(1, 0, 2)
    uk_t = w_ukv[:, :, :QK_NOPE].transpose(1, 2, 0)
    w["wqlt"] = _bmm(uq_nope, uk_t).transpose(0, 2, 1).reshape(N_HEADS * LAT, Q_RANK).astype(BF16)
    uq_rope = w_uq[:, :, QK_NOPE:]
    w["wqrt"] = uq_rope.reshape(Q_RANK, N_HEADS * QK_ROPE).T.astype(BF16)
    w["wqrst"] = _rot_pair(uq_rope).reshape(Q_RANK, N_HEADS * QK_ROPE).T.astype(BF16)
    uv = w_ukv[:, :, QK_NOPE:].transpose(1, 0, 2)
    wo = p["w_o_attn"][l].reshape(N_HEADS, V_DIM, d)
    w["wfold"] = _bmm(uv, wo).reshape(N_HEADS * LAT, d).astype(BF16)

    w["lng"] = p["sgu_ln_g"][l][None, :]
    w["lnb"] = p["sgu_ln_b"][l][None, :]
    gpl = LANES // (SGU_WIDTH // SGU_GROUPS)
    ws = p["w_s"][l].reshape(SGU_GROUPS // gpl, gpl, CHUNK, CHUNK)
    w["wsp"] = ws.transpose(0, 2, 1, 3).reshape(SGU_GROUPS // gpl, CHUNK, gpl * CHUNK).astype(BF16)
    w["bs"] = jnp.repeat(p["b_s"][l].T, SGU_WIDTH // SGU_GROUPS, axis=1)
    w["wosgu"] = p["w_o_sgu"][l].astype(BF16)
    w["wout"] = p["w_out"][l].astype(BF16)
    w["ln1g"] = p["ln1_g"][l][None, :]
    w["ln1b"] = p["ln1_b"][l][None, :]
    w["rwt"] = p["router_w"][l].T
    w["rb"] = p["router_b"][l][:, None]
    w["ln2g"] = p["ln2_g"][l][None, :]
    w["ln2b"] = p["ln2_b"][l][None, :]
    return w


def _route_and_dispatch(h2, idx):
    t = h2.shape[0]
    n_pad = N_EXPERTS * MOE_ROWS
    n_rows = TOP_K * t + n_pad
    dest, counts, padded, pad_start, pad_end = _route(idx, _pick(t, (512, 256, 128)))
    pad_rows = _padding_rows(counts, padded, pad_start, pad_end, n_pad)
    block_start = jnp.arange(n_rows // MOE_ROWS, dtype=I32) * MOE_ROWS
    block_e = jnp.minimum(_count_le(pad_end, block_start), N_EXPERTS - 1)
    return _dispatch(h2, dest, pad_rows, n_rows), block_e, dest


def kernel(x_prompt, x_sample, c_prompt, c_sample, w_mod, b_mod, w_in, q_norm_g, kv_norm_g, w_uq, w_ukv, w_o_attn, sgu_ln_g, sgu_ln_b, w_s, b_s, w_o_sgu, w_out, ln1_g, ln1_b, router_w, router_b, w_gate_up, b_gate_up, w_down, b_down, ln2_g, ln2_b):
    p = dict(w_in=w_in, q_norm_g=q_norm_g, kv_norm_g=kv_norm_g, w_uq=w_uq, w_ukv=w_ukv, w_o_attn=w_o_attn,
             sgu_ln_g=sgu_ln_g, sgu_ln_b=sgu_ln_b, w_s=w_s, b_s=b_s, w_o_sgu=w_o_sgu, w_out=w_out,
             ln1_g=ln1_g, ln1_b=ln1_b, router_w=router_w, router_b=router_b, ln2_g=ln2_g, ln2_b=ln2_b)
    depth = w_mod.shape[0]
    d = x_prompt.shape[-1]
    dn_alpha = (2 * depth) ** 0.25
    groups = [_Seqs(x_prompt.shape[0], x_prompt.shape[1], 0),
              _Seqs(x_sample.shape[0], x_sample.shape[1], x_prompt.shape[0])]
    xs = [x_prompt.reshape(-1, d), x_sample.reshape(-1, d)]
    for g in groups:
        assert g.t % (SC_WORKERS * SC_CHUNK) == 0
    assert (N_EXPERTS * MOE_ROWS) % (SC_WORKERS * SC_CHUNK) == 0

    n_seq = sum(g.b for g in groups)
    ns_pad = -(-n_seq // 8) * 8
    c_all = jnp.pad(jnp.concatenate([c_prompt, c_sample], axis=0), ((0, ns_pad - n_seq), (0, 0)))
    mod = _modulation(c_all, w_mod, b_mod)
    rope = _rope_tables(max(g.s for g in groups))
    moe_w = dict(wgu=_deinterleave_gate_up(w_gate_up), bgu=_deinterleave_bias(b_gate_up),
                 wdown=w_down.astype(BF16), bdown=b_down[:, :, None, :])

    def tiles(g):
        return (_pick(g.s, (512, 256, 128)), _pick(g.s, (1024, 512, 256, 128)), _pick(g.s, (2048, 1024, 512, 256)))

    for l in range(depth):
        w = dict(_prep_layer(l, p), **moe_w)
        mixer_in = [_inproj(x, mod[l], rope, w, g, tiles(g)[0]) for x, g in zip(xs, groups)]
        routed = []
        for x, g, (qt, klat, vt, sgu, sg) in zip(xs, groups, mixer_in):
            tm, tq, tk = tiles(g)
            o = _attention(qt, klat, vt, g.b, g.s, tq, tk)
            x1, h2, idx, wts = _outproj(o, sgu, sg, x, mod[l], w, g, tm, dn_alpha)
            routed.append((x1, wts) + _route_and_dispatch(h2, idx))
        collected = [(x1, wts, _collect(_moe_ffn(rows, block_e, w, l), dest))
                     for x1, wts, rows, block_e, dest in routed]
        xs = [_combine(yg, wts.T, x1, mod[l], w["ln2g"], w["ln2b"], g, tiles(g)[0], dn_alpha)
              for (x1, wts, yg), g in zip(collected, groups)]

    return tuple(x.reshape(g.b, g.s, d) for x, g in zip(xs, groups))
```

```python
import functools
import math

import jax
import jax.numpy as jnp
from jax import lax
from jax.experimental import pallas as pl
from jax.experimental.pallas import tpu as pltpu
from jax.experimental.pallas import tpu_sc as plsc

F32 = jnp.float32
BF16 = jnp.bfloat16
I32 = jnp.int32

N_HEADS = 8
QK_NOPE = 64
QK_ROPE = 32
V_DIM = 64
Q_RANK = 256
KV_RANK = 128
ROPE_BASE = 10000.0
CHUNK = 128
SGU_WIDTH = 512
SGU_GROUPS = 8
N_EXPERTS = 32
TOP_K = 4
SWIGLU_LIMIT = 7.0
SWIGLU_ALPHA = 1.702
LN_EPS = 1e-5
RMS_EPS = 1e-6
ATTN_SCALE = 1.0 / math.sqrt(QK_NOPE + QK_ROPE)
Q_SCALE = ATTN_SCALE * math.log2(math.e)

LANES = 128
LAT = KV_RANK
QK_PAD = 2 * LANES
V_ROWS = V_DIM + 16
KEY_CHUNK = 256
MOE_ROWS = 512
GU_BLOCK = 2 * LANES
VMEM_LIMIT = 56 * 1024 * 1024
SC_CORES = 2
SC_SUBCORES = 16
SC_WORKERS = SC_CORES * SC_SUBCORES
SC_CHUNK = 64


def _cparams(sem, vmem=VMEM_LIMIT):
    return pltpu.CompilerParams(dimension_semantics=sem, vmem_limit_bytes=vmem)


def _pick(n, prefs):
    for p in prefs:
        if n % p == 0:
            return p
    raise ValueError(f"no tile in {prefs} divides {n}")


def _const_spec(shape):
    nd = len(shape)
    return pl.BlockSpec(shape, lambda *_: (0,) * nd)


def _dot(a, b):
    return jnp.dot(a, b, preferred_element_type=F32)


def _layer_norm(y, g, b):
    mu = jnp.mean(y, axis=-1, keepdims=True)
    d = y - mu
    var = jnp.mean(d * d, axis=-1, keepdims=True)
    return d * lax.rsqrt(var + LN_EPS) * g + b


def _rms_norm(y, g):
    return y * lax.rsqrt(jnp.mean(y * y, axis=-1, keepdims=True) + RMS_EPS) * g


HIGH_HALF = -65536


def _pack_bf16_pairs(y):
    w = y.shape[1] // 2
    bits = lax.bitcast_convert_type(y.astype(BF16).astype(F32), I32)
    return (bits[:, w:] & HIGH_HALF) | lax.shift_right_logical(bits[:, :w], 16)


def _unpack_bf16_pairs(p):
    lo = lax.bitcast_convert_type(lax.shift_left(p, 16), F32)
    hi = lax.bitcast_convert_type(p & HIGH_HALF, F32)
    return lo, hi


class _Seqs:
    def __init__(self, b, s, seq0):
        self.b, self.s, self.seq0 = b, s, seq0
        self.t = b * s

    def seq_of_tile(self, i, tm):
        return self.seq0 + (i * tm) // self.s

    def pos_tile(self, i, tm):
        return i % (self.s // tm)


def _mod_kernel(c_ref, w_ref, b_ref, o_ref):
    c = c_ref[...]
    a = c * jax.nn.sigmoid(c)
    o_ref[0] = jnp.dot(a, w_ref[0], preferred_element_type=F32,
                       precision=lax.Precision.HIGHEST) + b_ref[0]


def _modulation(c_all, w_mod, b_mod):
    depth, d, six_d = w_mod.shape
    ns = c_all.shape[0]
    n_col = six_d // d
    out = pl.pallas_call(
        _mod_kernel,
        grid=(depth, n_col),
        in_specs=[
            pl.BlockSpec((ns, d), lambda l, j: (0, 0)),
            pl.BlockSpec((1, d, d), lambda l, j: (l, 0, j)),
            pl.BlockSpec((1, 1, d), lambda l, j: (l, 0, j)),
        ],
        out_specs=pl.BlockSpec((1, ns, d), lambda l, j: (l, 0, j)),
        out_shape=jax.ShapeDtypeStruct((depth, ns, six_d), F32),
        compiler_params=_cparams(("parallel", "parallel")),
        name="modulation",
    )(c_all, w_mod, b_mod.reshape(depth, 1, six_d))
    return out.reshape(depth, ns, n_col, d)


def _bmm_kernel(a_ref, b_ref, o_ref):
    o_ref[0] = jnp.dot(a_ref[0], b_ref[0], preferred_element_type=F32,
                       precision=lax.Precision.HIGHEST)


def _bmm(a, b):
    h, m, k = a.shape
    n = b.shape[2]
    return pl.pallas_call(
        _bmm_kernel,
        grid=(h,),
        in_specs=[pl.BlockSpec((1, m, k), lambda i: (i, 0, 0)),
                  pl.BlockSpec((1, k, n), lambda i: (i, 0, 0))],
        out_specs=pl.BlockSpec((1, m, n), lambda i: (i, 0, 0)),
        out_shape=jax.ShapeDtypeStruct((h, m, n), F32),
        compiler_params=_cparams(("parallel",)),
        name="weight_fold",
    )(a, b)


def _gelu_tanh(x):
    c = math.sqrt(2.0 / math.pi)
    return 0.5 * x * (1.0 + jnp.tanh(c * (x + 0.044715 * (x * x * x))))


def _inproj_kernel(x_ref, mod_ref, cos_ref, sin_ref, cost_ref, sint_ref, wa_ref, wuv_ref, wg_ref, gq_ref,
                   gkv_ref, wqlt_ref, wqrt_ref, wqrst_ref, wvt_ref, lng_ref, lnb_ref, wsp_ref, bs_ref,
                   q_ref, klat_ref, vt_ref, sgu_ref, sg_ref):
    tm = x_ref.shape[0]
    x = x_ref[...]
    sh1 = mod_ref[0:1, :]
    sc1 = mod_ref[1:2, :]
    h = (x * (1.0 + sc1) + sh1).astype(BF16)

    za = _dot(h, wa_ref[...])
    cq = za[:, 0:Q_RANK]
    ckv = za[:, Q_RANK:Q_RANK + KV_RANK]
    kr = za[:, Q_RANK + KV_RANK:Q_RANK + KV_RANK + LANES]
    krs = za[:, Q_RANK + KV_RANK + LANES:Q_RANK + KV_RANK + 2 * LANES]
    cos = cos_ref[...]
    sin = sin_ref[...]
    cqn = _rms_norm(cq, gq_ref[...]).astype(BF16)
    ckvn = _rms_norm(ckv, gkv_ref[...])
    klat_ref[:, 0:LAT] = ckvn.astype(BF16)
    klat_ref[:, LAT:QK_PAD] = (kr * cos + krs * sin).astype(BF16)
    nt = (((1,), (1,)), ((), ()))
    vt = lax.dot_general(wvt_ref[...], ckvn.astype(BF16), nt, preferred_element_type=F32)
    extra = lax.broadcasted_iota(I32, (V_ROWS - V_DIM, tm), 0)
    ones_row = jnp.where(extra == 0, 1.0, 0.0).astype(BF16)
    for hd in range(N_HEADS):
        vt_ref[hd, 0:V_DIM, :] = vt[hd * V_DIM:(hd + 1) * V_DIM, :].astype(BF16)
        vt_ref[hd, V_DIM:V_ROWS, :] = ones_row

    qlt = lax.dot_general(wqlt_ref[...], cqn, nt, preferred_element_type=F32)
    qrt = lax.dot_general(wqrt_ref[...], cqn, nt, preferred_element_type=F32)
    qrst = lax.dot_general(wqrst_ref[...], cqn, nt, preferred_element_type=F32)
    cos_q = cost_ref[...] * Q_SCALE
    sin_q = sint_ref[...] * Q_SCALE
    pad = jnp.zeros((QK_PAD - LAT - QK_ROPE, tm), BF16)
    for hd in range(N_HEADS):
        q_ref[hd, 0:LAT, :] = (qlt[hd * LAT:(hd + 1) * LAT, :] * Q_SCALE).astype(BF16)
        rows = slice(hd * QK_ROPE, (hd + 1) * QK_ROPE)
        q_ref[hd, LAT:LAT + QK_ROPE, :] = (qrt[rows, :] * cos_q + qrst[rows, :] * sin_q).astype(BF16)
        q_ref[hd, LAT + QK_ROPE:QK_PAD, :] = pad

    sg_ref[...] = jax.nn.sigmoid(_dot(h, wg_ref[...])).astype(BF16)

    guv = _gelu_tanh(_dot(h, wuv_ref[...]))
    u = guv[:, :SGU_WIDTH]
    vn = _layer_norm(guv[:, SGU_WIDTH:], lng_ref[...], lnb_ref[...]).astype(BF16)
    lane = lax.broadcasted_iota(I32, (CHUNK, LANES), 1)
    low_half = lane < (LANES // 2)
    zero = jnp.zeros((CHUNK, LANES), BF16)
    for c in range(tm // CHUNK):
        rows = slice(c * CHUNK, (c + 1) * CHUNK)
        vc = vn[rows, :]
        mixed = []
        for j in range(SGU_WIDTH // LANES):
            blk = vc[:, j * LANES:(j + 1) * LANES]
            rhs = jnp.concatenate([jnp.where(low_half, blk, zero), jnp.where(low_half, zero, blk)], axis=0)
            mixed.append(_dot(wsp_ref[j], rhs))
        mixed = jnp.concatenate(mixed, axis=1) + bs_ref[...]
        sgu_ref[rows, :] = (u[rows, :] * mixed).astype(BF16)


def _inproj(x, mod_l, rope, w, g, tm):
    t, d = x.shape
    n = t // tm
    seq = lambda i: (g.seq_of_tile(i, tm), 0, 0)
    pos = lambda i: (g.pos_tile(i, tm), 0)
    pos_t = lambda i: (0, g.pos_tile(i, tm))
    row = lambda i: (i, 0)
    col = lambda i: (0, i)
    cos_t, sin_t, cos_tt, sin_tt = rope
    consts = [w["wa"], w["wuv"], w["wg"], w["gq"], w["gkv"], w["wqlt"], w["wqrt"], w["wqrst"], w["wvt"],
              w["lng"], w["lnb"], w["wsp"], w["bs"]]
    return pl.pallas_call(
        _inproj_kernel,
        grid=(n,),
        in_specs=[pl.BlockSpec((tm, d), row),
                  pl.BlockSpec((None, mod_l.shape[1], d), seq),
                  pl.BlockSpec((tm, LANES), pos),
                  pl.BlockSpec((tm, LANES), pos),
                  pl.BlockSpec((QK_ROPE, tm), pos_t),
                  pl.BlockSpec((QK_ROPE, tm), pos_t)] + [_const_spec(c.shape) for c in consts],
        out_specs=[pl.BlockSpec((N_HEADS, QK_PAD, tm), lambda i: (0, 0, i)),
                   pl.BlockSpec((tm, QK_PAD), row),
                   pl.BlockSpec((N_HEADS, V_ROWS, tm), lambda i: (0, 0, i)),
                   pl.BlockSpec((tm, SGU_WIDTH), row),
                   pl.BlockSpec((tm, 2 * d), row)],
        out_shape=[jax.ShapeDtypeStruct((N_HEADS, QK_PAD, t), BF16),
                   jax.ShapeDtypeStruct((t, QK_PAD), BF16),
                   jax.ShapeDtypeStruct((N_HEADS, V_ROWS, t), BF16),
                   jax.ShapeDtypeStruct((t, SGU_WIDTH), BF16),
                   jax.ShapeDtypeStruct((t, 2 * d), BF16)],
        compiler_params=_cparams(("parallel",)),
        name="token_mixer_in",
    )(x, mod_l, cos_t, sin_t, cos_tt, sin_tt, *consts)


def _attn_kernel(qt_ref, k_ref, vt_ref, o_ref, m_ref, acc_ref, s_ref):
    ki = pl.program_id(2)
    nh = qt_ref.shape[0]

    @pl.when(ki == 0)
    def _():
        m_ref[...] = jnp.full(m_ref.shape, -jnp.inf, F32)
        acc_ref[...] = jnp.zeros(acc_ref.shape, F32)

    k = k_ref[...]
    tk = k.shape[0]
    def scores(hd):
        st = _dot(k, qt_ref[hd])
        s_ref[hd % 2] = st
        return jnp.max(st, axis=0, keepdims=True)

    mx_next = scores(0)
    for hd in range(nh):
        mx = mx_next
        if hd + 1 < nh:
            mx_next = scores(hd + 1)
        m_prev = m_ref[hd:hd + 1, :]
        m_new = jnp.maximum(m_prev, mx)
        alpha = jnp.exp2(m_prev - m_new)
        pv = None
        for kt in range(tk // KEY_CHUNK):
            keys = slice(kt * KEY_CHUNK, (kt + 1) * KEY_CHUNK)
            p = jnp.exp2(s_ref[hd % 2, keys, :] - m_new).astype(BF16)
            part = _dot(vt_ref[hd, :, keys], p)
            pv = part if pv is None else pv + part
        acc_ref[hd] = alpha * acc_ref[hd] + pv
        m_ref[hd:hd + 1, :] = m_new

    @pl.when(ki == pl.num_programs(2) - 1)
    def _():
        for hd in range(0, nh, 2):
            pair = [acc_ref[h][0:V_DIM, :] / acc_ref[h][V_DIM:V_DIM + 1, :] for h in (hd, hd + 1)]
            o_ref[:, hd * V_DIM:(hd + 2) * V_DIM] = jnp.concatenate(pair, axis=0).T.astype(BF16)


def _attention(qt, klat, vt, nb, s, tq, tk):
    nq, nk = s // tq, s // tk
    return pl.pallas_call(
        _attn_kernel,
        grid=(nb, nq, nk),
        in_specs=[pl.BlockSpec((N_HEADS, QK_PAD, tq), lambda b, i, j: (0, 0, b * nq + i)),
                  pl.BlockSpec((tk, QK_PAD), lambda b, i, j: (b * nk + j, 0)),
                  pl.BlockSpec((N_HEADS, V_ROWS, tk), lambda b, i, j: (0, 0, b * nk + j))],
        out_specs=pl.BlockSpec((tq, N_HEADS * V_DIM), lambda b, i, j: (b * nq + i, 0)),
        out_shape=jax.ShapeDtypeStruct((nb * s, N_HEADS * V_DIM), BF16),
        scratch_shapes=[pltpu.VMEM((N_HEADS, tq), F32), pltpu.VMEM((N_HEADS, V_ROWS, tq), F32),
                        pltpu.VMEM((2, tk, tq), F32)],
        compiler_params=_cparams(("parallel", "parallel", "arbitrary")),
        name="latent_attention",
    )(qt, klat, vt)


def _outproj_kernel(dn_alpha, o_ref, sgu_ref, sg_ref, x_ref, mod_ref, woattn_ref, wosgu_ref, wout_ref,
                    lng_ref, lnb_ref, rwt_ref, rb_ref, x1_ref, h2_ref, idx_ref, wts_ref):
    d = x_ref.shape[1]
    o_attn = _dot(o_ref[...], woattn_ref[...])
    o_sgu = _dot(sgu_ref[...], wosgu_ref[...])
    sg = sg_ref[...].astype(F32)
    merged = (sg[:, :d] * o_attn + sg[:, d:] * o_sgu).astype(BF16)
    mix = _dot(merged, wout_ref[...])
    g1 = mod_ref[2:3, :]
    sh2 = mod_ref[3:4, :]
    sc2 = mod_ref[4:5, :]
    x1 = _layer_norm(dn_alpha * x_ref[...] + (1.0 + g1) * mix, lng_ref[...], lnb_ref[...])
    x1_ref[...] = x1
    h2 = x1 * (1.0 + sc2) + sh2
    h2_ref[...] = _pack_bf16_pairs(h2)

    logits = lax.dot_general(rwt_ref[...], h2, (((1,), (1,)), ((), ())), preferred_element_type=F32,
                             precision=lax.Precision.HIGHEST) + rb_ref[...]
    e_iota = lax.broadcasted_iota(I32, logits.shape, 0)
    vals, idxs = [], []
    for _ in range(TOP_K):
        mx = jnp.max(logits, axis=0, keepdims=True)
        ix = jnp.min(jnp.where(logits == mx, e_iota, N_EXPERTS), axis=0, keepdims=True)
        vals.append(mx)
        idxs.append(ix)
        logits = jnp.where(e_iota == ix, -jnp.inf, logits)
    v = jnp.concatenate(vals, axis=0)
    w = jnp.exp(v - v[0:1, :])
    wts_ref[...] = w / jnp.sum(w, axis=0, keepdims=True)
    idx_ref[...] = jnp.concatenate(idxs, axis=0)


def _outproj(o, sgu, sg, x, mod_l, w, g, tm, dn_alpha):
    t, d = x.shape
    n = t // tm
    seq = lambda i: (g.seq_of_tile(i, tm), 0, 0)
    row = lambda i: (i, 0)
    col = lambda i: (0, i)
    consts = [w["woattn"], w["wosgu"], w["wout"], w["ln1g"], w["ln1b"], w["rwt"], w["rb"]]
    return pl.pallas_call(
        functools.partial(_outproj_kernel, dn_alpha),
        grid=(n,),
        in_specs=[pl.BlockSpec((tm, N_HEADS * V_DIM), row),
                  pl.BlockSpec((tm, SGU_WIDTH), row),
                  pl.BlockSpec((tm, 2 * d), row),
                  pl.BlockSpec((tm, d), row),
                  pl.BlockSpec((None, mod_l.shape[1], d), seq)] + [_const_spec(c.shape) for c in consts],
        out_specs=[pl.BlockSpec((tm, d), row),
                   pl.BlockSpec((tm, d // 2), row),
                   pl.BlockSpec((TOP_K, tm), col),
                   pl.BlockSpec((TOP_K, tm), col)],
        out_shape=[jax.ShapeDtypeStruct((t, d), F32),
                   jax.ShapeDtypeStruct((t, d // 2), I32),
                   jax.ShapeDtypeStruct((TOP_K, t), I32),
                   jax.ShapeDtypeStruct((TOP_K, t), F32)],
        compiler_params=_cparams(("parallel",)),
        name="token_mixer_out",
    )(o, sgu, sg, x, mod_l, *consts)


def _one_hots(idx, tm):
    e_iota = lax.broadcasted_iota(I32, (N_EXPERTS, tm), 0)
    return [(e_iota == idx[k:k + 1, :]).astype(F32) for k in range(TOP_K)]


def _count_kernel(idx_ref, cnt_ref):
    @pl.when(pl.program_id(0) == 0)
    def _():
        cnt_ref[...] = jnp.zeros(cnt_ref.shape, F32)

    oh = sum(_one_hots(idx_ref[...], idx_ref.shape[1]))
    cnt_ref[...] += jnp.sum(oh, axis=1, keepdims=True)


def _dest_kernel(idx_ref, start_ref, dest_ref, carry_ref):
    tm = idx_ref.shape[1]

    @pl.when(pl.program_id(0) == 0)
    def _():
        carry_ref[...] = start_ref[...]

    ohs = _one_hots(idx_ref[...], tm)
    oh = sum(ohs)
    r = lax.broadcasted_iota(I32, (tm, tm), 0)
    c = lax.broadcasted_iota(I32, (tm, tm), 1)
    before = (r < c).astype(BF16)
    rank = _dot(oh.astype(BF16), before) + carry_ref[:, 0:1]
    dest_ref[...] = jnp.concatenate(
        [jnp.sum(o * rank, axis=0, keepdims=True) for o in ohs], axis=0).astype(I32)
    carry_ref[...] += jnp.sum(oh, axis=1, keepdims=True)


def _route(idx, tm):
    k, t = idx.shape
    n = t // tm
    col = lambda i: (0, i)
    cnt = pl.pallas_call(
        _count_kernel,
        grid=(n,),
        in_specs=[pl.BlockSpec((k, tm), col)],
        out_specs=_const_spec((N_EXPERTS, LANES)),
        out_shape=jax.ShapeDtypeStruct((N_EXPERTS, LANES), F32),
        compiler_params=_cparams(("arbitrary",)),
        name="expert_counts",
    )(idx)
    counts = cnt[:, 0].astype(I32)
    padded = (counts + MOE_ROWS - 1) // MOE_ROWS * MOE_ROWS
    pad_end = jnp.cumsum(padded)
    pad_start = pad_end - padded
    start = jnp.broadcast_to(pad_start.astype(F32)[:, None], (N_EXPERTS, LANES))
    dest = pl.pallas_call(
        _dest_kernel,
        grid=(n,),
        in_specs=[pl.BlockSpec((k, tm), col), _const_spec((N_EXPERTS, LANES))],
        out_specs=pl.BlockSpec((k, tm), col),
        out_shape=jax.ShapeDtypeStruct((k, t), I32),
        scratch_shapes=[pltpu.VMEM((N_EXPERTS, LANES), F32)],
        compiler_params=_cparams(("arbitrary",)),
        name="expert_dest_rows",
    )(idx, start)
    return dest, counts, padded, pad_start, pad_end


def _count_le(sorted_vals, v):
    return jnp.sum((sorted_vals[None, :] <= v[:, None]).astype(I32), axis=1)


def _padding_rows(counts, padded, pad_start, pad_end, n_pad):
    per = padded - counts
    cum = jnp.cumsum(per)
    j = jnp.arange(n_pad, dtype=I32)
    e = _count_le(cum, j)
    ec = jnp.minimum(e, N_EXPERTS - 1)
    inside = pad_start[ec] + counts[ec] + (j - (cum[ec] - per[ec]))
    tail = pad_end[-1] + (j - cum[-1])
    return jnp.where(e < N_EXPERTS, inside, tail)


def _sc_mesh():
    return plsc.VectorSubcoreMesh(core_axis_name="c", subcore_axis_name="s")


def _sc_scatter_rows(src, idx_main, idx_pad, n_out):
    nw, cpk, c = idx_main.shape
    width = src.shape[1]
    cpw = src.shape[0] // (nw * c)
    copies = cpk // cpw
    ppw = idx_pad.shape[1]

    @functools.partial(
        pl.kernel, mesh=_sc_mesh(),
        out_type=jax.ShapeDtypeStruct((n_out, width), src.dtype),
        scratch_types=[pltpu.VMEM((cpk, c), I32), pltpu.VMEM((ppw, c), I32),
                       pltpu.VMEM((c, width), src.dtype), pltpu.SemaphoreType.DMA],
        name="dispatch_rows")
    def run(src_hbm, idx_hbm, pad_hbm, out_hbm, idx_v, pad_v, rows_v, sem):
        wid = lax.axis_index("s") * SC_CORES + lax.axis_index("c")
        pltpu.sync_copy(idx_hbm.at[wid], idx_v)
        pltpu.sync_copy(pad_hbm.at[wid], pad_v)

        @pl.loop(0, cpw)
        def _(j):
            pltpu.sync_copy(src_hbm.at[pl.ds((wid * cpw + j) * c, c)], rows_v)
            cps = [pltpu.async_copy(rows_v, out_hbm.at[idx_v.at[j * copies + kk]], sem)
                   for kk in range(copies)]
            for cp in cps:
                cp.wait()

        n_src_chunks = src.shape[0] // c

        @pl.loop(0, ppw)
        def _(j):
            pltpu.sync_copy(src_hbm.at[pl.ds(((wid * ppw + j) % n_src_chunks) * c, c)], rows_v)
            pltpu.async_copy(rows_v, out_hbm.at[pad_v.at[j]], sem).wait()

    return run(src, idx_main, idx_pad)


def _sc_gather_rows(src, idx):
    nw, cpw, c = idx.shape
    width = src.shape[1]

    @functools.partial(
        pl.kernel, mesh=_sc_mesh(),
        out_type=jax.ShapeDtypeStruct((nw * cpw * c, width), src.dtype),
        scratch_types=[pltpu.VMEM((cpw, c), I32), pltpu.VMEM((c, width), src.dtype),
                       pltpu.SemaphoreType.DMA],
        name="collect_rows")
    def run(src_hbm, idx_hbm, out_hbm, idx_v, rows_v, sem):
        wid = lax.axis_index("s") * SC_CORES + lax.axis_index("c")
        pltpu.sync_copy(idx_hbm.at[wid], idx_v)

        @pl.loop(0, cpw)
        def _(j):
            pltpu.async_copy(src_hbm.at[idx_v.at[j]], rows_v, sem).wait()
            pltpu.sync_copy(rows_v, out_hbm.at[pl.ds((wid * cpw + j) * c, c)])

    return run(src, idx)


def _dispatch(h2, dest, pad_rows, n_rows):
    k, t = dest.shape
    cpw = t // (SC_WORKERS * SC_CHUNK)
    idx_main = dest.reshape(k, SC_WORKERS, cpw, SC_CHUNK).transpose(1, 2, 0, 3)
    idx_main = idx_main.reshape(SC_WORKERS, cpw * k, SC_CHUNK)
    idx_pad = pad_rows.reshape(SC_WORKERS, -1, SC_CHUNK)
    return _sc_scatter_rows(h2, idx_main, idx_pad, n_rows)


def _collect(y, dest):
    k, t = dest.shape
    idx = dest.reshape(SC_WORKERS, (k * t) // (SC_WORKERS * SC_CHUNK), SC_CHUNK)
    return _sc_gather_rows(y, idx).reshape(k, t, y.shape[1])


def _moe_kernel(be_ref, x_ref, wgu_ref, bgu_ref, wd_ref, bd_ref, y_ref):
    in_use = pl.program_id(0) < be_ref[pl.num_programs(0)]

    @pl.when(in_use)
    def _():
        lo, hi = _unpack_bf16_pairs(x_ref[...])
        x = jnp.concatenate([lo, hi], axis=1).astype(BF16)
        gu = _dot(x, wgu_ref[...]) + bgu_ref[...]
        acts = []
        for b in range(gu.shape[1] // GU_BLOCK):
            glu = jnp.minimum(gu[:, b * GU_BLOCK:b * GU_BLOCK + LANES], SWIGLU_LIMIT)
            lin = jnp.clip(gu[:, b * GU_BLOCK + LANES:(b + 1) * GU_BLOCK], -SWIGLU_LIMIT, SWIGLU_LIMIT)
            acts.append(glu * jax.nn.sigmoid(SWIGLU_ALPHA * glu) * (lin + 1.0))
        act = jnp.concatenate(acts, axis=1).astype(BF16)
        y_ref[...] = _pack_bf16_pairs(_dot(act, wd_ref[...]) + bd_ref[...])

    @pl.when(jnp.logical_not(in_use))
    def _():
        y_ref[...] = jnp.zeros(y_ref.shape, y_ref.dtype)


def _moe_ffn(xs, block_e, w, l):
    n_rows, half = xs.shape
    d = 2 * half
    f2 = w["wgu"].shape[3]
    n_blocks = n_rows // MOE_ROWS
    row = lambda i, be: (i, 0)
    exp = lambda i, be: (l, be[i], 0, 0)
    return pl.pallas_call(
        _moe_kernel,
        grid_spec=pltpu.PrefetchScalarGridSpec(
            num_scalar_prefetch=1,
            grid=(n_blocks,),
            in_specs=[pl.BlockSpec((MOE_ROWS, half), row),
                      pl.BlockSpec((None, None, d, f2), exp), pl.BlockSpec((None, None, 1, f2), exp),
                      pl.BlockSpec((None, None, f2 // 2, d), exp), pl.BlockSpec((None, None, 1, d), exp)],
            out_specs=pl.BlockSpec((MOE_ROWS, half), row)),
        out_shape=jax.ShapeDtypeStruct((n_rows, half), I32),
        compiler_params=_cparams(("arbitrary",)),
        name="expert_ffn",
    )(block_e, xs, w["wgu"], w["bgu"], w["wdown"], w["bdown"])


def _combine_kernel(dn_alpha, yg_ref, wts_ref, x1_ref, mod_ref, lng_ref, lnb_ref, x2_ref):
    wts = wts_ref[...]
    lo = hi = None
    for k in range(TOP_K):
        l_k, h_k = _unpack_bf16_pairs(yg_ref[k])
        w_k = wts[:, k:k + 1]
        lo = l_k * w_k if lo is None else lo + l_k * w_k
        hi = h_k * w_k if hi is None else hi + h_k * w_k
    ff = jnp.concatenate([lo, hi], axis=1)
    g2 = mod_ref[5:6, :]
    x2_ref[...] = _layer_norm(dn_alpha * x1_ref[...] + (1.0 + g2) * ff, lng_ref[...], lnb_ref[...])


def _combine(yg, wts_t, x1, mod_l, ln_g, ln_b, g, tm, dn_alpha):
    t, d = x1.shape
    row = lambda i: (i, 0)
    seq = lambda i: (g.seq_of_tile(i, tm), 0, 0)
    return pl.pallas_call(
        functools.partial(_combine_kernel, dn_alpha),
        grid=(t // tm,),
        in_specs=[pl.BlockSpec((TOP_K, tm, d // 2), lambda i: (0, i, 0)),
                  pl.BlockSpec((tm, TOP_K), row),
                  pl.BlockSpec((tm, d), row),
                  pl.BlockSpec((None, mod_l.shape[1], d), seq),
                  _const_spec(ln_g.shape), _const_spec(ln_b.shape)],
        out_specs=pl.BlockSpec((tm, d), row),
        out_shape=jax.ShapeDtypeStruct((t, d), F32),
        compiler_params=_cparams(("parallel",)),
        name="moe_combine",
    )(yg, wts_t, x1, mod_l, ln_g, ln_b)


def _deinterleave_kernel(w_ref, perm_ref, o_ref):
    w = w_ref[...].astype(BF16)
    for b in range(w.shape[1] // GU_BLOCK):
        cols = slice(b * GU_BLOCK, (b + 1) * GU_BLOCK)
        o_ref[:, cols] = _dot(w[:, cols], perm_ref[...]).astype(BF16)


def _deinterleave_gate_up(w_gate_up):
    depth, ne, d, f2 = w_gate_up.shape
    tn = _pick(f2, (1024, 512, GU_BLOCK))
    j = jnp.arange(GU_BLOCK)
    target = (j % 2) * LANES + j // 2
    perm = (target[:, None] == jnp.arange(GU_BLOCK)[None, :]).astype(BF16)
    out = pl.pallas_call(
        _deinterleave_kernel,
        grid=(depth * ne, f2 // tn),
        in_specs=[pl.BlockSpec((None, d, tn), lambda e, j: (e, 0, j)), _const_spec((GU_BLOCK, GU_BLOCK))],
        out_specs=pl.BlockSpec((None, d, tn), lambda e, j: (e, 0, j)),
        out_shape=jax.ShapeDtypeStruct((depth * ne, d, f2), BF16),
        compiler_params=_cparams(("parallel", "parallel")),
        name="gate_up_columns",
    )(w_gate_up.reshape(depth * ne, d, f2), perm)
    return out.reshape(depth, ne, d, f2)


def _deinterleave_bias(b_gate_up):
    depth, ne, f2 = b_gate_up.shape
    b = b_gate_up.reshape(depth, ne, f2 // GU_BLOCK, LANES, 2).transpose(0, 1, 2, 4, 3)
    return b.reshape(depth, ne, 1, f2)


def _rope_tables(seq_len):
    inv = 1.0 / (ROPE_BASE ** (jnp.arange(0, QK_ROPE, 2, dtype=F32) / QK_ROPE))
    ang = jnp.arange(seq_len, dtype=F32)[:, None] * inv[None, :]
    pad = jnp.zeros((seq_len, LANES - QK_ROPE), F32)
    cos2 = jnp.concatenate([jnp.cos(ang), jnp.cos(ang)], axis=1)
    sin2 = jnp.concatenate([jnp.sin(ang), jnp.sin(ang)], axis=1)
    return (jnp.concatenate([cos2, pad], axis=1), jnp.concatenate([sin2, pad], axis=1), cos2.T, sin2.T)


def _rot_pair(w):
    half = w.shape[-1] // 2
    return jnp.concatenate([-w[..., half:], w[..., :half]], axis=-1)


def _pad_lanes(w):
    return jnp.pad(w, [(0, 0)] * (w.ndim - 1) + [(0, LANES - w.shape[-1])])


def _prep_layer(l, p):
    d = p["w_in"].shape[1]
    w_in = p["w_in"][l]
    i1 = Q_RANK
    i2 = i1 + KV_RANK
    i3 = i2 + QK_ROPE
    i4 = i3 + 2 * SGU_WIDTH
    w_kr = w_in[:, i2:i3]
    w = {}
    w["wa"] = jnp.concatenate([w_in[:, :i2], _pad_lanes(w_kr), _pad_lanes(_rot_pair(w_kr))], axis=1).astype(BF16)
    w["wuv"] = w_in[:, i3:i4].astype(BF16)
    w["wg"] = w_in[:, i4:].astype(BF16)
    w["gq"] = p["q_norm_g"][l][None, :]
    w["gkv"] = p["kv_norm_g"][l][None, :]

    w_uq = p["w_uq"][l].reshape(Q_RANK, N_HEADS, QK_NOPE + QK_ROPE)
    w_ukv = p["w_ukv"][l].reshape(KV_RANK, N_HEADS, QK_NOPE + V_DIM)
    uq_nope = w_uq[:, :, :QK_NOPE].transpose(1, 0, 2)
    uk_t = w_ukv[:, :, :QK_NOPE].transpose(1, 2, 0)
    w["wqlt"] = _bmm(uq_nope, uk_t).transpose(0, 2, 1).reshape(N_HEADS * LAT, Q_RANK).astype(BF16)
    uq_rope = w_uq[:, :, QK_NOPE:]
    w["wqrt"] = uq_rope.reshape(Q_RANK, N_HEADS * QK_ROPE).T.astype(BF16)
    w["wqrst"] = _rot_pair(uq_rope).reshape(Q_RANK, N_HEADS * QK_ROPE).T.astype(BF16)
    w["wvt"] = w_ukv[:, :, QK_NOPE:].reshape(KV_RANK, N_HEADS * V_DIM).T.astype(BF16)
    w["woattn"] = p["w_o_attn"][l].astype(BF16)

    w["lng"] = p["sgu_ln_g"][l][None, :]
    w["lnb"] = p["sgu_ln_b"][l][None, :]
    gpl = LANES // (SGU_WIDTH // SGU_GROUPS)
    ws = p["w_s"][l].reshape(SGU_GROUPS // gpl, gpl, CHUNK, CHUNK)
    w["wsp"] = ws.transpose(0, 2, 1, 3).reshape(SGU_GROUPS // gpl, CHUNK, gpl * CHUNK).astype(BF16)
    w["bs"] = jnp.repeat(p["b_s"][l].T, SGU_WIDTH // SGU_GROUPS, axis=1)
    w["wosgu"] = p["w_o_sgu"][l].astype(BF16)
    w["wout"] = p["w_out"][l].astype(BF16)
    w["ln1g"] = p["ln1_g"][l][None, :]
    w["ln1b"] = p["ln1_b"][l][None, :]
    w["rwt"] = p["router_w"][l].T
    w["rb"] = p["router_b"][l][:, None]
    w["ln2g"] = p["ln2_g"][l][None, :]
    w["ln2b"] = p["ln2_b"][l][None, :]
    return w


def _route_and_dispatch(h2, idx):
    t = h2.shape[0]
    n_pad = N_EXPERTS * MOE_ROWS
    n_rows = TOP_K * t + n_pad
    dest, counts, padded, pad_start, pad_end = _route(idx, _pick(t, (512, 256, 128)))
    pad_rows = _padding_rows(counts, padded, pad_start, pad_end, n_pad)
    block_start = jnp.arange(n_rows // MOE_ROWS, dtype=I32) * MOE_ROWS
    block_e = jnp.minimum(_count_le(pad_end, block_start), N_EXPERTS - 1)
    blocks_in_use = pad_end[-1:] // MOE_ROWS
    return _dispatch(h2, dest, pad_rows, n_rows), jnp.concatenate([block_e, blocks_in_use]), dest


def kernel(x_prompt, x_sample, c_prompt, c_sample, w_mod, b_mod, w_in, q_norm_g, kv_norm_g, w_uq, w_ukv, w_o_attn, sgu_ln_g, sgu_ln_b, w_s, b_s, w_o_sgu, w_out, ln1_g, ln1_b, router_w, router_b, w_gate_up, b_gate_up, w_down, b_down, ln2_g, ln2_b):
    p = dict(w_in=w_in, q_norm_g=q_norm_g, kv_norm_g=kv_norm_g, w_uq=w_uq, w_ukv=w_ukv, w_o_attn=w_o_attn,
             sgu_ln_g=sgu_ln_g, sgu_ln_b=sgu_ln_b, w_s=w_s, b_s=b_s, w_o_sgu=w_o_sgu, w_out=w_out,
             ln1_g=ln1_g, ln1_b=ln1_b, router_w=router_w, router_b=router_b, ln2_g=ln2_g, ln2_b=ln2_b)
    depth = w_mod.shape[0]
    d = x_prompt.shape[-1]
    dn_alpha = (2 * depth) ** 0.25
    groups = [_Seqs(x_prompt.shape[0], x_prompt.shape[1], 0),
              _Seqs(x_sample.shape[0], x_sample.shape[1], x_prompt.shape[0])]
    xs = [x_prompt.reshape(-1, d), x_sample.reshape(-1, d)]
    for g in groups:
        assert g.t % (SC_WORKERS * SC_CHUNK) == 0
    assert (N_EXPERTS * MOE_ROWS) % (SC_WORKERS * SC_CHUNK) == 0

    n_seq = sum(g.b for g in groups)
    ns_pad = -(-n_seq // 8) * 8
    c_all = jnp.pad(jnp.concatenate([c_prompt, c_sample], axis=0), ((0, ns_pad - n_seq), (0, 0)))
    mod = _modulation(c_all, w_mod, b_mod)
    rope = _rope_tables(max(g.s for g in groups))
    moe_w = dict(wgu=_deinterleave_gate_up(w_gate_up), bgu=_deinterleave_bias(b_gate_up),
                 wdown=w_down.astype(BF16), bdown=b_down[:, :, None, :])

    def tiles(g):
        return (_pick(g.s, (512, 256, 128)), _pick(g.s, (1024, 512, 256, 128)), _pick(g.s, (2048, 1024, 512, 256)))

    for l in range(depth):
        w = dict(_prep_layer(l, p), **moe_w)
        mixer_in = [_inproj(x, mod[l], rope, w, g, tiles(g)[0]) for x, g in zip(xs, groups)]
        routed = []
        for x, g, (qt, klat, vt, sgu, sg) in zip(xs, groups, mixer_in):
            tm, tq, tk = tiles(g)
            o = _attention(qt, klat, vt, g.b, g.s, tq, tk)
            x1, h2, idx, wts = _outproj(o, sgu, sg, x, mod[l], w, g, tm, dn_alpha)
            routed.append((x1, wts) + _route_and_dispatch(h2, idx))
        collected = [(x1, wts, _collect(_moe_ffn(rows, block_e, w, l), dest))
                     for x1, wts, rows, block_e, dest in routed]
        xs = [_combine(yg, wts.T, x1, mod[l], w["ln2g"], w["ln2b"], g, tiles(g)[0], dn_alpha)
              for (x1, wts, yg), g in zip(collected, groups)]

    return tuple(x.reshape(g.b, g.s, d) for x, g in zip(xs, groups))
```

```python
import functools
import math

import jax
import jax.numpy as jnp
from jax import lax
from jax.experimental import pallas as pl
from jax.experimental.pallas import tpu as pltpu
from jax.experimental.pallas import tpu_sc as plsc

F32 = jnp.float32
BF16 = jnp.bfloat16
I32 = jnp.int32

N_HEADS = 8
QK_NOPE = 64
QK_ROPE = 32
V_DIM = 64
Q_RANK = 256
KV_RANK = 128
ROPE_BASE = 10000.0
CHUNK = 128
SGU_WIDTH = 512
SGU_GROUPS = 8
N_EXPERTS = 32
TOP_K = 4
SWIGLU_LIMIT = 7.0
SWIGLU_ALPHA = 1.702
LN_EPS = 1e-5
RMS_EPS = 1e-6
ATTN_SCALE = 1.0 / math.sqrt(QK_NOPE + QK_ROPE)
Q_SCALE = ATTN_SCALE * math.log2(math.e)

LANES = 128
LAT = KV_RANK
QK_PAD = 2 * LANES
V_ROWS = LAT + 16
KEY_CHUNK = 256
MOE_ROWS = 512
GU_BLOCK = 2 * LANES
VMEM_LIMIT = 56 * 1024 * 1024
SC_CORES = 2
SC_SUBCORES = 16
SC_WORKERS = SC_CORES * SC_SUBCORES
SC_CHUNK = 64


def _cparams(sem, vmem=VMEM_LIMIT):
    return pltpu.CompilerParams(dimension_semantics=sem, vmem_limit_bytes=vmem)


def _pick(n, prefs):
    for p in prefs:
        if n % p == 0:
            return p
    raise ValueError(f"no tile in {prefs} divides {n}")


def _const_spec(shape):
    nd = len(shape)
    return pl.BlockSpec(shape, lambda *_: (0,) * nd)


def _dot(a, b):
    return jnp.dot(a, b, preferred_element_type=F32)


def _layer_norm(y, g, b):
    mu = jnp.mean(y, axis=-1, keepdims=True)
    d = y - mu
    var = jnp.mean(d * d, axis=-1, keepdims=True)
    return d * lax.rsqrt(var + LN_EPS) * g + b


def _rms_norm(y, g):
    return y * lax.rsqrt(jnp.mean(y * y, axis=-1, keepdims=True) + RMS_EPS) * g


HIGH_HALF = -65536


def _pack_bf16_pairs(y):
    w = y.shape[1] // 2
    bits = lax.bitcast_convert_type(y.astype(BF16).astype(F32), I32)
    return (bits[:, w:] & HIGH_HALF) | lax.shift_right_logical(bits[:, :w], 16)


def _unpack_bf16_pairs(p):
    lo = lax.bitcast_convert_type(lax.shift_left(p, 16), F32)
    hi = lax.bitcast_convert_type(p & HIGH_HALF, F32)
    return lo, hi


class _Seqs:
    def __init__(self, b, s, seq0):
        self.b, self.s, self.seq0 = b, s, seq0
        self.t = b * s

    def seq_of_tile(self, i, tm):
        return self.seq0 + (i * tm) // self.s

    def pos_tile(self, i, tm):
        return i % (self.s // tm)


def _mod_kernel(c_ref, w_ref, b_ref, o_ref):
    c = c_ref[...]
    a = c * jax.nn.sigmoid(c)
    o_ref[0] = jnp.dot(a, w_ref[0], preferred_element_type=F32,
                       precision=lax.Precision.HIGHEST) + b_ref[0]


def _modulation(c_all, w_mod, b_mod):
    depth, d, six_d = w_mod.shape
    ns = c_all.shape[0]
    n_col = six_d // d
    out = pl.pallas_call(
        _mod_kernel,
        grid=(depth, n_col),
        in_specs=[
            pl.BlockSpec((ns, d), lambda l, j: (0, 0)),
            pl.BlockSpec((1, d, d), lambda l, j: (l, 0, j)),
            pl.BlockSpec((1, 1, d), lambda l, j: (l, 0, j)),
        ],
        out_specs=pl.BlockSpec((1, ns, d), lambda l, j: (l, 0, j)),
        out_shape=jax.ShapeDtypeStruct((depth, ns, six_d), F32),
        compiler_params=_cparams(("parallel", "parallel")),
        name="modulation",
    )(c_all, w_mod, b_mod.reshape(depth, 1, six_d))
    return out.reshape(depth, ns, n_col, d)


def _bmm_kernel(a_ref, b_ref, o_ref):
    o_ref[0] = jnp.dot(a_ref[0], b_ref[0], preferred_element_type=F32,
                       precision=lax.Precision.HIGHEST)


def _bmm(a, b):
    h, m, k = a.shape
    n = b.shape[2]
    return pl.pallas_call(
        _bmm_kernel,
        grid=(h,),
        in_specs=[pl.BlockSpec((1, m, k), lambda i: (i, 0, 0)),
                  pl.BlockSpec((1, k, n), lambda i: (i, 0, 0))],
        out_specs=pl.BlockSpec((1, m, n), lambda i: (i, 0, 0)),
        out_shape=jax.ShapeDtypeStruct((h, m, n), F32),
        compiler_params=_cparams(("parallel",)),
        name="weight_fold",
    )(a, b)


def _gelu_tanh(x):
    c = math.sqrt(2.0 / math.pi)
    return 0.5 * x * (1.0 + jnp.tanh(c * (x + 0.044715 * (x * x * x))))


def _inproj_kernel(x_ref, mod_ref, cos_ref, sin_ref, cost_ref, sint_ref, wa_ref, wuv_ref, wg_ref, gq_ref,
                   gkv_ref, wqlt_ref, wqrt_ref, wqrst_ref, lng_ref, lnb_ref, wsp_ref, bs_ref,
                   q_ref, klat_ref, vt_ref, sgu_ref, sg_ref):
    tm = x_ref.shape[0]
    x = x_ref[...]
    sh1 = mod_ref[0:1, :]
    sc1 = mod_ref[1:2, :]
    h = (x * (1.0 + sc1) + sh1).astype(BF16)

    za = _dot(h, wa_ref[...])
    cq = za[:, 0:Q_RANK]
    ckv = za[:, Q_RANK:Q_RANK + KV_RANK]
    kr = za[:, Q_RANK + KV_RANK:Q_RANK + KV_RANK + LANES]
    krs = za[:, Q_RANK + KV_RANK + LANES:Q_RANK + KV_RANK + 2 * LANES]
    cos = cos_ref[...]
    sin = sin_ref[...]
    cqn = _rms_norm(cq, gq_ref[...]).astype(BF16)
    ckvn = _rms_norm(ckv, gkv_ref[...])
    klat_ref[:, 0:LAT] = ckvn.astype(BF16)
    klat_ref[:, LAT:QK_PAD] = (kr * cos + krs * sin).astype(BF16)
    vt_ref[0:LAT, :] = ckvn.T.astype(BF16)
    extra = lax.broadcasted_iota(I32, (V_ROWS - LAT, tm), 0)
    vt_ref[LAT:V_ROWS, :] = jnp.where(extra == 0, 1.0, 0.0).astype(BF16)

    nt = (((1,), (1,)), ((), ()))
    qlt = lax.dot_general(wqlt_ref[...], cqn, nt, preferred_element_type=F32)
    qrt = lax.dot_general(wqrt_ref[...], cqn, nt, preferred_element_type=F32)
    qrst = lax.dot_general(wqrst_ref[...], cqn, nt, preferred_element_type=F32)
    cos_q = cost_ref[...] * Q_SCALE
    sin_q = sint_ref[...] * Q_SCALE
    pad = jnp.zeros((QK_PAD - LAT - QK_ROPE, tm), BF16)
    for hd in range(N_HEADS):
        q_ref[hd, 0:LAT, :] = (qlt[hd * LAT:(hd + 1) * LAT, :] * Q_SCALE).astype(BF16)
        rows = slice(hd * QK_ROPE, (hd + 1) * QK_ROPE)
        q_ref[hd, LAT:LAT + QK_ROPE, :] = (qrt[rows, :] * cos_q + qrst[rows, :] * sin_q).astype(BF16)
        q_ref[hd, LAT + QK_ROPE:QK_PAD, :] = pad

    sg_ref[...] = jax.nn.sigmoid(_dot(h, wg_ref[...])).astype(BF16)

    guv = _gelu_tanh(_dot(h, wuv_ref[...]))
    u = guv[:, :SGU_WIDTH]
    vn = _layer_norm(guv[:, SGU_WIDTH:], lng_ref[...], lnb_ref[...]).astype(BF16)
    lane = lax.broadcasted_iota(I32, (CHUNK, LANES), 1)
    low_half = lane < (LANES // 2)
    zero = jnp.zeros((CHUNK, LANES), BF16)
    for c in range(tm // CHUNK):
        rows = slice(c * CHUNK, (c + 1) * CHUNK)
        vc = vn[rows, :]
        mixed = []
        for j in range(SGU_WIDTH // LANES):
            blk = vc[:, j * LANES:(j + 1) * LANES]
            rhs = jnp.concatenate([jnp.where(low_half, blk, zero), jnp.where(low_half, zero, blk)], axis=0)
            mixed.append(_dot(wsp_ref[j], rhs))
        mixed = jnp.concatenate(mixed, axis=1) + bs_ref[...]
        sgu_ref[rows, :] = (u[rows, :] * mixed).astype(BF16)


def _inproj(x, mod_l, rope, w, g, tm):
    t, d = x.shape
    n = t // tm
    seq = lambda i: (g.seq_of_tile(i, tm), 0, 0)
    pos = lambda i: (g.pos_tile(i, tm), 0)
    pos_t = lambda i: (0, g.pos_tile(i, tm))
    row = lambda i: (i, 0)
    col = lambda i: (0, i)
    cos_t, sin_t, cos_tt, sin_tt = rope
    consts = [w["wa"], w["wuv"], w["wg"], w["gq"], w["gkv"], w["wqlt"], w["wqrt"], w["wqrst"],
              w["lng"], w["lnb"], w["wsp"], w["bs"]]
    return pl.pallas_call(
        _inproj_kernel,
        grid=(n,),
        in_specs=[pl.BlockSpec((tm, d), row),
                  pl.BlockSpec((None, mod_l.shape[1], d), seq),
                  pl.BlockSpec((tm, LANES), pos),
                  pl.BlockSpec((tm, LANES), pos),
                  pl.BlockSpec((QK_ROPE, tm), pos_t),
                  pl.BlockSpec((QK_ROPE, tm), pos_t)] + [_const_spec(c.shape) for c in consts],
        out_specs=[pl.BlockSpec((N_HEADS, QK_PAD, tm), lambda i: (0, 0, i)),
                   pl.BlockSpec((tm, QK_PAD), row),
                   pl.BlockSpec((V_ROWS, tm), col),
                   pl.BlockSpec((tm, SGU_WIDTH), row),
                   pl.BlockSpec((tm, 2 * d), row)],
        out_shape=[jax.ShapeDtypeStruct((N_HEADS, QK_PAD, t), BF16),
                   jax.ShapeDtypeStruct((t, QK_PAD), BF16),
                   jax.ShapeDtypeStruct((V_ROWS, t), BF16),
                   jax.ShapeDtypeStruct((t, SGU_WIDTH), BF16),
                   jax.ShapeDtypeStruct((t, 2 * d), BF16)],
        compiler_params=_cparams(("parallel",)),
        name="token_mixer_in",
    )(x, mod_l, cos_t, sin_t, cos_tt, sin_tt, *consts)


def _attn_kernel(qt_ref, k_ref, vt_ref, o_ref, m_ref, acc_ref, s_ref):
    ki = pl.program_id(2)
    nh = qt_ref.shape[0]

    @pl.when(ki == 0)
    def _():
        m_ref[...] = jnp.full(m_ref.shape, -jnp.inf, F32)
        acc_ref[...] = jnp.zeros(acc_ref.shape, F32)

    k = k_ref[...]
    tk = k.shape[0]
    def scores(hd):
        st = _dot(k, qt_ref[hd])
        s_ref[hd % 2] = st
        return jnp.max(st, axis=0, keepdims=True)

    mx_next = scores(0)
    for hd in range(nh):
        mx = mx_next
        if hd + 1 < nh:
            mx_next = scores(hd + 1)
        m_prev = m_ref[hd:hd + 1, :]
        m_new = jnp.maximum(m_prev, mx)
        alpha = jnp.exp2(m_prev - m_new)
        pv = None
        for kt in range(tk // KEY_CHUNK):
            keys = slice(kt * KEY_CHUNK, (kt + 1) * KEY_CHUNK)
            p = jnp.exp2(s_ref[hd % 2, keys, :] - m_new).astype(BF16)
            part = _dot(vt_ref[:, keys], p)
            pv = part if pv is None else pv + part
        acc_ref[hd] = alpha * acc_ref[hd] + pv
        m_ref[hd:hd + 1, :] = m_new

    @pl.when(ki == pl.num_programs(2) - 1)
    def _():
        for hd in range(nh):
            a = acc_ref[hd]
            o = a[0:LAT, :] / a[LAT:LAT + 1, :]
            o_ref[:, hd * LAT:(hd + 1) * LAT] = o.T.astype(BF16)


def _attention(qt, klat, vt, nb, s, tq, tk):
    nq, nk = s // tq, s // tk
    return pl.pallas_call(
        _attn_kernel,
        grid=(nb, nq, nk),
        in_specs=[pl.BlockSpec((N_HEADS, QK_PAD, tq), lambda b, i, j: (0, 0, b * nq + i)),
                  pl.BlockSpec((tk, QK_PAD), lambda b, i, j: (b * nk + j, 0)),
                  pl.BlockSpec((V_ROWS, tk), lambda b, i, j: (0, b * nk + j))],
        out_specs=pl.BlockSpec((tq, N_HEADS * LAT), lambda b, i, j: (b * nq + i, 0)),
        out_shape=jax.ShapeDtypeStruct((nb * s, N_HEADS * LAT), BF16),
        scratch_shapes=[pltpu.VMEM((N_HEADS, tq), F32), pltpu.VMEM((N_HEADS, V_ROWS, tq), F32),
                        pltpu.VMEM((2, tk, tq), F32)],
        compiler_params=_cparams(("parallel", "parallel", "arbitrary")),
        name="latent_attention",
    )(qt, klat, vt)


def _outproj_kernel(dn_alpha, o_ref, sgu_ref, sg_ref, x_ref, mod_ref, wfold_ref, wosgu_ref, wout_ref,
                    lng_ref, lnb_ref, rwt_ref, rb_ref, x1_ref, h2_ref, idx_ref, wts_ref):
    d = x_ref.shape[1]
    o_attn = _dot(o_ref[...], wfold_ref[...])
    o_sgu = _dot(sgu_ref[...], wosgu_ref[...])
    sg = sg_ref[...].astype(F32)
    merged = (sg[:, :d] * o_attn + sg[:, d:] * o_sgu).astype(BF16)
    mix = _dot(merged, wout_ref[...])
    g1 = mod_ref[2:3, :]
    sh2 = mod_ref[3:4, :]
    sc2 = mod_ref[4:5, :]
    x1 = _layer_norm(dn_alpha * x_ref[...] + (1.0 + g1) * mix, lng_ref[...], lnb_ref[...])
    x1_ref[...] = x1
    h2 = x1 * (1.0 + sc2) + sh2
    h2_ref[...] = _pack_bf16_pairs(h2)

    logits = lax.dot_general(rwt_ref[...], h2, (((1,), (1,)), ((), ())), preferred_element_type=F32,
                             precision=lax.Precision.HIGHEST) + rb_ref[...]
    e_iota = lax.broadcasted_iota(I32, logits.shape, 0)
    vals, idxs = [], []
    for _ in range(TOP_K):
        mx = jnp.max(logits, axis=0, keepdims=True)
        ix = jnp.min(jnp.where(logits == mx, e_iota, N_EXPERTS), axis=0, keepdims=True)
        vals.append(mx)
        idxs.append(ix)
        logits = jnp.where(e_iota == ix, -jnp.inf, logits)
    v = jnp.concatenate(vals, axis=0)
    w = jnp.exp(v - v[0:1, :])
    wts_ref[...] = w / jnp.sum(w, axis=0, keepdims=True)
    idx_ref[...] = jnp.concatenate(idxs, axis=0)


def _outproj(o, sgu, sg, x, mod_l, w, g, tm, dn_alpha):
    t, d = x.shape
    n = t // tm
    seq = lambda i: (g.seq_of_tile(i, tm), 0, 0)
    row = lambda i: (i, 0)
    col = lambda i: (0, i)
    consts = [w["wfold"], w["wosgu"], w["wout"], w["ln1g"], w["ln1b"], w["rwt"], w["rb"]]
    return pl.pallas_call(
        functools.partial(_outproj_kernel, dn_alpha),
        grid=(n,),
        in_specs=[pl.BlockSpec((tm, N_HEADS * LAT), row),
                  pl.BlockSpec((tm, SGU_WIDTH), row),
                  pl.BlockSpec((tm, 2 * d), row),
                  pl.BlockSpec((tm, d), row),
                  pl.BlockSpec((None, mod_l.shape[1], d), seq)] + [_const_spec(c.shape) for c in consts],
        out_specs=[pl.BlockSpec((tm, d), row),
                   pl.BlockSpec((tm, d // 2), row),
                   pl.BlockSpec((TOP_K, tm), col),
                   pl.BlockSpec((TOP_K, tm), col)],
        out_shape=[jax.ShapeDtypeStruct((t, d), F32),
                   jax.ShapeDtypeStruct((t, d // 2), I32),
                   jax.ShapeDtypeStruct((TOP_K, t), I32),
                   jax.ShapeDtypeStruct((TOP_K, t), F32)],
        compiler_params=_cparams(("parallel",)),
        name="token_mixer_out",
    )(o, sgu, sg, x, mod_l, *consts)


def _one_hots(idx, tm):
    e_iota = lax.broadcasted_iota(I32, (N_EXPERTS, tm), 0)
    return [(e_iota == idx[k:k + 1, :]).astype(F32) for k in range(TOP_K)]


def _count_kernel(idx_ref, cnt_ref):
    @pl.when(pl.program_id(0) == 0)
    def _():
        cnt_ref[...] = jnp.zeros(cnt_ref.shape, F32)

    oh = sum(_one_hots(idx_ref[...], idx_ref.shape[1]))
    cnt_ref[...] += jnp.sum(oh, axis=1, keepdims=True)


def _dest_kernel(idx_ref, start_ref, dest_ref, carry_ref):
    tm = idx_ref.shape[1]

    @pl.when(pl.program_id(0) == 0)
    def _():
        carry_ref[...] = start_ref[...]

    ohs = _one_hots(idx_ref[...], tm)
    oh = sum(ohs)
    r = lax.broadcasted_iota(I32, (tm, tm), 0)
    c = lax.broadcasted_iota(I32, (tm, tm), 1)
    before = (r < c).astype(BF16)
    rank = _dot(oh.astype(BF16), before) + carry_ref[:, 0:1]
    dest_ref[...] = jnp.concatenate(
        [jnp.sum(o * rank, axis=0, keepdims=True) for o in ohs], axis=0).astype(I32)
    carry_ref[...] += jnp.sum(oh, axis=1, keepdims=True)


def _route(idx, tm):
    k, t = idx.shape
    n = t // tm
    col = lambda i: (0, i)
    cnt = pl.pallas_call(
        _count_kernel,
        grid=(n,),
        in_specs=[pl.BlockSpec((k, tm), col)],
        out_specs=_const_spec((N_EXPERTS, LANES)),
        out_shape=jax.ShapeDtypeStruct((N_EXPERTS, LANES), F32),
        compiler_params=_cparams(("arbitrary",)),
        name="expert_counts",
    )(idx)
    counts = cnt[:, 0].astype(I32)
    padded = (counts + MOE_ROWS - 1) // MOE_ROWS * MOE_ROWS
    pad_end = jnp.cumsum(padded)
    pad_start = pad_end - padded
    start = jnp.broadcast_to(pad_start.astype(F32)[:, None], (N_EXPERTS, LANES))
    dest = pl.pallas_call(
        _dest_kernel,
        grid=(n,),
        in_specs=[pl.BlockSpec((k, tm), col), _const_spec((N_EXPERTS, LANES))],
        out_specs=pl.BlockSpec((k, tm), col),
        out_shape=jax.ShapeDtypeStruct((k, t), I32),
        scratch_shapes=[pltpu.VMEM((N_EXPERTS, LANES), F32)],
        compiler_params=_cparams(("arbitrary",)),
        name="expert_dest_rows",
    )(idx, start)
    return dest, counts, padded, pad_start, pad_end


def _count_le(sorted_vals, v):
    return jnp.sum((sorted_vals[None, :] <= v[:, None]).astype(I32), axis=1)


def _padding_rows(counts, padded, pad_start, pad_end, n_pad):
    per = padded - counts
    cum = jnp.cumsum(per)
    offset = jnp.concatenate([pad_start + counts - (cum - per), pad_end[-1:] - cum[-1:]])
    j = jnp.arange(n_pad, dtype=I32)
    e = _count_le(cum, j)
    pick = (e[:, None] == jnp.arange(N_EXPERTS + 1, dtype=I32)[None, :]).astype(I32)
    return j + jnp.sum(pick * offset[None, :], axis=1)


def _sc_mesh():
    return plsc.VectorSubcoreMesh(core_axis_name="c", subcore_axis_name="s")


def _sc_scatter_rows(src, idx_main, idx_pad, n_out):
    nw, cpk, c = idx_main.shape
    width = src.shape[1]
    cpw = src.shape[0] // (nw * c)
    copies = cpk // cpw
    ppw = idx_pad.shape[1]

    @functools.partial(
        pl.kernel, mesh=_sc_mesh(),
        out_type=jax.ShapeDtypeStruct((n_out, width), src.dtype),
        scratch_types=[pltpu.VMEM((cpk, c), I32), pltpu.VMEM((ppw, c), I32),
                       pltpu.VMEM((c, width), src.dtype), pltpu.SemaphoreType.DMA],
        name="dispatch_rows")
    def run(src_hbm, idx_hbm, pad_hbm, out_hbm, idx_v, pad_v, rows_v, sem):
        wid = lax.axis_index("s") * SC_CORES + lax.axis_index("c")
        pltpu.sync_copy(idx_hbm.at[wid], idx_v)
        pltpu.sync_copy(pad_hbm.at[wid], pad_v)

        @pl.loop(0, cpw)
        def _(j):
            pltpu.sync_copy(src_hbm.at[pl.ds((wid * cpw + j) * c, c)], rows_v)
            cps = [pltpu.async_copy(rows_v, out_hbm.at[idx_v.at[j * copies + kk]], sem)
                   for kk in range(copies)]
            for cp in cps:
                cp.wait()

        n_src_chunks = src.shape[0] // c

        @pl.loop(0, ppw)
        def _(j):
            pltpu.sync_copy(src_hbm.at[pl.ds(((wid * ppw + j) % n_src_chunks) * c, c)], rows_v)
            pltpu.async_copy(rows_v, out_hbm.at[pad_v.at[j]], sem).wait()

    return run(src, idx_main, idx_pad)


def _sc_gather_rows(src, idx):
    nw, cpw, c = idx.shape
    width = src.shape[1]

    @functools.partial(
        pl.kernel, mesh=_sc_mesh(),
        out_type=jax.ShapeDtypeStruct((nw * cpw * c, width), src.dtype),
        scratch_types=[pltpu.VMEM((cpw, c), I32), pltpu.VMEM((c, width), src.dtype),
                       pltpu.SemaphoreType.DMA],
        name="collect_rows")
    def run(src_hbm, idx_hbm, out_hbm, idx_v, rows_v, sem):
        wid = lax.axis_index("s") * SC_CORES + lax.axis_index("c")
        pltpu.sync_copy(idx_hbm.at[wid], idx_v)

        @pl.loop(0, cpw)
        def _(j):
            pltpu.async_copy(src_hbm.at[idx_v.at[j]], rows_v, sem).wait()
            pltpu.sync_copy(rows_v, out_hbm.at[pl.ds((wid * cpw + j) * c, c)])

    return run(src, idx)


def _dispatch(h2, dest, pad_rows, n_rows):
    k, t = dest.shape
    cpw = t // (SC_WORKERS * SC_CHUNK)
    idx_main = dest.reshape(k, SC_WORKERS, cpw, SC_CHUNK).transpose(1, 2, 0, 3)
    idx_main = idx_main.reshape(SC_WORKERS, cpw * k, SC_CHUNK)
    idx_pad = pad_rows.reshape(SC_WORKERS, -1, SC_CHUNK)
    return _sc_scatter_rows(h2, idx_main, idx_pad, n_rows)


def _collect(y, dest):
    k, t = dest.shape
    idx = dest.reshape(SC_WORKERS, (k * t) // (SC_WORKERS * SC_CHUNK), SC_CHUNK)
    return _sc_gather_rows(y, idx).reshape(k, t, y.shape[1])


def _moe_kernel(be_ref, x_ref, wgu_ref, bgu_ref, wd_ref, bd_ref, y_ref):
    in_use = pl.program_id(0) < be_ref[pl.num_programs(0)]

    @pl.when(in_use)
    def _():
        lo, hi = _unpack_bf16_pairs(x_ref[...])
        x = jnp.concatenate([lo, hi], axis=1).astype(BF16)
        gu = _dot(x, wgu_ref[...]) + bgu_ref[...]
        acts = []
        for b in range(gu.shape[1] // GU_BLOCK):
            glu = jnp.minimum(gu[:, b * GU_BLOCK:b * GU_BLOCK + LANES], SWIGLU_LIMIT)
            lin = jnp.clip(gu[:, b * GU_BLOCK + LANES:(b + 1) * GU_BLOCK], -SWIGLU_LIMIT, SWIGLU_LIMIT)
            acts.append(glu * jax.nn.sigmoid(SWIGLU_ALPHA * glu) * (lin + 1.0))
        act = jnp.concatenate(acts, axis=1).astype(BF16)
        y_ref[...] = _pack_bf16_pairs(_dot(act, wd_ref[...]) + bd_ref[...])

    @pl.when(jnp.logical_not(in_use))
    def _():
        y_ref[...] = jnp.zeros(y_ref.shape, y_ref.dtype)


def _moe_ffn(xs, block_e, w, l):
    n_rows, half = xs.shape
    d = 2 * half
    f2 = w["wgu"].shape[3]
    n_blocks = n_rows // MOE_ROWS
    row = lambda i, be: (i, 0)
    exp = lambda i, be: (l, be[i], 0, 0)
    return pl.pallas_call(
        _moe_kernel,
        grid_spec=pltpu.PrefetchScalarGridSpec(
            num_scalar_prefetch=1,
            grid=(n_blocks,),
            in_specs=[pl.BlockSpec((MOE_ROWS, half), row),
                      pl.BlockSpec((None, None, d, f2), exp), pl.BlockSpec((None, None, 1, f2), exp),
                      pl.BlockSpec((None, None, f2 // 2, d), exp), pl.BlockSpec((None, None, 1, d), exp)],
            out_specs=pl.BlockSpec((MOE_ROWS, half), row)),
        out_shape=jax.ShapeDtypeStruct((n_rows, half), I32),
        compiler_params=_cparams(("arbitrary",)),
        name="expert_ffn",
    )(block_e, xs, w["wgu"], w["bgu"], w["wdown"], w["bdown"])


def _combine_kernel(dn_alpha, yg_ref, wts_ref, x1_ref, mod_ref, lng_ref, lnb_ref, x2_ref):
    wts = wts_ref[...]
    lo = hi = None
    for k in range(TOP_K):
        l_k, h_k = _unpack_bf16_pairs(yg_ref[k])
        w_k = wts[:, k:k + 1]
        lo = l_k * w_k if lo is None else lo + l_k * w_k
        hi = h_k * w_k if hi is None else hi + h_k * w_k
    ff = jnp.concatenate([lo, hi], axis=1)
    g2 = mod_ref[5:6, :]
    x2_ref[...] = _layer_norm(dn_alpha * x1_ref[...] + (1.0 + g2) * ff, lng_ref[...], lnb_ref[...])


def _combine(yg, wts_t, x1, mod_l, ln_g, ln_b, g, tm, dn_alpha):
    t, d = x1.shape
    row = lambda i: (i, 0)
    seq = lambda i: (g.seq_of_tile(i, tm), 0, 0)
    return pl.pallas_call(
        functools.partial(_combine_kernel, dn_alpha),
        grid=(t // tm,),
        in_specs=[pl.BlockSpec((TOP_K, tm, d // 2), lambda i: (0, i, 0)),
                  pl.BlockSpec((tm, TOP_K), row),
                  pl.BlockSpec((tm, d), row),
                  pl.BlockSpec((None, mod_l.shape[1], d), seq),
                  _const_spec(ln_g.shape), _const_spec(ln_b.shape)],
        out_specs=pl.BlockSpec((tm, d), row),
        out_shape=jax.ShapeDtypeStruct((t, d), F32),
        compiler_params=_cparams(("parallel",)),
        name="moe_combine",
    )(yg, wts_t, x1, mod_l, ln_g, ln_b)


def _deinterleave_kernel(w_ref, perm_ref, o_ref):
    w = w_ref[...].astype(BF16)
    for b in range(w.shape[1] // GU_BLOCK):
        cols = slice(b * GU_BLOCK, (b + 1) * GU_BLOCK)
        o_ref[:, cols] = _dot(w[:, cols], perm_ref[...]).astype(BF16)


def _deinterleave_gate_up(w_gate_up):
    depth, ne, d, f2 = w_gate_up.shape
    tn = _pick(f2, (1024, 512, GU_BLOCK))
    j = jnp.arange(GU_BLOCK)
    target = (j % 2) * LANES + j // 2
    perm = (target[:, None] == jnp.arange(GU_BLOCK)[None, :]).astype(BF16)
    out = pl.pallas_call(
        _deinterleave_kernel,
        grid=(depth * ne, f2 // tn),
        in_specs=[pl.BlockSpec((None, d, tn), lambda e, j: (e, 0, j)), _const_spec((GU_BLOCK, GU_BLOCK))],
        out_specs=pl.BlockSpec((None, d, tn), lambda e, j: (e, 0, j)),
        out_shape=jax.ShapeDtypeStruct((depth * ne, d, f2), BF16),
        compiler_params=_cparams(("parallel", "parallel")),
        name="gate_up_columns",
    )(w_gate_up.reshape(depth * ne, d, f2), perm)
    return out.reshape(depth, ne, d, f2)


def _deinterleave_bias(b_gate_up):
    depth, ne, f2 = b_gate_up.shape
    b = b_gate_up.reshape(depth, ne, f2 // GU_BLOCK, LANES, 2).transpose(0, 1, 2, 4, 3)
    return b.reshape(depth, ne, 1, f2)


def _rope_tables(seq_len):
    inv = 1.0 / (ROPE_BASE ** (jnp.arange(0, QK_ROPE, 2, dtype=F32) / QK_ROPE))
    ang = jnp.arange(seq_len, dtype=F32)[:, None] * inv[None, :]
    pad = jnp.zeros((seq_len, LANES - QK_ROPE), F32)
    cos2 = jnp.concatenate([jnp.cos(ang), jnp.cos(ang)], axis=1)
    sin2 = jnp.concatenate([jnp.sin(ang), jnp.sin(ang)], axis=1)
    return (jnp.concatenate([cos2, pad], axis=1), jnp.concatenate([sin2, pad], axis=1), cos2.T, sin2.T)


def _rot_pair(w):
    half = w.shape[-1] // 2
    return jnp.concatenate([-w[..., half:], w[..., :half]], axis=-1)


def _pad_lanes(w):
    return jnp.pad(w, [(0, 0)] * (w.ndim - 1) + [(0, LANES - w.shape[-1])])


def _prep_layer(l, p):
    d = p["w_in"].shape[1]
    w_in = p["w_in"][l]
    i1 = Q_RANK
    i2 = i1 + KV_RANK
    i3 = i2 + QK_ROPE
    i4 = i3 + 2 * SGU_WIDTH
    w_kr = w_in[:, i2:i3]
    w = {}
    w["wa"] = jnp.concatenate([w_in[:, :i2], _pad_lanes(w_kr), _pad_lanes(_rot_pair(w_kr))], axis=1).astype(BF16)
    w["wuv"] = w_in[:, i3:i4].astype(BF16)
    w["wg"] = w_in[:, i4:].astype(BF16)
    w["gq"] = p["q_norm_g"][l][None, :]
    w["gkv"] = p["kv_norm_g"][l][None, :]

    w_uq = p["w_uq"][l].reshape(Q_RANK, N_HEADS, QK_NOPE + QK_ROPE)
    w_ukv = p["w_ukv"][l].reshape(KV_RANK, N_HEADS, QK_NOPE + V_DIM)
    uq_nope = w_uq[:, :, :QK_NOPE].transpose(1, 0, 2)
    uk_t = w_ukv[:, :, :QK_NOPE].transpose(1, 2, 0)
    w["wqlt"] = _bmm(uq_nope, uk_t).transpose(0, 2, 1).reshape(N_HEADS * LAT, Q_RANK).astype(BF16)
    uq_rope = w_uq[:, :, QK_NOPE:]
    w["wqrt"] = uq_rope.reshape(Q_RANK, N_HEADS * QK_ROPE).T.astype(BF16)
    w["wqrst"] = _rot_pair(uq_rope).reshape(Q_RANK, N_HEADS * QK_ROPE).T.astype(BF16)
    uv = w_ukv[:, :, QK_NOPE:].transpose(1, 0, 2)
    wo = p["w_o_attn"][l].reshape(N_HEADS, V_DIM, d)
    w["wfold"] = _bmm(uv, wo).reshape(N_HEADS * LAT, d).astype(BF16)

    w["lng"] = p["sgu_ln_g"][l][None, :]
    w["lnb"] = p["sgu_ln_b"][l][None, :]
    gpl = LANES // (SGU_WIDTH // SGU_GROUPS)
    ws = p["w_s"][l].reshape(SGU_GROUPS // gpl, gpl, CHUNK, CHUNK)
    w["wsp"] = ws.transpose(0, 2, 1, 3).reshape(SGU_GROUPS // gpl, CHUNK, gpl * CHUNK).astype(BF16)
    w["bs"] = jnp.repeat(p["b_s"][l].T, SGU_WIDTH // SGU_GROUPS, axis=1)
    w["wosgu"] = p["w_o_sgu"][l].astype(BF16)
    w["wout"] = p["w_out"][l].astype(BF16)
    w["ln1g"] = p["ln1_g"][l][None, :]
    w["ln1b"] = p["ln1_b"][l][None, :]
    w["rwt"] = p["router_w"][l].T
    w["rb"] = p["router_b"][l][:, None]
    w["ln2g"] = p["ln2_g"][l][None, :]
    w["ln2b"] = p["ln2_b"][l][None, :]
    return w


def _route_and_dispatch(h2, idx):
    t = h2.shape[0]
    n_pad = N_EXPERTS * MOE_ROWS
    n_rows = TOP_K * t + n_pad
    dest, counts, padded, pad_start, pad_end = _route(idx, _pick(t, (512, 256, 128)))
    pad_rows = _padding_rows(counts, padded, pad_start, pad_end, n_pad)
    block_start = jnp.arange(n_rows // MOE_ROWS, dtype=I32) * MOE_ROWS
    block_e = jnp.minimum(_count_le(pad_end, block_start), N_EXPERTS - 1)
    blocks_in_use = pad_end[-1:] // MOE_ROWS
    return _dispatch(h2, dest, pad_rows, n_rows), jnp.concatenate([block_e, blocks_in_use]), dest


def kernel(x_prompt, x_sample, c_prompt, c_sample, w_mod, b_mod, w_in, q_norm_g, kv_norm_g, w_uq, w_ukv, w_o_attn, sgu_ln_g, sgu_ln_b, w_s, b_s, w_o_sgu, w_out, ln1_g, ln1_b, router_w, router_b, w_gate_up, b_gate_up, w_down, b_down, ln2_g, ln2_b):
    p = dict(w_in=w_in, q_norm_g=q_norm_g, kv_norm_g=kv_norm_g, w_uq=w_uq, w_ukv=w_ukv, w_o_attn=w_o_attn,
             sgu_ln_g=sgu_ln_g, sgu_ln_b=sgu_ln_b, w_s=w_s, b_s=b_s, w_o_sgu=w_o_sgu, w_out=w_out,
             ln1_g=ln1_g, ln1_b=ln1_b, router_w=router_w, router_b=router_b, ln2_g=ln2_g, ln2_b=ln2_b)
    depth = w_mod.shape[0]
    d = x_prompt.shape[-1]
    dn_alpha = (2 * depth) ** 0.25
    groups = [_Seqs(x_prompt.shape[0], x_prompt.shape[1], 0),
              _Seqs(x_sample.shape[0], x_sample.shape[1], x_prompt.shape[0])]
    xs = [x_prompt.reshape(-1, d), x_sample.reshape(-1, d)]
    for g in groups:
        assert g.t % (SC_WORKERS * SC_CHUNK) == 0
    assert (N_EXPERTS * MOE_ROWS) % (SC_WORKERS * SC_CHUNK) == 0

    n_seq = sum(g.b for g in groups)
    ns_pad = -(-n_seq // 8) * 8
    c_all = jnp.pad(jnp.concatenate([c_prompt, c_sample], axis=0), ((0, ns_pad - n_seq), (0, 0)))
    mod = _modulation(c_all, w_mod, b_mod)
    rope = _rope_tables(max(g.s for g in groups))
    moe_w = dict(wgu=_deinterleave_gate_up(w_gate_up), bgu=_deinterleave_bias(b_gate_up),
                 wdown=w_down.astype(BF16), bdown=b_down[:, :, None, :])

    def tiles(g):
        return (_pick(g.s, (512, 256, 128)), _pick(g.s, (1024, 512, 256, 128)), _pick(g.s, (2048, 1024, 512, 256)))

    for l in range(depth):
        w = dict(_prep_layer(l, p), **moe_w)
        mixer_in = [_inproj(x, mod[l], rope, w, g, tiles(g)[0]) for x, g in zip(xs, groups)]
        routed = []
        for x, g, (qt, klat, vt, sgu, sg) in zip(xs, groups, mixer_in):
            tm, tq, tk = tiles(g)
            o = _attention(qt, klat, vt, g.b, g.s, tq, tk)
            x1, h2, idx, wts = _outproj(o, sgu, sg, x, mod[l], w, g, tm, dn_alpha)
            routed.append((x1, wts) + _route_and_dispatch(h2, idx))
        collected = [(x1, wts, _collect(_moe_ffn(rows, block_e, w, l), dest))
                     for x1, wts, rows, block_e, dest in routed]
        xs = [_combine(yg, wts.T, x1, mod[l], w["ln2g"], w["ln2b"], g, tiles(g)[0], dn_alpha)
              for (x1, wts, yg), g in zip(collected, groups)]

    return tuple(x.reshape(g.b, g.s, d) for x, g in zip(xs, groups))
```

```python
import functools
import math

import jax
import jax.numpy as jnp
from jax import lax
from jax.experimental import pallas as pl
from jax.experimental.pallas import tpu as pltpu
from jax.experimental.pallas import tpu_sc as plsc

F32 = jnp.float32
BF16 = jnp.bfloat16
I32 = jnp.int32

N_HEADS = 8
QK_NOPE = 64
QK_ROPE = 32
V_DIM = 64
Q_RANK = 256
KV_RANK = 128
ROPE_BASE = 10000.0
CHUNK = 128
SGU_WIDTH = 512
SGU_GROUPS = 8
N_EXPERTS = 32
TOP_K = 4
SWIGLU_LIMIT = 7.0
SWIGLU_ALPHA = 1.702
LN_EPS = 1e-5
RMS_EPS = 1e-6
ATTN_SCALE = 1.0 / math.sqrt(QK_NOPE + QK_ROPE)
Q_SCALE = ATTN_SCALE * math.log2(math.e)

LANES = 128
LAT = KV_RANK
QK_PAD = 2 * LANES
V_ROWS = LAT + 16
KEY_CHUNK = 256
MOE_ROWS = 512
GU_BLOCK = 2 * LANES
VMEM_LIMIT = 56 * 1024 * 1024
SC_CORES = 2
SC_SUBCORES = 16
SC_WORKERS = SC_CORES * SC_SUBCORES
SC_CHUNK = 64


def _cparams(sem, vmem=VMEM_LIMIT):
    return pltpu.CompilerParams(dimension_semantics=sem, vmem_limit_bytes=vmem)


def _pick(n, prefs):
    for p in prefs:
        if n % p == 0:
            return p
    raise ValueError(f"no tile in {prefs} divides {n}")


def _const_spec(shape):
    nd = len(shape)
    return pl.BlockSpec(shape, lambda *_: (0,) * nd)


def _dot(a, b):
    return jnp.dot(a, b, preferred_element_type=F32)


def _layer_norm(y, g, b):
    mu = jnp.mean(y, axis=-1, keepdims=True)
    d = y - mu
    var = jnp.mean(d * d, axis=-1, keepdims=True)
    return d * lax.rsqrt(var + LN_EPS) * g + b


def _rms_norm(y, g):
    return y * lax.rsqrt(jnp.mean(y * y, axis=-1, keepdims=True) + RMS_EPS) * g


HIGH_HALF = -65536


def _pack_bf16_pairs(y):
    w = y.shape[1] // 2
    bits = lax.bitcast_convert_type(y.astype(BF16).astype(F32), I32)
    return (bits[:, w:] & HIGH_HALF) | lax.shift_right_logical(bits[:, :w], 16)


def _unpack_bf16_pairs(p):
    lo = lax.bitcast_convert_type(lax.shift_left(p, 16), F32)
    hi = lax.bitcast_convert_type(p & HIGH_HALF, F32)
    return lo, hi


class _Seqs:
    def __init__(self, b, s, seq0):
        self.b, self.s, self.seq0 = b, s, seq0
        self.t = b * s

    def seq_of_tile(self, i, tm):
        return self.seq0 + (i * tm) // self.s

    def pos_tile(self, i, tm):
        return i % (self.s // tm)


def _mod_kernel(c_ref, w_ref, b_ref, o_ref):
    c = c_ref[...]
    a = c * jax.nn.sigmoid(c)
    o_ref[0] = jnp.dot(a, w_ref[0], preferred_element_type=F32,
                       precision=lax.Precision.HIGHEST) + b_ref[0]


def _modulation(c_all, w_mod, b_mod):
    depth, d, six_d = w_mod.shape
    ns = c_all.shape[0]
    n_col = six_d // d
    out = pl.pallas_call(
        _mod_kernel,
        grid=(depth, n_col),
        in_specs=[
            pl.BlockSpec((ns, d), lambda l, j: (0, 0)),
            pl.BlockSpec((1, d, d), lambda l, j: (l, 0, j)),
            pl.BlockSpec((1, 1, d), lambda l, j: (l, 0, j)),
        ],
        out_specs=pl.BlockSpec((1, ns, d), lambda l, j: (l, 0, j)),
        out_shape=jax.ShapeDtypeStruct((depth, ns, six_d), F32),
        compiler_params=_cparams(("parallel", "parallel")),
        name="modulation",
    )(c_all, w_mod, b_mod.reshape(depth, 1, six_d))
    return out.reshape(depth, ns, n_col, d)


def _bmm_kernel(a_ref, b_ref, o_ref):
    o_ref[0] = jnp.dot(a_ref[0], b_ref[0], preferred_element_type=F32,
                       precision=lax.Precision.HIGHEST)


def _bmm(a, b):
    h, m, k = a.shape
    n = b.shape[2]
    return pl.pallas_call(
        _bmm_kernel,
        grid=(h,),
        in_specs=[pl.BlockSpec((1, m, k), lambda i: (i, 0, 0)),
                  pl.BlockSpec((1, k, n), lambda i: (i, 0, 0))],
        out_specs=pl.BlockSpec((1, m, n), lambda i: (i, 0, 0)),
        out_shape=jax.ShapeDtypeStruct((h, m, n), F32),
        compiler_params=_cparams(("parallel",)),
        name="weight_fold",
    )(a, b)


def _gelu_tanh(x):
    c = math.sqrt(2.0 / math.pi)
    return 0.5 * x * (1.0 + jnp.tanh(c * (x + 0.044715 * (x * x * x))))


def _inproj_kernel(x_ref, mod_ref, cos_ref, sin_ref, cost_ref, sint_ref, wa_ref, wuv_ref, wg_ref, gq_ref,
                   gkv_ref, wqlt_ref, wqrt_ref, wqrst_ref, lng_ref, lnb_ref, wsp_ref, bs_ref,
                   q_ref, klat_ref, vt_ref, sgu_ref, sg_ref):
    tm = x_ref.shape[0]
    x = x_ref[...]
    sh1 = mod_ref[0:1, :]
    sc1 = mod_ref[1:2, :]
    h = (x * (1.0 + sc1) + sh1).astype(BF16)

    za = _dot(h, wa_ref[...])
    cq = za[:, 0:Q_RANK]
    ckv = za[:, Q_RANK:Q_RANK + KV_RANK]
    kr = za[:, Q_RANK + KV_RANK:Q_RANK + KV_RANK + LANES]
    krs = za[:, Q_RANK + KV_RANK + LANES:Q_RANK + KV_RANK + 2 * LANES]
    cos = cos_ref[...]
    sin = sin_ref[...]
    cqn = _rms_norm(cq, gq_ref[...]).astype(BF16)
    ckvn = _rms_norm(ckv, gkv_ref[...])
    klat_ref[:, 0:LAT] = ckvn.astype(BF16)
    klat_ref[:, LAT:QK_PAD] = (kr * cos + krs * sin).astype(BF16)
    vt_ref[0:LAT, :] = ckvn.T.astype(BF16)
    extra = lax.broadcasted_iota(I32, (V_ROWS - LAT, tm), 0)
    vt_ref[LAT:V_ROWS, :] = jnp.where(extra == 0, 1.0, 0.0).astype(BF16)

    nt = (((1,), (1,)), ((), ()))
    qlt = lax.dot_general(wqlt_ref[...], cqn, nt, preferred_element_type=F32)
    qrt = lax.dot_general(wqrt_ref[...], cqn, nt, preferred_element_type=F32)
    qrst = lax.dot_general(wqrst_ref[...], cqn, nt, preferred_element_type=F32)
    cos_q = cost_ref[...] * Q_SCALE
    sin_q = sint_ref[...] * Q_SCALE
    pad = jnp.zeros((QK_PAD - LAT - QK_ROPE, tm), BF16)
    for hd in range(N_HEADS):
        q_ref[hd, 0:LAT, :] = (qlt[hd * LAT:(hd + 1) * LAT, :] * Q_SCALE).astype(BF16)
        rows = slice(hd * QK_ROPE, (hd + 1) * QK_ROPE)
        q_ref[hd, LAT:LAT + QK_ROPE, :] = (qrt[rows, :] * cos_q + qrst[rows, :] * sin_q).astype(BF16)
        q_ref[hd, LAT + QK_ROPE:QK_PAD, :] = pad

    sg_ref[...] = jax.nn.sigmoid(_dot(h, wg_ref[...])).astype(BF16)

    guv = _gelu_tanh(_dot(h, wuv_ref[...]))
    u = guv[:, :SGU_WIDTH]
    vn = _layer_norm(guv[:, SGU_WIDTH:], lng_ref[...], lnb_ref[...]).astype(BF16)
    lane = lax.broadcasted_iota(I32, (CHUNK, LANES), 1)
    low_half = lane < (LANES // 2)
    zero = jnp.zeros((CHUNK, LANES), BF16)
    for c in range(tm // CHUNK):
        rows = slice(c * CHUNK, (c + 1) * CHUNK)
        vc = vn[rows, :]
        mixed = []
        for j in range(SGU_WIDTH // LANES):
            blk = vc[:, j * LANES:(j + 1) * LANES]
            rhs = jnp.concatenate([jnp.where(low_half, blk, zero), jnp.where(low_half, zero, blk)], axis=0)
            mixed.append(_dot(wsp_ref[j], rhs))
        mixed = jnp.concatenate(mixed, axis=1) + bs_ref[...]
        sgu_ref[rows, :] = (u[rows, :] * mixed).astype(BF16)


def _inproj(x, mod_l, rope, w, g, tm):
    t, d = x.shape
    n = t // tm
    seq = lambda i: (g.seq_of_tile(i, tm), 0, 0)
    pos = lambda i: (g.pos_tile(i, tm), 0)
    pos_t = lambda i: (0, g.pos_tile(i, tm))
    row = lambda i: (i, 0)
    col = lambda i: (0, i)
    cos_t, sin_t, cos_tt, sin_tt = rope
    consts = [w["wa"], w["wuv"], w["wg"], w["gq"], w["gkv"], w["wqlt"], w["wqrt"], w["wqrst"],
              w["lng"], w["lnb"], w["wsp"], w["bs"]]
    return pl.pallas_call(
        _inproj_kernel,
        grid=(n,),
        in_specs=[pl.BlockSpec((tm, d), row),
                  pl.BlockSpec((None, mod_l.shape[1], d), seq),
                  pl.BlockSpec((tm, LANES), pos),
                  pl.BlockSpec((tm, LANES), pos),
                  pl.BlockSpec((QK_ROPE, tm), pos_t),
                  pl.BlockSpec((QK_ROPE, tm), pos_t)] + [_const_spec(c.shape) for c in consts],
        out_specs=[pl.BlockSpec((N_HEADS, QK_PAD, tm), lambda i: (0, 0, i)),
                   pl.BlockSpec((tm, QK_PAD), row),
                   pl.BlockSpec((V_ROWS, tm), col),
                   pl.BlockSpec((tm, SGU_WIDTH), row),
                   pl.BlockSpec((tm, 2 * d), row)],
        out_shape=[jax.ShapeDtypeStruct((N_HEADS, QK_PAD, t), BF16),
                   jax.ShapeDtypeStruct((t, QK_PAD), BF16),
                   jax.ShapeDtypeStruct((V_ROWS, t), BF16),
                   jax.ShapeDtypeStruct((t, SGU_WIDTH), BF16),
                   jax.ShapeDtypeStruct((t, 2 * d), BF16)],
        compiler_params=_cparams(("parallel",)),
        name="token_mixer_in",
    )(x, mod_l, cos_t, sin_t, cos_tt, sin_tt, *consts)


def _attn_kernel(qt_ref, k_ref, vt_ref, o_ref, m_ref, acc_ref, s_ref):
    ki = pl.program_id(2)
    nh = qt_ref.shape[0]

    @pl.when(ki == 0)
    def _():
        m_ref[...] = jnp.full(m_ref.shape, -jnp.inf, F32)
        acc_ref[...] = jnp.zeros(acc_ref.shape, F32)

    k = k_ref[...]
    tk = k.shape[0]
    def scores(hd):
        st = _dot(k, qt_ref[hd])
        s_ref[hd % 2] = st
        return jnp.max(st, axis=0, keepdims=True)

    mx_next = scores(0)
    for hd in range(nh):
        mx = mx_next
        if hd + 1 < nh:
            mx_next = scores(hd + 1)
        m_prev = m_ref[hd:hd + 1, :]
        m_new = jnp.maximum(m_prev, mx)
        alpha = jnp.exp2(m_prev - m_new)
        pv = None
        for kt in range(tk // KEY_CHUNK):
            keys = slice(kt * KEY_CHUNK, (kt + 1) * KEY_CHUNK)
            p = jnp.exp2(s_ref[hd % 2, keys, :] - m_new).astype(BF16)
            part = _dot(vt_ref[:, keys], p)
            pv = part if pv is None else pv + part
        acc_ref[hd] = alpha * acc_ref[hd] + pv
        m_ref[hd:hd + 1, :] = m_new

    @pl.when(ki == pl.num_programs(2) - 1)
    def _():
        for hd in range(nh):
            a = acc_ref[hd]
            o = a[0:LAT, :] / a[LAT:LAT + 1, :]
            o_ref[:, hd * LAT:(hd + 1) * LAT] = o.T.astype(BF16)


def _attention(qt, klat, vt, nb, s, tq, tk):
    nq, nk = s // tq, s // tk
    return pl.pallas_call(
        _attn_kernel,
        grid=(nb, nq, nk),
        in_specs=[pl.BlockSpec((N_HEADS, QK_PAD, tq), lambda b, i, j: (0, 0, b * nq + i)),
                  pl.BlockSpec((tk, QK_PAD), lambda b, i, j: (b * nk + j, 0)),
                  pl.BlockSpec((V_ROWS, tk), lambda b, i, j: (0, b * nk + j))],
        out_specs=pl.BlockSpec((tq, N_HEADS * LAT), lambda b, i, j: (b * nq + i, 0)),
        out_shape=jax.ShapeDtypeStruct((nb * s, N_HEADS * LAT), BF16),
        scratch_shapes=[pltpu.VMEM((N_HEADS, tq), F32), pltpu.VMEM((N_HEADS, V_ROWS, tq), F32),
                        pltpu.VMEM((2, tk, tq), F32)],
        compiler_params=_cparams(("parallel", "parallel", "arbitrary")),
        name="latent_attention",
    )(qt, klat, vt)


def _outproj_kernel(dn_alpha, o_ref, sgu_ref, sg_ref, x_ref, mod_ref, wfold_ref, wosgu_ref, wout_ref,
                    lng_ref, lnb_ref, rwt_ref, rb_ref, x1_ref, h2_ref, idx_ref, wts_ref):
    d = x_ref.shape[1]
    o_attn = _dot(o_ref[...], wfold_ref[...])
    o_sgu = _dot(sgu_ref[...], wosgu_ref[...])
    sg = sg_ref[...].astype(F32)
    merged = (sg[:, :d] * o_attn + sg[:, d:] * o_sgu).astype(BF16)
    mix = _dot(merged, wout_ref[...])
    g1 = mod_ref[2:3, :]
    sh2 = mod_ref[3:4, :]
    sc2 = mod_ref[4:5, :]
    x1 = _layer_norm(dn_alpha * x_ref[...] + (1.0 + g1) * mix, lng_ref[...], lnb_ref[...])
    x1_ref[...] = x1
    h2 = x1 * (1.0 + sc2) + sh2
    h2_ref[...] = _pack_bf16_pairs(h2)

    nt = (((1,), (1,)), ((), ()))
    h_hi = h2.astype(BF16)
    h_lo = (h2 - h_hi.astype(F32)).astype(BF16)
    rw = rwt_ref[...]
    w_hi = rw.astype(BF16)
    w_lo = (rw - w_hi.astype(F32)).astype(BF16)
    logits = (lax.dot_general(w_hi, h_hi, nt, preferred_element_type=F32)
              + lax.dot_general(w_lo, h_hi, nt, preferred_element_type=F32)
              + lax.dot_general(w_hi, h_lo, nt, preferred_element_type=F32)) + rb_ref[...]
    e_iota = lax.broadcasted_iota(I32, logits.shape, 0)
    vals, idxs = [], []
    for _ in range(TOP_K):
        mx = jnp.max(logits, axis=0, keepdims=True)
        ix = jnp.min(jnp.where(logits == mx, e_iota, N_EXPERTS), axis=0, keepdims=True)
        vals.append(mx)
        idxs.append(ix)
        logits = jnp.where(e_iota == ix, -jnp.inf, logits)
    v = jnp.concatenate(vals, axis=0)
    w = jnp.exp(v - v[0:1, :])
    wts_ref[...] = w / jnp.sum(w, axis=0, keepdims=True)
    idx_ref[...] = jnp.concatenate(idxs, axis=0)


def _outproj(o, sgu, sg, x, mod_l, w, g, tm, dn_alpha):
    t, d = x.shape
    n = t // tm
    seq = lambda i: (g.seq_of_tile(i, tm), 0, 0)
    row = lambda i: (i, 0)
    col = lambda i: (0, i)
    consts = [w["wfold"], w["wosgu"], w["wout"], w["ln1g"], w["ln1b"], w["rwt"], w["rb"]]
    return pl.pallas_call(
        functools.partial(_outproj_kernel, dn_alpha),
        grid=(n,),
        in_specs=[pl.BlockSpec((tm, N_HEADS * LAT), row),
                  pl.BlockSpec((tm, SGU_WIDTH), row),
                  pl.BlockSpec((tm, 2 * d), row),
                  pl.BlockSpec((tm, d), row),
                  pl.BlockSpec((None, mod_l.shape[1], d), seq)] + [_const_spec(c.shape) for c in consts],
        out_specs=[pl.BlockSpec((tm, d), row),
                   pl.BlockSpec((tm, d // 2), row),
                   pl.BlockSpec((TOP_K, tm), col),
                   pl.BlockSpec((TOP_K, tm), col)],
        out_shape=[jax.ShapeDtypeStruct((t, d), F32),
                   jax.ShapeDtypeStruct((t, d // 2), I32),
                   jax.ShapeDtypeStruct((TOP_K, t), I32),
                   jax.ShapeDtypeStruct((TOP_K, t), F32)],
        compiler_params=_cparams(("parallel",)),
        name="token_mixer_out",
    )(o, sgu, sg, x, mod_l, *consts)


def _one_hots(idx, tm):
    e_iota = lax.broadcasted_iota(I32, (N_EXPERTS, tm), 0)
    return [(e_iota == idx[k:k + 1, :]).astype(F32) for k in range(TOP_K)]


def _count_kernel(idx_ref, cnt_ref):
    @pl.when(pl.program_id(0) == 0)
    def _():
        cnt_ref[...] = jnp.zeros(cnt_ref.shape, F32)

    oh = sum(_one_hots(idx_ref[...], idx_ref.shape[1]))
    cnt_ref[...] += jnp.sum(oh, axis=1, keepdims=True)


def _dest_kernel(idx_ref, start_ref, dest_ref, carry_ref):
    tm = idx_ref.shape[1]

    @pl.when(pl.program_id(0) == 0)
    def _():
        carry_ref[...] = start_ref[...]

    ohs = _one_hots(idx_ref[...], tm)
    oh = sum(ohs)
    r = lax.broadcasted_iota(I32, (tm, tm), 0)
    c = lax.broadcasted_iota(I32, (tm, tm), 1)
    before = (r < c).astype(BF16)
    rank = _dot(oh.astype(BF16), before) + carry_ref[:, 0:1]
    dest_ref[...] = jnp.concatenate(
        [jnp.sum(o * rank, axis=0, keepdims=True) for o in ohs], axis=0).astype(I32)
    carry_ref[...] += jnp.sum(oh, axis=1, keepdims=True)


def _route(idx, tm):
    k, t = idx.shape
    n = t // tm
    col = lambda i: (0, i)
    cnt = pl.pallas_call(
        _count_kernel,
        grid=(n,),
        in_specs=[pl.BlockSpec((k, tm), col)],
        out_specs=_const_spec((N_EXPERTS, LANES)),
        out_shape=jax.ShapeDtypeStruct((N_EXPERTS, LANES), F32),
        compiler_params=_cparams(("arbitrary",)),
        name="expert_counts",
    )(idx)
    counts = cnt[:, 0].astype(I32)
    padded = (counts + MOE_ROWS - 1) // MOE_ROWS * MOE_ROWS
    pad_end = jnp.cumsum(padded)
    pad_start = pad_end - padded
    start = jnp.broadcast_to(pad_start.astype(F32)[:, None], (N_EXPERTS, LANES))
    dest = pl.pallas_call(
        _dest_kernel,
        grid=(n,),
        in_specs=[pl.BlockSpec((k, tm), col), _const_spec((N_EXPERTS, LANES))],
        out_specs=pl.BlockSpec((k, tm), col),
        out_shape=jax.ShapeDtypeStruct((k, t), I32),
        scratch_shapes=[pltpu.VMEM((N_EXPERTS, LANES), F32)],
        compiler_params=_cparams(("arbitrary",)),
        name="expert_dest_rows",
    )(idx, start)
    return dest, counts, padded, pad_start, pad_end


def _count_le(sorted_vals, v):
    return jnp.sum((sorted_vals[None, :] <= v[:, None]).astype(I32), axis=1)


def _padding_rows(counts, padded, pad_start, pad_end, n_pad):
    per = padded - counts
    cum = jnp.cumsum(per)
    offset = jnp.concatenate([pad_start + counts - (cum - per), pad_end[-1:] - cum[-1:]])
    j = jnp.arange(n_pad, dtype=I32)
    e = _count_le(cum, j)
    pick = (e[:, None] == jnp.arange(N_EXPERTS + 1, dtype=I32)[None, :]).astype(I32)
    return j + jnp.sum(pick * offset[None, :], axis=1)


def _sc_mesh():
    return plsc.VectorSubcoreMesh(core_axis_name="c", subcore_axis_name="s")


def _sc_scatter_rows(src, idx_main, idx_pad, n_out):
    nw, cpk, c = idx_main.shape
    width = src.shape[1]
    cpw = src.shape[0] // (nw * c)
    copies = cpk // cpw
    ppw = idx_pad.shape[1]

    @functools.partial(
        pl.kernel, mesh=_sc_mesh(),
        out_type=jax.ShapeDtypeStruct((n_out, width), src.dtype),
        scratch_types=[pltpu.VMEM((cpk, c), I32), pltpu.VMEM((ppw, c), I32),
                       pltpu.VMEM((c, width), src.dtype), pltpu.SemaphoreType.DMA],
        name="dispatch_rows")
    def run(src_hbm, idx_hbm, pad_hbm, out_hbm, idx_v, pad_v, rows_v, sem):
        wid = lax.axis_index("s") * SC_CORES + lax.axis_index("c")
        pltpu.sync_copy(idx_hbm.at[wid], idx_v)
        pltpu.sync_copy(pad_hbm.at[wid], pad_v)

        @pl.loop(0, cpw)
        def _(j):
            pltpu.sync_copy(src_hbm.at[pl.ds((wid * cpw + j) * c, c)], rows_v)
            cps = [pltpu.async_copy(rows_v, out_hbm.at[idx_v.at[j * copies + kk]], sem)
                   for kk in range(copies)]
            for cp in cps:
                cp.wait()

        n_src_chunks = src.shape[0] // c

        @pl.loop(0, ppw)
        def _(j):
            pltpu.sync_copy(src_hbm.at[pl.ds(((wid * ppw + j) % n_src_chunks) * c, c)], rows_v)
            pltpu.async_copy(rows_v, out_hbm.at[pad_v.at[j]], sem).wait()

    return run(src, idx_main, idx_pad)


def _sc_gather_rows(src, idx):
    nw, cpw, c = idx.shape
    width = src.shape[1]

    @functools.partial(
        pl.kernel, mesh=_sc_mesh(),
        out_type=jax.ShapeDtypeStruct((nw * cpw * c, width), src.dtype),
        scratch_types=[pltpu.VMEM((cpw, c), I32), pltpu.VMEM((c, width), src.dtype),
                       pltpu.SemaphoreType.DMA],
        name="collect_rows")
    def run(src_hbm, idx_hbm, out_hbm, idx_v, rows_v, sem):
        wid = lax.axis_index("s") * SC_CORES + lax.axis_index("c")
        pltpu.sync_copy(idx_hbm.at[wid], idx_v)

        @pl.loop(0, cpw)
        def _(j):
            pltpu.async_copy(src_hbm.at[idx_v.at[j]], rows_v, sem).wait()
            pltpu.sync_copy(rows_v, out_hbm.at[pl.ds((wid * cpw + j) * c, c)])

    return run(src, idx)


def _dispatch(h2, dest, pad_rows, n_rows):
    k, t = dest.shape
    cpw = t // (SC_WORKERS * SC_CHUNK)
    idx_main = dest.reshape(k, SC_WORKERS, cpw, SC_CHUNK).transpose(1, 2, 0, 3)
    idx_main = idx_main.reshape(SC_WORKERS, cpw * k, SC_CHUNK)
    idx_pad = pad_rows.reshape(SC_WORKERS, -1, SC_CHUNK)
    return _sc_scatter_rows(h2, idx_main, idx_pad, n_rows)


def _collect(y, dest):
    k, t = dest.shape
    idx = dest.reshape(SC_WORKERS, (k * t) // (SC_WORKERS * SC_CHUNK), SC_CHUNK)
    return _sc_gather_rows(y, idx).reshape(k, t, y.shape[1])


def _moe_kernel(be_ref, x_ref, wgu_ref, bgu_ref, wd_ref, bd_ref, y_ref):
    in_use = pl.program_id(0) < be_ref[pl.num_programs(0)]

    @pl.when(in_use)
    def _():
        lo, hi = _unpack_bf16_pairs(x_ref[...])
        x = jnp.concatenate([lo, hi], axis=1).astype(BF16)
        gu = _dot(x, wgu_ref[...]) + bgu_ref[...]
        acts = []
        for b in range(gu.shape[1] // GU_BLOCK):
            glu = jnp.minimum(gu[:, b * GU_BLOCK:b * GU_BLOCK + LANES], SWIGLU_LIMIT)
            lin = jnp.clip(gu[:, b * GU_BLOCK + LANES:(b + 1) * GU_BLOCK], -SWIGLU_LIMIT, SWIGLU_LIMIT)
            acts.append(glu * jax.nn.sigmoid(SWIGLU_ALPHA * glu) * (lin + 1.0))
        act = jnp.concatenate(acts, axis=1).astype(BF16)
        y_ref[...] = _pack_bf16_pairs(_dot(act, wd_ref[...].astype(BF16)) + bd_ref[...])

    @pl.when(jnp.logical_not(in_use))
    def _():
        y_ref[...] = jnp.zeros(y_ref.shape, y_ref.dtype)


def _moe_ffn(xs, block_e, w, l):
    n_rows, half = xs.shape
    d = 2 * half
    f2 = w["wgu"].shape[3]
    n_blocks = n_rows // MOE_ROWS
    row = lambda i, be: (i, 0)
    exp = lambda i, be: (l, be[i], 0, 0)
    return pl.pallas_call(
        _moe_kernel,
        grid_spec=pltpu.PrefetchScalarGridSpec(
            num_scalar_prefetch=1,
            grid=(n_blocks,),
            in_specs=[pl.BlockSpec((MOE_ROWS, half), row),
                      pl.BlockSpec((None, None, d, f2), exp), pl.BlockSpec((None, None, 1, f2), exp),
                      pl.BlockSpec((None, None, f2 // 2, d), exp), pl.BlockSpec((None, None, 1, d), exp)],
            out_specs=pl.BlockSpec((MOE_ROWS, half), row)),
        out_shape=jax.ShapeDtypeStruct((n_rows, half), I32),
        compiler_params=_cparams(("arbitrary",)),
        name="expert_ffn",
    )(block_e, xs, w["wgu"], w["bgu"], w["wdown"], w["bdown"])


def _combine_kernel(dn_alpha, yg_ref, wts_ref, x1_ref, mod_ref, lng_ref, lnb_ref, x2_ref):
    wts = wts_ref[...]
    lo = hi = None
    for k in range(TOP_K):
        l_k, h_k = _unpack_bf16_pairs(yg_ref[k])
        w_k = wts[:, k:k + 1]
        lo = l_k * w_k if lo is None else lo + l_k * w_k
        hi = h_k * w_k if hi is None else hi + h_k * w_k
    ff = jnp.concatenate([lo, hi], axis=1)
    g2 = mod_ref[5:6, :]
    x2_ref[...] = _layer_norm(dn_alpha * x1_ref[...] + (1.0 + g2) * ff, lng_ref[...], lnb_ref[...])


def _combine(yg, wts_t, x1, mod_l, ln_g, ln_b, g, tm, dn_alpha):
    t, d = x1.shape
    row = lambda i: (i, 0)
    seq = lambda i: (g.seq_of_tile(i, tm), 0, 0)
    return pl.pallas_call(
        functools.partial(_combine_kernel, dn_alpha),
        grid=(t // tm,),
        in_specs=[pl.BlockSpec((TOP_K, tm, d // 2), lambda i: (0, i, 0)),
                  pl.BlockSpec((tm, TOP_K), row),
                  pl.BlockSpec((tm, d), row),
                  pl.BlockSpec((None, mod_l.shape[1], d), seq),
                  _const_spec(ln_g.shape), _const_spec(ln_b.shape)],
        out_specs=pl.BlockSpec((tm, d), row),
        out_shape=jax.ShapeDtypeStruct((t, d), F32),
        compiler_params=_cparams(("parallel",)),
        name="moe_combine",
    )(yg, wts_t, x1, mod_l, ln_g, ln_b)


def _deinterleave_kernel(w_ref, perm_ref, o_ref):
    w = w_ref[...].astype(BF16)
    for b in range(w.shape[1] // GU_BLOCK):
        cols = slice(b * GU_BLOCK, (b + 1) * GU_BLOCK)
        o_ref[:, cols] = _dot(w[:, cols], perm_ref[...]).astype(BF16)


def _deinterleave_gate_up(w_gate_up):
    depth, ne, d, f2 = w_gate_up.shape
    tn = _pick(f2, (1024, 512, GU_BLOCK))
    j = jnp.arange(GU_BLOCK)
    target = (j % 2) * LANES + j // 2
    perm = (target[:, None] == jnp.arange(GU_BLOCK)[None, :]).astype(BF16)
    out = pl.pallas_call(
        _deinterleave_kernel,
        grid=(depth * ne, f2 // tn),
        in_specs=[pl.BlockSpec((None, d, tn), lambda e, j: (e, 0, j)), _const_spec((GU_BLOCK, GU_BLOCK))],
        out_specs=pl.BlockSpec((None, d, tn), lambda e, j: (e, 0, j)),
        out_shape=jax.ShapeDtypeStruct((depth * ne, d, f2), BF16),
        compiler_params=_cparams(("parallel", "parallel")),
        name="gate_up_columns",
    )(w_gate_up.reshape(depth * ne, d, f2), perm)
    return out.reshape(depth, ne, d, f2)


def _deinterleave_bias(b_gate_up):
    depth, ne, f2 = b_gate_up.shape
    b = b_gate_up.reshape(depth, ne, f2 // GU_BLOCK, LANES, 2).transpose(0, 1, 2, 4, 3)
    return b.reshape(depth, ne, 1, f2)


def _rope_tables(seq_len):
    inv = 1.0 / (ROPE_BASE ** (jnp.arange(0, QK_ROPE, 2, dtype=F32) / QK_ROPE))
    ang = jnp.arange(seq_len, dtype=F32)[:, None] * inv[None, :]
    pad = jnp.zeros((seq_len, LANES - QK_ROPE), F32)
    cos2 = jnp.concatenate([jnp.cos(ang), jnp.cos(ang)], axis=1)
    sin2 = jnp.concatenate([jnp.sin(ang), jnp.sin(ang)], axis=1)
    return (jnp.concatenate([cos2, pad], axis=1), jnp.concatenate([sin2, pad], axis=1), cos2.T, sin2.T)


def _rot_pair(w):
    half = w.shape[-1] // 2
    return jnp.concatenate([-w[..., half:], w[..., :half]], axis=-1)


def _pad_lanes(w):
    return jnp.pad(w, [(0, 0)] * (w.ndim - 1) + [(0, LANES - w.shape[-1])])


def _prep_layer(l, p):
    d = p["w_in"].shape[1]
    w_in = p["w_in"][l]
    i1 = Q_RANK
    i2 = i1 + KV_RANK
    i3 = i2 + QK_ROPE
    i4 = i3 + 2 * SGU_WIDTH
    w_kr = w_in[:, i2:i3]
    w = {}
    w["wa"] = jnp.concatenate([w_in[:, :i2], _pad_lanes(w_kr), _pad_lanes(_rot_pair(w_kr))], axis=1).astype(BF16)
    w["wuv"] = w_in[:, i3:i4].astype(BF16)
    w["wg"] = w_in[:, i4:].astype(BF16)
    w["gq"] = p["q_norm_g"][l][None, :]
    w["gkv"] = p["kv_norm_g"][l][None, :]

    w_uq = p["w_uq"][l].reshape(Q_RANK, N_HEADS, QK_NOPE + QK_ROPE)
    w_ukv = p["w_ukv"][l].reshape(KV_RANK, N_HEADS, QK_NOPE + V_DIM)
    uq_nope = w_uq[:, :, :QK_NOPE].transpose(1, 0, 2)
    uk_t = w_ukv[:, :, :QK_NOPE].transpose(1, 2, 0)
    w["wqlt"] = _bmm(uq_nope, uk_t).transpose(0, 2, 1).reshape(N_HEADS * LAT, Q_RANK).astype(BF16)
    uq_rope = w_uq[:, :, QK_NOPE:]
    w["wqrt"] = uq_rope.reshape(Q_RANK, N_HEADS * QK_ROPE).T.astype(BF16)
    w["wqrst"] = _rot_pair(uq_rope).reshape(Q_RANK, N_HEADS * QK_ROPE).T.astype(BF16)
    uv = w_ukv[:, :, QK_NOPE:].transpose(1, 0, 2)
    wo = p["w_o_attn"][l].reshape(N_HEADS, V_DIM, d)
    w["wfold"] = _bmm(uv, wo).reshape(N_HEADS * LAT, d).astype(BF16)

    w["lng"] = p["sgu_ln_g"][l][None, :]
    w["lnb"] = p["sgu_ln_b"][l][None, :]
    gpl = LANES // (SGU_WIDTH // SGU_GROUPS)
    ws = p["w_s"][l].reshape(SGU_GROUPS // gpl, gpl, CHUNK, CHUNK)
    w["wsp"] = ws.transpose(0, 2, 1, 3).reshape(SGU_GROUPS // gpl, CHUNK, gpl * CHUNK).astype(BF16)
    w["bs"] = jnp.repeat(p["b_s"][l].T, SGU_WIDTH // SGU_GROUPS, axis=1)
    w["wosgu"] = p["w_o_sgu"][l].astype(BF16)
    w["wout"] = p["w_out"][l].astype(BF16)
    w["ln1g"] = p["ln1_g"][l][None, :]
    w["ln1b"] = p["ln1_b"][l][None, :]
    w["rwt"] = p["router_w"][l].T
    w["rb"] = p["router_b"][l][:, None]
    w["ln2g"] = p["ln2_g"][l][None, :]
    w["ln2b"] = p["ln2_b"][l][None, :]
    return w


def _route_and_dispatch(h2, idx):
    t = h2.shape[0]
    n_pad = N_EXPERTS * MOE_ROWS
    n_rows = TOP_K * t + n_pad
    dest, counts, padded, pad_start, pad_end = _route(idx, _pick(t, (512, 256, 128)))
    pad_rows = _padding_rows(counts, padded, pad_start, pad_end, n_pad)
    block_start = jnp.arange(n_rows // MOE_ROWS, dtype=I32) * MOE_ROWS
    block_e = jnp.minimum(_count_le(pad_end, block_start), N_EXPERTS - 1)
    blocks_in_use = pad_end[-1:] // MOE_ROWS
    return _dispatch(h2, dest, pad_rows, n_rows), jnp.concatenate([block_e, blocks_in_use]), dest


def kernel(x_prompt, x_sample, c_prompt, c_sample, w_mod, b_mod, w_in, q_norm_g, kv_norm_g, w_uq, w_ukv, w_o_attn, sgu_ln_g, sgu_ln_b, w_s, b_s, w_o_sgu, w_out, ln1_g, ln1_b, router_w, router_b, w_gate_up, b_gate_up, w_down, b_down, ln2_g, ln2_b):
    p = dict(w_in=w_in, q_norm_g=q_norm_g, kv_norm_g=kv_norm_g, w_uq=w_uq, w_ukv=w_ukv, w_o_attn=w_o_attn,
             sgu_ln_g=sgu_ln_g, sgu_ln_b=sgu_ln_b, w_s=w_s, b_s=b_s, w_o_sgu=w_o_sgu, w_out=w_out,
             ln1_g=ln1_g, ln1_b=ln1_b, router_w=router_w, router_b=router_b, ln2_g=ln2_g, ln2_b=ln2_b)
    depth = w_mod.shape[0]
    d = x_prompt.shape[-1]
    dn_alpha = (2 * depth) ** 0.25
    groups = [_Seqs(x_prompt.shape[0], x_prompt.shape[1], 0),
              _Seqs(x_sample.shape[0], x_sample.shape[1], x_prompt.shape[0])]
    xs = [x_prompt.reshape(-1, d), x_sample.reshape(-1, d)]
    for g in groups:
        assert g.t % (SC_WORKERS * SC_CHUNK) == 0
    assert (N_EXPERTS * MOE_ROWS) % (SC_WORKERS * SC_CHUNK) == 0

    n_seq = sum(g.b for g in groups)
    ns_pad = -(-n_seq // 8) * 8
    c_all = jnp.pad(jnp.concatenate([c_prompt, c_sample], axis=0), ((0, ns_pad - n_seq), (0, 0)))
    mod = _modulation(c_all, w_mod, b_mod)
    rope = _rope_tables(max(g.s for g in groups))
    moe_w = dict(wgu=_deinterleave_gate_up(w_gate_up), bgu=_deinterleave_bias(b_gate_up),
                 wdown=w_down, bdown=b_down[:, :, None, :])

    def tiles(g):
        return (_pick(g.s, (512, 256, 128)), _pick(g.s, (1024, 512, 256, 128)), _pick(g.s, (2048, 1024, 512, 256)))

    for l in range(depth):
        w = dict(_prep_layer(l, p), **moe_w)
        mixer_in = [_inproj(x, mod[l], rope, w, g, tiles(g)[0]) for x, g in zip(xs, groups)]
        routed = []
        for x, g, (qt, klat, vt, sgu, sg) in zip(xs, groups, mixer_in):
            tm, tq, tk = tiles(g)
            o = _attention(qt, klat, vt, g.b, g.s, tq, tk)
            x1, h2, idx, wts = _outproj(o, sgu, sg, x, mod[l], w, g, tm, dn_alpha)
            routed.append((x1, wts) + _route_and_dispatch(h2, idx))
        collected = [(x1, wts, _collect(_moe_ffn(rows, block_e, w, l), dest))
                     for x1, wts, rows, block_e, dest in routed]
        xs = [_combine(yg, wts.T, x1, mod[l], w["ln2g"], w["ln2b"], g, tiles(g)[0], dn_alpha)
              for (x1, wts, yg), g in zip(collected, groups)]

    return tuple(x.reshape(g.b, g.s, d) for x, g in zip(xs, groups))
```

```python
import functools
import math

import jax
import jax.numpy as jnp
from jax import lax
from jax.experimental import pallas as pl
from jax.experimental.pallas import tpu as pltpu
from jax.experimental.pallas import tpu_sc as plsc

F32 = jnp.float32
BF16 = jnp.bfloat16
I32 = jnp.int32

N_HEADS = 8
QK_NOPE = 64
QK_ROPE = 32
V_DIM = 64
Q_RANK = 256
KV_RANK = 128
ROPE_BASE = 10000.0
CHUNK = 128
SGU_WIDTH = 512
SGU_GROUPS = 8
N_EXPERTS = 32
TOP_K = 4
SWIGLU_LIMIT = 7.0
SWIGLU_ALPHA = 1.702
LN_EPS = 1e-5
RMS_EPS = 1e-6
ATTN_SCALE = 1.0 / math.sqrt(QK_NOPE + QK_ROPE)
Q_SCALE = ATTN_SCALE * math.log2(math.e)

LANES = 128
LAT = KV_RANK
QK_PAD = 2 * LANES
V_ROWS = LAT + 16
KEY_CHUNK = 256
WTS_ROWS = 8
MOE_ROWS = 512
GU_BLOCK = 2 * LANES
VMEM_LIMIT = 56 * 1024 * 1024
SC_CORES = 2
SC_SUBCORES = 16
SC_WORKERS = SC_CORES * SC_SUBCORES
SC_CHUNK = 64


def _cparams(sem, vmem=VMEM_LIMIT):
    return pltpu.CompilerParams(dimension_semantics=sem, vmem_limit_bytes=vmem)


def _pick(n, prefs):
    for p in prefs:
        if n % p == 0:
            return p
    raise ValueError(f"no tile in {prefs} divides {n}")


def _const_spec(shape):
    nd = len(shape)
    return pl.BlockSpec(shape, lambda *_: (0,) * nd)


def _dot(a, b):
    return jnp.dot(a, b, preferred_element_type=F32)


def _layer_norm(y, g, b):
    mu = jnp.mean(y, axis=-1, keepdims=True)
    d = y - mu
    var = jnp.mean(d * d, axis=-1, keepdims=True)
    return d * lax.rsqrt(var + LN_EPS) * g + b


def _rms_norm(y, g):
    return y * lax.rsqrt(jnp.mean(y * y, axis=-1, keepdims=True) + RMS_EPS) * g


HIGH_HALF = -65536


def _pack_bf16_pairs(y):
    w = y.shape[1] // 2
    bits = lax.bitcast_convert_type(y.astype(BF16).astype(F32), I32)
    return (bits[:, w:] & HIGH_HALF) | lax.shift_right_logical(bits[:, :w], 16)


def _unpack_bf16_pairs(p):
    lo = lax.bitcast_convert_type(lax.shift_left(p, 16), F32)
    hi = lax.bitcast_convert_type(p & HIGH_HALF, F32)
    return lo, hi


class _Seqs:
    def __init__(self, b, s, seq0):
        self.b, self.s, self.seq0 = b, s, seq0
        self.t = b * s

    def seq_of_tile(self, i, tm):
        return self.seq0 + (i * tm) // self.s

    def pos_tile(self, i, tm):
        return i % (self.s // tm)


def _mod_kernel(c_ref, w_ref, b_ref, o_ref):
    c = c_ref[...]
    a = c * jax.nn.sigmoid(c)
    o_ref[0] = jnp.dot(a, w_ref[0], preferred_element_type=F32,
                       precision=lax.Precision.HIGHEST) + b_ref[0]


def _modulation(c_all, w_mod, b_mod):
    depth, d, six_d = w_mod.shape
    ns = c_all.shape[0]
    n_col = six_d // d
    out = pl.pallas_call(
        _mod_kernel,
        grid=(depth, n_col),
        in_specs=[
            pl.BlockSpec((ns, d), lambda l, j: (0, 0)),
            pl.BlockSpec((1, d, d), lambda l, j: (l, 0, j)),
            pl.BlockSpec((1, 1, d), lambda l, j: (l, 0, j)),
        ],
        out_specs=pl.BlockSpec((1, ns, d), lambda l, j: (l, 0, j)),
        out_shape=jax.ShapeDtypeStruct((depth, ns, six_d), F32),
        compiler_params=_cparams(("parallel", "parallel")),
        name="modulation",
    )(c_all, w_mod, b_mod.reshape(depth, 1, six_d))
    return out.reshape(depth, ns, n_col, d)


def _bmm_kernel(a_ref, b_ref, o_ref):
    o_ref[0] = jnp.dot(a_ref[0], b_ref[0], preferred_element_type=F32,
                       precision=lax.Precision.HIGHEST)


def _bmm(a, b):
    h, m, k = a.shape
    n = b.shape[2]
    return pl.pallas_call(
        _bmm_kernel,
        grid=(h,),
        in_specs=[pl.BlockSpec((1, m, k), lambda i: (i, 0, 0)),
                  pl.BlockSpec((1, k, n), lambda i: (i, 0, 0))],
        out_specs=pl.BlockSpec((1, m, n), lambda i: (i, 0, 0)),
        out_shape=jax.ShapeDtypeStruct((h, m, n), F32),
        compiler_params=_cparams(("parallel",)),
        name="weight_fold",
    )(a, b)


def _gelu_tanh(x):
    c = math.sqrt(2.0 / math.pi)
    return 0.5 * x * (1.0 + jnp.tanh(c * (x + 0.044715 * (x * x * x))))


def _inproj_kernel(x_ref, mod_ref, cos_ref, sin_ref, cost_ref, sint_ref, wa_ref, wuv_ref, wg_ref, gq_ref,
                   gkv_ref, wqlt_ref, wqrt_ref, wqrst_ref, lng_ref, lnb_ref, wsp_ref, bs_ref,
                   q_ref, klat_ref, vt_ref, sgu_ref, sg_ref):
    tm = x_ref.shape[0]
    x = x_ref[...]
    sh1 = mod_ref[0:1, :]
    sc1 = mod_ref[1:2, :]
    h = (x * (1.0 + sc1) + sh1).astype(BF16)

    za = _dot(h, wa_ref[...])
    cq = za[:, 0:Q_RANK]
    ckv = za[:, Q_RANK:Q_RANK + KV_RANK]
    kr = za[:, Q_RANK + KV_RANK:Q_RANK + KV_RANK + LANES]
    krs = za[:, Q_RANK + KV_RANK + LANES:Q_RANK + KV_RANK + 2 * LANES]
    cos = cos_ref[...]
    sin = sin_ref[...]
    cqn = _rms_norm(cq, gq_ref[...]).astype(BF16)
    ckvn = _rms_norm(ckv, gkv_ref[...])
    klat_ref[:, 0:LAT] = ckvn.astype(BF16)
    klat_ref[:, LAT:QK_PAD] = (kr * cos + krs * sin).astype(BF16)
    vt_ref[0:LAT, :] = ckvn.T.astype(BF16)
    extra = lax.broadcasted_iota(I32, (V_ROWS - LAT, tm), 0)
    vt_ref[LAT:V_ROWS, :] = jnp.where(extra == 0, 1.0, 0.0).astype(BF16)

    nt = (((1,), (1,)), ((), ()))
    qlt = lax.dot_general(wqlt_ref[...], cqn, nt, preferred_element_type=F32)
    qrt = lax.dot_general(wqrt_ref[...], cqn, nt, preferred_element_type=F32)
    qrst = lax.dot_general(wqrst_ref[...], cqn, nt, preferred_element_type=F32)
    cos_q = cost_ref[...] * Q_SCALE
    sin_q = sint_ref[...] * Q_SCALE
    pad = jnp.zeros((QK_PAD - LAT - QK_ROPE, tm), BF16)
    for hd in range(N_HEADS):
        q_ref[hd, 0:LAT, :] = (qlt[hd * LAT:(hd + 1) * LAT, :] * Q_SCALE).astype(BF16)
        rows = slice(hd * QK_ROPE, (hd + 1) * QK_ROPE)
        q_ref[hd, LAT:LAT + QK_ROPE, :] = (qrt[rows, :] * cos_q + qrst[rows, :] * sin_q).astype(BF16)
        q_ref[hd, LAT + QK_ROPE:QK_PAD, :] = pad

    sg_ref[...] = jax.nn.sigmoid(_dot(h, wg_ref[...])).astype(BF16)

    guv = _gelu_tanh(_dot(h, wuv_ref[...]))
    u = guv[:, :SGU_WIDTH]
    vn = _layer_norm(guv[:, SGU_WIDTH:], lng_ref[...], lnb_ref[...]).astype(BF16)
    lane = lax.broadcasted_iota(I32, (CHUNK, LANES), 1)
    low_half = lane < (LANES // 2)
    zero = jnp.zeros((CHUNK, LANES), BF16)
    for c in range(tm // CHUNK):
        rows = slice(c * CHUNK, (c + 1) * CHUNK)
        vc = vn[rows, :]
        mixed = []
        for j in range(SGU_WIDTH // LANES):
            blk = vc[:, j * LANES:(j + 1) * LANES]
            rhs = jnp.concatenate([jnp.where(low_half, blk, zero), jnp.where(low_half, zero, blk)], axis=0)
            mixed.append(_dot(wsp_ref[j], rhs))
        mixed = jnp.concatenate(mixed, axis=1) + bs_ref[...]
        sgu_ref[rows, :] = (u[rows, :] * mixed).astype(BF16)


def _inproj(x, mod_l, rope, w, g, tm):
    t, d = x.shape
    n = t // tm
    seq = lambda i: (g.seq_of_tile(i, tm), 0, 0)
    pos = lambda i: (g.pos_tile(i, tm), 0)
    pos_t = lambda i: (0, g.pos_tile(i, tm))
    row = lambda i: (i, 0)
    col = lambda i: (0, i)
    cos_t, sin_t, cos_tt, sin_tt = rope
    consts = [w["wa"], w["wuv"], w["wg"], w["gq"], w["gkv"], w["wqlt"], w["wqrt"], w["wqrst"],
              w["lng"], w["lnb"], w["wsp"], w["bs"]]
    return pl.pallas_call(
        _inproj_kernel,
        grid=(n,),
        in_specs=[pl.BlockSpec((tm, d), row),
                  pl.BlockSpec((None, mod_l.shape[1], d), seq),
                  pl.BlockSpec((tm, LANES), pos),
                  pl.BlockSpec((tm, LANES), pos),
                  pl.BlockSpec((QK_ROPE, tm), pos_t),
                  pl.BlockSpec((QK_ROPE, tm), pos_t)] + [_const_spec(c.shape) for c in consts],
        out_specs=[pl.BlockSpec((N_HEADS, QK_PAD, tm), lambda i: (0, 0, i)),
                   pl.BlockSpec((tm, QK_PAD), row),
                   pl.BlockSpec((V_ROWS, tm), col),
                   pl.BlockSpec((tm, SGU_WIDTH), row),
                   pl.BlockSpec((tm, 2 * d), row)],
        out_shape=[jax.ShapeDtypeStruct((N_HEADS, QK_PAD, t), BF16),
                   jax.ShapeDtypeStruct((t, QK_PAD), BF16),
                   jax.ShapeDtypeStruct((V_ROWS, t), BF16),
                   jax.ShapeDtypeStruct((t, SGU_WIDTH), BF16),
                   jax.ShapeDtypeStruct((t, 2 * d), BF16)],
        compiler_params=_cparams(("parallel",)),
        name="token_mixer_in",
    )(x, mod_l, cos_t, sin_t, cos_tt, sin_tt, *consts)


def _attn_kernel(qt_ref, k_ref, vt_ref, o_ref, m_ref, acc_ref, s_ref):
    ki = pl.program_id(2)
    nh = qt_ref.shape[0]

    @pl.when(ki == 0)
    def _():
        m_ref[...] = jnp.full(m_ref.shape, -jnp.inf, F32)
        acc_ref[...] = jnp.zeros(acc_ref.shape, F32)

    k = k_ref[...]
    tk = k.shape[0]
    def scores(hd):
        st = _dot(k, qt_ref[hd])
        s_ref[hd % 2] = st
        return jnp.max(st, axis=0, keepdims=True)

    mx_next = scores(0)
    for hd in range(nh):
        mx = mx_next
        if hd + 1 < nh:
            mx_next = scores(hd + 1)
        m_prev = m_ref[hd:hd + 1, :]
        m_new = jnp.maximum(m_prev, mx)
        alpha = jnp.exp2(m_prev - m_new)
        pv = None
        for kt in range(tk // KEY_CHUNK):
            keys = slice(kt * KEY_CHUNK, (kt + 1) * KEY_CHUNK)
            p = jnp.exp2(s_ref[hd % 2, keys, :] - m_new).astype(BF16)
            part = _dot(vt_ref[:, keys], p)
            pv = part if pv is None else pv + part
        acc_ref[hd] = alpha * acc_ref[hd] + pv
        m_ref[hd:hd + 1, :] = m_new

    @pl.when(ki == pl.num_programs(2) - 1)
    def _():
        for hd in range(nh):
            a = acc_ref[hd]
            o = a[0:LAT, :] / a[LAT:LAT + 1, :]
            o_ref[:, hd * LAT:(hd + 1) * LAT] = o.T.astype(BF16)


def _attention(qt, klat, vt, nb, s, tq, tk):
    nq, nk = s // tq, s // tk
    return pl.pallas_call(
        _attn_kernel,
        grid=(nb, nq, nk),
        in_specs=[pl.BlockSpec((N_HEADS, QK_PAD, tq), lambda b, i, j: (0, 0, b * nq + i)),
                  pl.BlockSpec((tk, QK_PAD), lambda b, i, j: (b * nk + j, 0)),
                  pl.BlockSpec((V_ROWS, tk), lambda b, i, j: (0, b * nk + j))],
        out_specs=pl.BlockSpec((tq, N_HEADS * LAT), lambda b, i, j: (b * nq + i, 0)),
        out_shape=jax.ShapeDtypeStruct((nb * s, N_HEADS * LAT), BF16),
        scratch_shapes=[pltpu.VMEM((N_HEADS, tq), F32), pltpu.VMEM((N_HEADS, V_ROWS, tq), F32),
                        pltpu.VMEM((2, tk, tq), F32)],
        compiler_params=_cparams(("parallel", "parallel", "arbitrary")),
        name="latent_attention",
    )(qt, klat, vt)


def _outproj_kernel(dn_alpha, o_ref, sgu_ref, sg_ref, x_ref, mod_ref, wfold_ref, wosgu_ref, wout_ref,
                    lng_ref, lnb_ref, rwt_ref, rb_ref, x1_ref, h2_ref, idx_ref, wts_ref):
    d = x_ref.shape[1]
    o_attn = _dot(o_ref[...], wfold_ref[...])
    o_sgu = _dot(sgu_ref[...], wosgu_ref[...])
    sg = sg_ref[...].astype(F32)
    merged = (sg[:, :d] * o_attn + sg[:, d:] * o_sgu).astype(BF16)
    mix = _dot(merged, wout_ref[...])
    g1 = mod_ref[2:3, :]
    sh2 = mod_ref[3:4, :]
    sc2 = mod_ref[4:5, :]
    x1 = _layer_norm(dn_alpha * x_ref[...] + (1.0 + g1) * mix, lng_ref[...], lnb_ref[...])
    x1_ref[...] = x1
    h2 = x1 * (1.0 + sc2) + sh2
    h2_ref[...] = _pack_bf16_pairs(h2)

    nt = (((1,), (1,)), ((), ()))
    h_hi = h2.astype(BF16)
    h_lo = (h2 - h_hi.astype(F32)).astype(BF16)
    rw = rwt_ref[...]
    w_hi = rw.astype(BF16)
    w_lo = (rw - w_hi.astype(F32)).astype(BF16)
    logits = (lax.dot_general(w_hi, h_hi, nt, preferred_element_type=F32)
              + lax.dot_general(w_lo, h_hi, nt, preferred_element_type=F32)
              + lax.dot_general(w_hi, h_lo, nt, preferred_element_type=F32)) + rb_ref[...]
    e_iota = lax.broadcasted_iota(I32, logits.shape, 0)
    vals, idxs = [], []
    for _ in range(TOP_K):
        mx = jnp.max(logits, axis=0, keepdims=True)
        ix = jnp.min(jnp.where(logits == mx, e_iota, N_EXPERTS), axis=0, keepdims=True)
        vals.append(mx)
        idxs.append(ix)
        logits = jnp.where(e_iota == ix, -jnp.inf, logits)
    v = jnp.concatenate(vals, axis=0)
    w = jnp.exp(v - v[0:1, :])
    wn = w / jnp.sum(w, axis=0, keepdims=True)
    wts_ref[...] = jnp.concatenate([wn, jnp.zeros((WTS_ROWS - TOP_K, wn.shape[1]), F32)], axis=0)
    idx_ref[...] = jnp.concatenate(idxs, axis=0)


def _outproj(o, sgu, sg, x, mod_l, w, g, tm, dn_alpha):
    t, d = x.shape
    n = t // tm
    seq = lambda i: (g.seq_of_tile(i, tm), 0, 0)
    row = lambda i: (i, 0)
    col = lambda i: (0, i)
    consts = [w["wfold"], w["wosgu"], w["wout"], w["ln1g"], w["ln1b"], w["rwt"], w["rb"]]
    return pl.pallas_call(
        functools.partial(_outproj_kernel, dn_alpha),
        grid=(n,),
        in_specs=[pl.BlockSpec((tm, N_HEADS * LAT), row),
                  pl.BlockSpec((tm, SGU_WIDTH), row),
                  pl.BlockSpec((tm, 2 * d), row),
                  pl.BlockSpec((tm, d), row),
                  pl.BlockSpec((None, mod_l.shape[1], d), seq)] + [_const_spec(c.shape) for c in consts],
        out_specs=[pl.BlockSpec((tm, d), row),
                   pl.BlockSpec((tm, d // 2), row),
                   pl.BlockSpec((TOP_K, tm), col),
                   pl.BlockSpec((WTS_ROWS, tm), col)],
        out_shape=[jax.ShapeDtypeStruct((t, d), F32),
                   jax.ShapeDtypeStruct((t, d // 2), I32),
                   jax.ShapeDtypeStruct((TOP_K, t), I32),
                   jax.ShapeDtypeStruct((WTS_ROWS, t), F32)],
        compiler_params=_cparams(("parallel",)),
        name="token_mixer_out",
    )(o, sgu, sg, x, mod_l, *consts)


def _one_hots(idx, tm):
    e_iota = lax.broadcasted_iota(I32, (N_EXPERTS, tm), 0)
    return [(e_iota == idx[k:k + 1, :]).astype(F32) for k in range(TOP_K)]


def _count_kernel(idx_ref, cnt_ref):
    @pl.when(pl.program_id(0) == 0)
    def _():
        cnt_ref[...] = jnp.zeros(cnt_ref.shape, F32)

    oh = sum(_one_hots(idx_ref[...], idx_ref.shape[1]))
    cnt_ref[...] += jnp.sum(oh, axis=1, keepdims=True)


def _dest_kernel(idx_ref, start_ref, dest_ref, carry_ref):
    tm = idx_ref.shape[1]

    @pl.when(pl.program_id(0) == 0)
    def _():
        carry_ref[...] = start_ref[...]

    ohs = _one_hots(idx_ref[...], tm)
    oh = sum(ohs)
    r = lax.broadcasted_iota(I32, (tm, tm), 0)
    c = lax.broadcasted_iota(I32, (tm, tm), 1)
    before = (r < c).astype(BF16)
    rank = _dot(oh.astype(BF16), before) + carry_ref[:, 0:1]
    dest_ref[...] = jnp.concatenate(
        [jnp.sum(o * rank, axis=0, keepdims=True) for o in ohs], axis=0).astype(I32)
    carry_ref[...] += jnp.sum(oh, axis=1, keepdims=True)


def _route(idx, tm):
    k, t = idx.shape
    n = t // tm
    col = lambda i: (0, i)
    cnt = pl.pallas_call(
        _count_kernel,
        grid=(n,),
        in_specs=[pl.BlockSpec((k, tm), col)],
        out_specs=_const_spec((N_EXPERTS, LANES)),
        out_shape=jax.ShapeDtypeStruct((N_EXPERTS, LANES), F32),
        compiler_params=_cparams(("arbitrary",)),
        name="expert_counts",
    )(idx)
    counts = cnt[:, 0].astype(I32)
    padded = (counts + MOE_ROWS - 1) // MOE_ROWS * MOE_ROWS
    pad_end = jnp.cumsum(padded)
    pad_start = pad_end - padded
    start = jnp.broadcast_to(pad_start.astype(F32)[:, None], (N_EXPERTS, LANES))
    dest = pl.pallas_call(
        _dest_kernel,
        grid=(n,),
        in_specs=[pl.BlockSpec((k, tm), col), _const_spec((N_EXPERTS, LANES))],
        out_specs=pl.BlockSpec((k, tm), col),
        out_shape=jax.ShapeDtypeStruct((k, t), I32),
        scratch_shapes=[pltpu.VMEM((N_EXPERTS, LANES), F32)],
        compiler_params=_cparams(("arbitrary",)),
        name="expert_dest_rows",
    )(idx, start)
    return dest, counts, padded, pad_start, pad_end


def _count_le(sorted_vals, v):
    return jnp.sum((sorted_vals[None, :] <= v[:, None]).astype(I32), axis=1)


def _padding_rows(counts, padded, pad_start, pad_end, n_pad):
    per = padded - counts
    cum = jnp.cumsum(per)
    offset = jnp.concatenate([pad_start + counts - (cum - per), pad_end[-1:] - cum[-1:]])
    j = jnp.arange(n_pad, dtype=I32)
    e = _count_le(cum, j)
    pick = (e[:, None] == jnp.arange(N_EXPERTS + 1, dtype=I32)[None, :]).astype(I32)
    return j + jnp.sum(pick * offset[None, :], axis=1)


def _sc_mesh():
    return plsc.VectorSubcoreMesh(core_axis_name="c", subcore_axis_name="s")


def _sc_scatter_rows(src, idx_main, idx_pad, n_out):
    nw, cpk, c = idx_main.shape
    width = src.shape[1]
    cpw = src.shape[0] // (nw * c)
    copies = cpk // cpw
    ppw = idx_pad.shape[1]

    @functools.partial(
        pl.kernel, mesh=_sc_mesh(),
        out_type=jax.ShapeDtypeStruct((n_out, width), src.dtype),
        scratch_types=[pltpu.VMEM((cpk, c), I32), pltpu.VMEM((ppw, c), I32),
                       pltpu.VMEM((c, width), src.dtype), pltpu.SemaphoreType.DMA],
        name="dispatch_rows")
    def run(src_hbm, idx_hbm, pad_hbm, out_hbm, idx_v, pad_v, rows_v, sem):
        wid = lax.axis_index("s") * SC_CORES + lax.axis_index("c")
        pltpu.sync_copy(idx_hbm.at[wid], idx_v)
        pltpu.sync_copy(pad_hbm.at[wid], pad_v)

        @pl.loop(0, cpw)
        def _(j):
            pltpu.sync_copy(src_hbm.at[pl.ds((wid * cpw + j) * c, c)], rows_v)
            cps = [pltpu.async_copy(rows_v, out_hbm.at[idx_v.at[j * copies + kk]], sem)
                   for kk in range(copies)]
            for cp in cps:
                cp.wait()

        n_src_chunks = src.shape[0] // c

        @pl.loop(0, ppw)
        def _(j):
            pltpu.sync_copy(src_hbm.at[pl.ds(((wid * ppw + j) % n_src_chunks) * c, c)], rows_v)
            pltpu.async_copy(rows_v, out_hbm.at[pad_v.at[j]], sem).wait()

    return run(src, idx_main, idx_pad)


def _sc_gather_rows(src, idx):
    nw, cpw, c = idx.shape
    width = src.shape[1]

    @functools.partial(
        pl.kernel, mesh=_sc_mesh(),
        out_type=jax.ShapeDtypeStruct((nw * cpw * c, width), src.dtype),
        scratch_types=[pltpu.VMEM((cpw, c), I32), pltpu.VMEM((c, width), src.dtype),
                       pltpu.SemaphoreType.DMA],
        name="collect_rows")
    def run(src_hbm, idx_hbm, out_hbm, idx_v, rows_v, sem):
        wid = lax.axis_index("s") * SC_CORES + lax.axis_index("c")
        pltpu.sync_copy(idx_hbm.at[wid], idx_v)

        @pl.loop(0, cpw)
        def _(j):
            pltpu.async_copy(src_hbm.at[idx_v.at[j]], rows_v, sem).wait()
            pltpu.sync_copy(rows_v, out_hbm.at[pl.ds((wid * cpw + j) * c, c)])

    return run(src, idx)


def _dispatch(h2, dest, pad_rows, n_rows):
    k, t = dest.shape
    cpw = t // (SC_WORKERS * SC_CHUNK)
    idx_main = dest.reshape(k, SC_WORKERS, cpw, SC_CHUNK).transpose(1, 2, 0, 3)
    idx_main = idx_main.reshape(SC_WORKERS, cpw * k, SC_CHUNK)
    idx_pad = pad_rows.reshape(SC_WORKERS, -1, SC_CHUNK)
    return _sc_scatter_rows(h2, idx_main, idx_pad, n_rows)


def _collect(y, dest):
    k, t = dest.shape
    idx = dest.reshape(SC_WORKERS, (k * t) // (SC_WORKERS * SC_CHUNK), SC_CHUNK)
    return _sc_gather_rows(y, idx).reshape(k, t, y.shape[1])


def _moe_kernel(be_ref, x_ref, wgu_ref, bgu_ref, wd_ref, bd_ref, y_ref):
    in_use = pl.program_id(0) < be_ref[pl.num_programs(0)]

    @pl.when(in_use)
    def _():
        lo, hi = _unpack_bf16_pairs(x_ref[...])
        x = jnp.concatenate([lo, hi], axis=1).astype(BF16)
        gu = _dot(x, wgu_ref[...]) + bgu_ref[...]
        acts = []
        for b in range(gu.shape[1] // GU_BLOCK):
            glu = jnp.minimum(gu[:, b * GU_BLOCK:b * GU_BLOCK + LANES], SWIGLU_LIMIT)
            lin = jnp.clip(gu[:, b * GU_BLOCK + LANES:(b + 1) * GU_BLOCK], -SWIGLU_LIMIT, SWIGLU_LIMIT)
            acts.append(glu * jax.nn.sigmoid(SWIGLU_ALPHA * glu) * (lin + 1.0))
        act = jnp.concatenate(acts, axis=1).astype(BF16)
        y_ref[...] = _pack_bf16_pairs(_dot(act, wd_ref[...].astype(BF16)) + bd_ref[...])

    @pl.when(jnp.logical_not(in_use))
    def _():
        y_ref[...] = jnp.zeros(y_ref.shape, y_ref.dtype)


def _moe_ffn(xs, block_e, w, l):
    n_rows, half = xs.shape
    d = 2 * half
    f2 = w["wgu"].shape[3]
    n_blocks = n_rows // MOE_ROWS
    row = lambda i, be: (i, 0)
    exp = lambda i, be: (l, be[i], 0, 0)
    return pl.pallas_call(
        _moe_kernel,
        grid_spec=pltpu.PrefetchScalarGridSpec(
            num_scalar_prefetch=1,
            grid=(n_blocks,),
            in_specs=[pl.BlockSpec((MOE_ROWS, half), row),
                      pl.BlockSpec((None, None, d, f2), exp), pl.BlockSpec((None, None, 1, f2), exp),
                      pl.BlockSpec((None, None, f2 // 2, d), exp), pl.BlockSpec((None, None, 1, d), exp)],
            out_specs=pl.BlockSpec((MOE_ROWS, half), row)),
        out_shape=jax.ShapeDtypeStruct((n_rows, half), I32),
        compiler_params=_cparams(("arbitrary",)),
        name="expert_ffn",
    )(block_e, xs, w["wgu"], w["bgu"], w["wdown"], w["bdown"])


def _combine_kernel(dn_alpha, yg_ref, wts_ref, x1_ref, mod_ref, lng_ref, lnb_ref, x2_ref):
    wts = wts_ref[...].T
    lo = hi = None
    for k in range(TOP_K):
        l_k, h_k = _unpack_bf16_pairs(yg_ref[k])
        w_k = wts[:, k:k + 1]
        lo = l_k * w_k if lo is None else lo + l_k * w_k
        hi = h_k * w_k if hi is None else hi + h_k * w_k
    ff = jnp.concatenate([lo, hi], axis=1)
    g2 = mod_ref[5:6, :]
    x2_ref[...] = _layer_norm(dn_alpha * x1_ref[...] + (1.0 + g2) * ff, lng_ref[...], lnb_ref[...])


def _combine(yg, wts, x1, mod_l, ln_g, ln_b, g, tm, dn_alpha):
    t, d = x1.shape
    row = lambda i: (i, 0)
    seq = lambda i: (g.seq_of_tile(i, tm), 0, 0)
    return pl.pallas_call(
        functools.partial(_combine_kernel, dn_alpha),
        grid=(t // tm,),
        in_specs=[pl.BlockSpec((TOP_K, tm, d // 2), lambda i: (0, i, 0)),
                  pl.BlockSpec((WTS_ROWS, tm), lambda i: (0, i)),
                  pl.BlockSpec((tm, d), row),
                  pl.BlockSpec((None, mod_l.shape[1], d), seq),
                  _const_spec(ln_g.shape), _const_spec(ln_b.shape)],
        out_specs=pl.BlockSpec((tm, d), row),
        out_shape=jax.ShapeDtypeStruct((t, d), F32),
        compiler_params=_cparams(("parallel",)),
        name="moe_combine",
    )(yg, wts, x1, mod_l, ln_g, ln_b)


def _deinterleave_kernel(w_ref, perm_ref, o_ref):
    w = w_ref[...].astype(BF16)
    for b in range(w.shape[1] // GU_BLOCK):
        cols = slice(b * GU_BLOCK, (b + 1) * GU_BLOCK)
        o_ref[:, cols] = _dot(w[:, cols], perm_ref[...]).astype(BF16)


def _deinterleave_gate_up(w_gate_up):
    depth, ne, d, f2 = w_gate_up.shape
    tn = _pick(f2, (1024, 512, GU_BLOCK))
    j = jnp.arange(GU_BLOCK)
    target = (j % 2) * LANES + j // 2
    perm = (target[:, None] == jnp.arange(GU_BLOCK)[None, :]).astype(BF16)
    out = pl.pallas_call(
        _deinterleave_kernel,
        grid=(depth * ne, f2 // tn),
        in_specs=[pl.BlockSpec((None, d, tn), lambda e, j: (e, 0, j)), _const_spec((GU_BLOCK, GU_BLOCK))],
        out_specs=pl.BlockSpec((None, d, tn), lambda e, j: (e, 0, j)),
        out_shape=jax.ShapeDtypeStruct((depth * ne, d, f2), BF16),
        compiler_params=_cparams(("parallel", "parallel")),
        name="gate_up_columns",
    )(w_gate_up.reshape(depth * ne, d, f2), perm)
    return out.reshape(depth, ne, d, f2)


def _deinterleave_bias(b_gate_up):
    depth, ne, f2 = b_gate_up.shape
    b = b_gate_up.reshape(depth, ne, f2 // GU_BLOCK, LANES, 2).transpose(0, 1, 2, 4, 3)
    return b.reshape(depth, ne, 1, f2)


def _rope_tables(seq_len):
    inv = 1.0 / (ROPE_BASE ** (jnp.arange(0, QK_ROPE, 2, dtype=F32) / QK_ROPE))
    ang = jnp.arange(seq_len, dtype=F32)[:, None] * inv[None, :]
    pad = jnp.zeros((seq_len, LANES - QK_ROPE), F32)
    cos2 = jnp.concatenate([jnp.cos(ang), jnp.cos(ang)], axis=1)
    sin2 = jnp.concatenate([jnp.sin(ang), jnp.sin(ang)], axis=1)
    return (jnp.concatenate([cos2, pad], axis=1), jnp.concatenate([sin2, pad], axis=1), cos2.T, sin2.T)


def _rot_pair(w):
    half = w.shape[-1] // 2
    return jnp.concatenate([-w[..., half:], w[..., :half]], axis=-1)


def _pad_lanes(w):
    return jnp.pad(w, [(0, 0)] * (w.ndim - 1) + [(0, LANES - w.shape[-1])])


def _prep_layer(l, p):
    d = p["w_in"].shape[1]
    w_in = p["w_in"][l]
    i1 = Q_RANK
    i2 = i1 + KV_RANK
    i3 = i2 + QK_ROPE
    i4 = i3 + 2 * SGU_WIDTH
    w_kr = w_in[:, i2:i3]
    w = {}
    w["wa"] = jnp.concatenate([w_in[:, :i2], _pad_lanes(w_kr), _pad_lanes(_rot_pair(w_kr))], axis=1).astype(BF16)
    w["wuv"] = w_in[:, i3:i4].astype(BF16)
    w["wg"] = w_in[:, i4:].astype(BF16)
    w["gq"] = p["q_norm_g"][l][None, :]
    w["gkv"] = p["kv_norm_g"][l][None, :]

    w_uq = p["w_uq"][l].reshape(Q_RANK, N_HEADS, QK_NOPE + QK_ROPE)
    w_ukv = p["w_ukv"][l].reshape(KV_RANK, N_HEADS, QK_NOPE + V_DIM)
    uq_nope = w_uq[:, :, :QK_NOPE].transpose(1, 0, 2)
    uk_t = w_ukv[:, :, :QK_NOPE].transpose(1, 2, 0)
    w["wqlt"] = _bmm(uq_nope, uk_t).transpose(0, 2, 1).reshape(N_HEADS * LAT, Q_RANK).astype(BF16)
    uq_rope = w_uq[:, :, QK_NOPE:]
    w["wqrt"] = uq_rope.reshape(Q_RANK, N_HEADS * QK_ROPE).T.astype(BF16)
    w["wqrst"] = _rot_pair(uq_rope).reshape(Q_RANK, N_HEADS * QK_ROPE).T.astype(BF16)
    uv = w_ukv[:, :, QK_NOPE:].transpose(1, 0, 2)
    wo = p["w_o_attn"][l].reshape(N_HEADS, V_DIM, d)
    w["wfold"] = _bmm(uv, wo).reshape(N_HEADS * LAT, d).astype(BF16)

    w["lng"] = p["sgu_ln_g"][l][None, :]
    w["lnb"] = p["sgu_ln_b"][l][None, :]
    gpl = LANES // (SGU_WIDTH // SGU_GROUPS)
    ws = p["w_s"][l].reshape(SGU_GROUPS // gpl, gpl, CHUNK, CHUNK)
    w["wsp"] = ws.transpose(0, 2, 1, 3).reshape(SGU_GROUPS // gpl, CHUNK, gpl * CHUNK).astype(BF16)
    w["bs"] = jnp.repeat(p["b_s"][l].T, SGU_WIDTH // SGU_GROUPS, axis=1)
    w["wosgu"] = p["w_o_sgu"][l].astype(BF16)
    w["wout"] = p["w_out"][l].astype(BF16)
    w["ln1g"] = p["ln1_g"][l][None, :]
    w["ln1b"] = p["ln1_b"][l][None, :]
    w["rwt"] = p["router_w"][l].T
    w["rb"] = p["router_b"][l][:, None]
    w["ln2g"] = p["ln2_g"][l][None, :]
    w["ln2b"] = p["ln2_b"][l][None, :]
    return w


def _route_and_dispatch(h2, idx):
    t = h2.shape[0]
    n_pad = N_EXPERTS * MOE_ROWS
    n_rows = TOP_K * t + n_pad
    dest, counts, padded, pad_start, pad_end = _route(idx, _pick(t, (512, 256, 128)))
    pad_rows = _padding_rows(counts, padded, pad_start, pad_end, n_pad)
    block_start = jnp.arange(n_rows // MOE_ROWS, dtype=I32) * MOE_ROWS
    block_e = jnp.minimum(_count_le(pad_end, block_start), N_EXPERTS - 1)
    blocks_in_use = pad_end[-1:] // MOE_ROWS
    return _dispatch(h2, dest, pad_rows, n_rows), jnp.concatenate([block_e, blocks_in_use]), dest


def kernel(x_prompt, x_sample, c_prompt, c_sample, w_mod, b_mod, w_in, q_norm_g, kv_norm_g, w_uq, w_ukv, w_o_attn, sgu_ln_g, sgu_ln_b, w_s, b_s, w_o_sgu, w_out, ln1_g, ln1_b, router_w, router_b, w_gate_up, b_gate_up, w_down, b_down, ln2_g, ln2_b):
    p = dict(w_in=w_in, q_norm_g=q_norm_g, kv_norm_g=kv_norm_g, w_uq=w_uq, w_ukv=w_ukv, w_o_attn=w_o_attn,
             sgu_ln_g=sgu_ln_g, sgu_ln_b=sgu_ln_b, w_s=w_s, b_s=b_s, w_o_sgu=w_o_sgu, w_out=w_out,
             ln1_g=ln1_g, ln1_b=ln1_b, router_w=router_w, router_b=router_b, ln2_g=ln2_g, ln2_b=ln2_b)
    depth = w_mod.shape[0]
    d = x_prompt.shape[-1]
    dn_alpha = (2 * depth) ** 0.25
    groups = [_Seqs(x_prompt.shape[0], x_prompt.shape[1], 0),
              _Seqs(x_sample.shape[0], x_sample.shape[1], x_prompt.shape[0])]
    xs = [x_prompt.reshape(-1, d), x_sample.reshape(-1, d)]
    for g in groups:
        assert g.t % (SC_WORKERS * SC_CHUNK) == 0
    assert (N_EXPERTS * MOE_ROWS) % (SC_WORKERS * SC_CHUNK) == 0

    n_seq = sum(g.b for g in groups)
    ns_pad = -(-n_seq // 8) * 8
    c_all = jnp.pad(jnp.concatenate([c_prompt, c_sample], axis=0), ((0, ns_pad - n_seq), (0, 0)))
    mod = _modulation(c_all, w_mod, b_mod)
    rope = _rope_tables(max(g.s for g in groups))
    moe_w = dict(wgu=_deinterleave_gate_up(w_gate_up), bgu=_deinterleave_bias(b_gate_up),
                 wdown=w_down, bdown=b_down[:, :, None, :])

    def tiles(g):
        return (_pick(g.s, (512, 256, 128)), _pick(g.s, (1024, 512, 256, 128)), _pick(g.s, (2048, 1024, 512, 256)))

    for l in range(depth):
        w = dict(_prep_layer(l, p), **moe_w)
        mixer_in = [_inproj(x, mod[l], rope, w, g, tiles(g)[0]) for x, g in zip(xs, groups)]
        routed = []
        for x, g, (qt, klat, vt, sgu, sg) in zip(xs, groups, mixer_in):
            tm, tq, tk = tiles(g)
            o = _attention(qt, klat, vt, g.b, g.s, tq, tk)
            x1, h2, idx, wts = _outproj(o, sgu, sg, x, mod[l], w, g, tm, dn_alpha)
            routed.append((x1, wts) + _route_and_dispatch(h2, idx))
        collected = [(x1, wts, _collect(_moe_ffn(rows, block_e, w, l), dest))
                     for x1, wts, rows, block_e, dest in routed]
        xs = [_combine(yg, wts, x1, mod[l], w["ln2g"], w["ln2b"], g, tiles(g)[0], dn_alpha)
              for (x1, wts, yg), g in zip(collected, groups)]

    return tuple(x.reshape(g.b, g.s, d) for x, g in zip(xs, groups))
```

```python
import functools
import math

import jax
import jax.numpy as jnp
from jax import lax
from jax.experimental import pallas as pl
from jax.experimental.pallas import tpu as pltpu
from jax.experimental.pallas import tpu_sc as plsc

F32 = jnp.float32
BF16 = jnp.bfloat16
I32 = jnp.int32

N_HEADS = 8
QK_NOPE = 64
QK_ROPE = 32
V_DIM = 64
Q_RANK = 256
KV_RANK = 128
ROPE_BASE = 10000.0
CHUNK = 128
SGU_WIDTH = 512
SGU_GROUPS = 8
N_EXPERTS = 32
TOP_K = 4
SWIGLU_LIMIT = 7.0
SWIGLU_ALPHA = 1.702
LN_EPS = 1e-5
RMS_EPS = 1e-6
ATTN_SCALE = 1.0 / math.sqrt(QK_NOPE + QK_ROPE)
Q_SCALE = ATTN_SCALE * math.log2(math.e)

LANES = 128
LAT = KV_RANK
QK_PAD = 2 * LANES
V_ROWS = LAT + 16
KEY_CHUNK = 256
WTS_ROWS = 8
MOE_ROWS = 512
GU_BLOCK = 2 * LANES
VMEM_LIMIT = 56 * 1024 * 1024
SC_CORES = 2
SC_SUBCORES = 16
SC_WORKERS = SC_CORES * SC_SUBCORES
SC_CHUNK = 64


def _cparams(sem, vmem=VMEM_LIMIT):
    return pltpu.CompilerParams(dimension_semantics=sem, vmem_limit_bytes=vmem)


def _pick(n, prefs):
    for p in prefs:
        if n % p == 0:
            return p
    raise ValueError(f"no tile in {prefs} divides {n}")


def _const_spec(shape):
    nd = len(shape)
    return pl.BlockSpec(shape, lambda *_: (0,) * nd)


def _dot(a, b):
    return jnp.dot(a, b, preferred_element_type=F32)


def _layer_norm(y, g, b):
    mu = jnp.mean(y, axis=-1, keepdims=True)
    d = y - mu
    var = jnp.mean(d * d, axis=-1, keepdims=True)
    return d * lax.rsqrt(var + LN_EPS) * g + b


def _rms_norm(y, g):
    return y * lax.rsqrt(jnp.mean(y * y, axis=-1, keepdims=True) + RMS_EPS) * g


HIGH_HALF = -65536


def _pack_bf16_pairs(y):
    w = y.shape[1] // 2
    bits = lax.bitcast_convert_type(y.astype(BF16).astype(F32), I32)
    return (bits[:, w:] & HIGH_HALF) | lax.shift_right_logical(bits[:, :w], 16)


def _unpack_bf16_pairs(p):
    lo = lax.bitcast_convert_type(lax.shift_left(p, 16), F32)
    hi = lax.bitcast_convert_type(p & HIGH_HALF, F32)
    return lo, hi


class _Seqs:
    def __init__(self, b, s, seq0):
        self.b, self.s, self.seq0 = b, s, seq0
        self.t = b * s

    def seq_of_tile(self, i, tm):
        return self.seq0 + (i * tm) // self.s

    def pos_tile(self, i, tm):
        return i % (self.s // tm)


def _mod_kernel(c_ref, w_ref, b_ref, o_ref):
    c = c_ref[...]
    a = c * jax.nn.sigmoid(c)
    o_ref[0] = jnp.dot(a, w_ref[0], preferred_element_type=F32,
                       precision=lax.Precision.HIGHEST) + b_ref[0]


def _modulation(c_all, w_mod, b_mod):
    depth, d, six_d = w_mod.shape
    ns = c_all.shape[0]
    n_col = six_d // d
    out = pl.pallas_call(
        _mod_kernel,
        grid=(depth, n_col),
        in_specs=[
            pl.BlockSpec((ns, d), lambda l, j: (0, 0)),
            pl.BlockSpec((1, d, d), lambda l, j: (l, 0, j)),
            pl.BlockSpec((1, 1, d), lambda l, j: (l, 0, j)),
        ],
        out_specs=pl.BlockSpec((1, ns, d), lambda l, j: (l, 0, j)),
        out_shape=jax.ShapeDtypeStruct((depth, ns, six_d), F32),
        compiler_params=_cparams(("parallel", "parallel")),
        name="modulation",
    )(c_all, w_mod, b_mod.reshape(depth, 1, six_d))
    return out.reshape(depth, ns, n_col, d)


def _bmm_kernel(a_ref, b_ref, o_ref):
    o_ref[0] = jnp.dot(a_ref[0], b_ref[0], preferred_element_type=F32,
                       precision=lax.Precision.HIGHEST)


def _bmm(a, b):
    h, m, k = a.shape
    n = b.shape[2]
    return pl.pallas_call(
        _bmm_kernel,
        grid=(h,),
        in_specs=[pl.BlockSpec((1, m, k), lambda i: (i, 0, 0)),
                  pl.BlockSpec((1, k, n), lambda i: (i, 0, 0))],
        out_specs=pl.BlockSpec((1, m, n), lambda i: (i, 0, 0)),
        out_shape=jax.ShapeDtypeStruct((h, m, n), F32),
        compiler_params=_cparams(("parallel",)),
        name="weight_fold",
    )(a, b)


def _gelu_tanh(x):
    c = math.sqrt(2.0 / math.pi)
    return 0.5 * x * (1.0 + jnp.tanh(c * (x + 0.044715 * (x * x * x))))


def _inproj_kernel(x_ref, mod_ref, cos_ref, sin_ref, cost_ref, sint_ref, wa_ref, wuv_ref, wg_ref, gq_ref,
                   gkv_ref, wqlt_ref, wqrt_ref, wqrst_ref, lng_ref, lnb_ref, wsp_ref, bs_ref,
                   q_ref, klat_ref, vt_ref, sgu_ref, sg_ref):
    tm = x_ref.shape[0]
    x = x_ref[...]
    sh1 = mod_ref[0:1, :]
    sc1 = mod_ref[1:2, :]
    h = (x * (1.0 + sc1) + sh1).astype(BF16)

    za = _dot(h, wa_ref[...])
    cq = za[:, 0:Q_RANK]
    ckv = za[:, Q_RANK:Q_RANK + KV_RANK]
    kr = za[:, Q_RANK + KV_RANK:Q_RANK + KV_RANK + LANES]
    krs = za[:, Q_RANK + KV_RANK + LANES:Q_RANK + KV_RANK + 2 * LANES]
    cos = cos_ref[...]
    sin = sin_ref[...]
    cqn = _rms_norm(cq, gq_ref[...]).astype(BF16)
    ckvn = _rms_norm(ckv, gkv_ref[...])
    klat_ref[:, 0:LAT] = ckvn.astype(BF16)
    klat_ref[:, LAT:QK_PAD] = (kr * cos + krs * sin).astype(BF16)
    vt_ref[0:LAT, :] = ckvn.T.astype(BF16)
    extra = lax.broadcasted_iota(I32, (V_ROWS - LAT, tm), 0)
    vt_ref[LAT:V_ROWS, :] = jnp.where(extra == 0, 1.0, 0.0).astype(BF16)

    nt = (((1,), (1,)), ((), ()))
    qlt = lax.dot_general(wqlt_ref[...], cqn, nt, preferred_element_type=F32)
    qrt = lax.dot_general(wqrt_ref[...], cqn, nt, preferred_element_type=F32)
    qrst = lax.dot_general(wqrst_ref[...], cqn, nt, preferred_element_type=F32)
    cos_q = cost_ref[...] * Q_SCALE
    sin_q = sint_ref[...] * Q_SCALE
    pad = jnp.zeros((QK_PAD - LAT - QK_ROPE, tm), BF16)
    for hd in range(N_HEADS):
        q_ref[hd, 0:LAT, :] = (qlt[hd * LAT:(hd + 1) * LAT, :] * Q_SCALE).astype(BF16)
        rows = slice(hd * QK_ROPE, (hd + 1) * QK_ROPE)
        q_ref[hd, LAT:LAT + QK_ROPE, :] = (qrt[rows, :] * cos_q + qrst[rows, :] * sin_q).astype(BF16)
        q_ref[hd, LAT + QK_ROPE:QK_PAD, :] = pad

    sg_ref[...] = jax.nn.sigmoid(_dot(h, wg_ref[...])).astype(BF16)

    guv = _gelu_tanh(_dot(h, wuv_ref[...]))
    u = guv[:, :SGU_WIDTH]
    vn = _layer_norm(guv[:, SGU_WIDTH:], lng_ref[...], lnb_ref[...]).astype(BF16)
    lane = lax.broadcasted_iota(I32, (CHUNK, LANES), 1)
    low_half = lane < (LANES // 2)
    zero = jnp.zeros((CHUNK, LANES), BF16)
    for c in range(tm // CHUNK):
        rows = slice(c * CHUNK, (c + 1) * CHUNK)
        vc = vn[rows, :]
        mixed = []
        for j in range(SGU_WIDTH // LANES):
            blk = vc[:, j * LANES:(j + 1) * LANES]
            rhs = jnp.concatenate([jnp.where(low_half, blk, zero), jnp.where(low_half, zero, blk)], axis=0)
            mixed.append(_dot(wsp_ref[j], rhs))
        mixed = jnp.concatenate(mixed, axis=1) + bs_ref[...]
        sgu_ref[rows, :] = (u[rows, :] * mixed).astype(BF16)


def _inproj(x, mod_l, rope, w, g, tm):
    t, d = x.shape
    n = t // tm
    seq = lambda i: (g.seq_of_tile(i, tm), 0, 0)
    pos = lambda i: (g.pos_tile(i, tm), 0)
    pos_t = lambda i: (0, g.pos_tile(i, tm))
    row = lambda i: (i, 0)
    col = lambda i: (0, i)
    cos_t, sin_t, cos_tt, sin_tt = rope
    consts = [w["wa"], w["wuv"], w["wg"], w["gq"], w["gkv"], w["wqlt"], w["wqrt"], w["wqrst"],
              w["lng"], w["lnb"], w["wsp"], w["bs"]]
    return pl.pallas_call(
        _inproj_kernel,
        grid=(n,),
        in_specs=[pl.BlockSpec((tm, d), row),
                  pl.BlockSpec((None, mod_l.shape[1], d), seq),
                  pl.BlockSpec((tm, LANES), pos),
                  pl.BlockSpec((tm, LANES), pos),
                  pl.BlockSpec((QK_ROPE, tm), pos_t),
                  pl.BlockSpec((QK_ROPE, tm), pos_t)] + [_const_spec(c.shape) for c in consts],
        out_specs=[pl.BlockSpec((N_HEADS, QK_PAD, tm), lambda i: (0, 0, i)),
                   pl.BlockSpec((tm, QK_PAD), row),
                   pl.BlockSpec((V_ROWS, tm), col),
                   pl.BlockSpec((tm, SGU_WIDTH), row),
                   pl.BlockSpec((tm, 2 * d), row)],
        out_shape=[jax.ShapeDtypeStruct((N_HEADS, QK_PAD, t), BF16),
                   jax.ShapeDtypeStruct((t, QK_PAD), BF16),
                   jax.ShapeDtypeStruct((V_ROWS, t), BF16),
                   jax.ShapeDtypeStruct((t, SGU_WIDTH), BF16),
                   jax.ShapeDtypeStruct((t, 2 * d), BF16)],
        compiler_params=_cparams(("parallel",)),
        name="token_mixer_in",
    )(x, mod_l, cos_t, sin_t, cos_tt, sin_tt, *consts)


def _attn_kernel(qt_ref, k_ref, vt_ref, o_ref, m_ref, acc_ref, s_ref):
    ki = pl.program_id(2)
    nh = qt_ref.shape[0]

    @pl.when(ki == 0)
    def _():
        m_ref[...] = jnp.full(m_ref.shape, -jnp.inf, F32)
        acc_ref[...] = jnp.zeros(acc_ref.shape, F32)

    k = k_ref[...]
    tk = k.shape[0]
    def scores(hd):
        st = _dot(k, qt_ref[hd])
        s_ref[hd % 2] = st
        return jnp.max(st, axis=0, keepdims=True)

    mx_next = scores(0)
    for hd in range(nh):
        mx = mx_next
        if hd + 1 < nh:
            mx_next = scores(hd + 1)
        m_prev = m_ref[hd:hd + 1, :]
        m_new = jnp.maximum(m_prev, mx)
        alpha = jnp.exp2(m_prev - m_new)
        pv = None
        for kt in range(tk // KEY_CHUNK):
            keys = slice(kt * KEY_CHUNK, (kt + 1) * KEY_CHUNK)
            p = jnp.exp2(s_ref[hd % 2, keys, :] - m_new).astype(BF16)
            part = _dot(vt_ref[:, keys], p)
            pv = part if pv is None else pv + part
        acc_ref[hd] = alpha * acc_ref[hd] + pv
        m_ref[hd:hd + 1, :] = m_new

    @pl.when(ki == pl.num_programs(2) - 1)
    def _():
        for hd in range(nh):
            a = acc_ref[hd]
            o = a[0:LAT, :] / a[LAT:LAT + 1, :]
            o_ref[:, hd * LAT:(hd + 1) * LAT] = o.T.astype(BF16)


def _attention(qt, klat, vt, nb, s, tq, tk):
    nq, nk = s // tq, s // tk
    return pl.pallas_call(
        _attn_kernel,
        grid=(nb, nq, nk),
        in_specs=[pl.BlockSpec((N_HEADS, QK_PAD, tq), lambda b, i, j: (0, 0, b * nq + i)),
                  pl.BlockSpec((tk, QK_PAD), lambda b, i, j: (b * nk + j, 0)),
                  pl.BlockSpec((V_ROWS, tk), lambda b, i, j: (0, b * nk + j))],
        out_specs=pl.BlockSpec((tq, N_HEADS * LAT), lambda b, i, j: (b * nq + i, 0)),
        out_shape=jax.ShapeDtypeStruct((nb * s, N_HEADS * LAT), BF16),
        scratch_shapes=[pltpu.VMEM((N_HEADS, tq), F32), pltpu.VMEM((N_HEADS, V_ROWS, tq), F32),
                        pltpu.VMEM((2, tk, tq), F32)],
        compiler_params=_cparams(("parallel", "parallel", "arbitrary")),
        name="latent_attention",
    )(qt, klat, vt)


def _outproj_kernel(dn_alpha, o_ref, sgu_ref, sg_ref, x_ref, mod_ref, wfold_ref, wosgu_ref, wout_ref,
                    lng_ref, lnb_ref, rwt_ref, rb_ref, x1_ref, h2_ref, idx_ref, wts_ref, cnt_ref):
    d = x_ref.shape[1]

    @pl.when(pl.program_id(0) == 0)
    def _():
        cnt_ref[...] = jnp.zeros(cnt_ref.shape, F32)

    o_attn = _dot(o_ref[...], wfold_ref[...])
    o_sgu = _dot(sgu_ref[...], wosgu_ref[...])
    sg = sg_ref[...].astype(F32)
    merged = (sg[:, :d] * o_attn + sg[:, d:] * o_sgu).astype(BF16)
    mix = _dot(merged, wout_ref[...])
    g1 = mod_ref[2:3, :]
    sh2 = mod_ref[3:4, :]
    sc2 = mod_ref[4:5, :]
    x1 = _layer_norm(dn_alpha * x_ref[...] + (1.0 + g1) * mix, lng_ref[...], lnb_ref[...])
    x1_ref[...] = x1
    h2 = x1 * (1.0 + sc2) + sh2
    h2_ref[...] = _pack_bf16_pairs(h2)

    nt = (((1,), (1,)), ((), ()))
    h_hi = h2.astype(BF16)
    h_lo = (h2 - h_hi.astype(F32)).astype(BF16)
    rw = rwt_ref[...]
    w_hi = rw.astype(BF16)
    w_lo = (rw - w_hi.astype(F32)).astype(BF16)
    logits = (lax.dot_general(w_hi, h_hi, nt, preferred_element_type=F32)
              + lax.dot_general(w_lo, h_hi, nt, preferred_element_type=F32)
              + lax.dot_general(w_hi, h_lo, nt, preferred_element_type=F32)) + rb_ref[...]
    e_iota = lax.broadcasted_iota(I32, logits.shape, 0)
    vals, idxs = [], []
    chosen = jnp.zeros(logits.shape, F32)
    for _ in range(TOP_K):
        mx = jnp.max(logits, axis=0, keepdims=True)
        ix = jnp.min(jnp.where(logits == mx, e_iota, N_EXPERTS), axis=0, keepdims=True)
        vals.append(mx)
        idxs.append(ix)
        chosen = jnp.where(e_iota == ix, 1.0, chosen)
        logits = jnp.where(e_iota == ix, -jnp.inf, logits)
    cnt_ref[...] += jnp.sum(chosen, axis=1, keepdims=True)
    v = jnp.concatenate(vals, axis=0)
    w = jnp.exp(v - v[0:1, :])
    wn = w / jnp.sum(w, axis=0, keepdims=True)
    wts_ref[...] = jnp.concatenate([wn, jnp.zeros((WTS_ROWS - TOP_K, wn.shape[1]), F32)], axis=0)
    idx_ref[...] = jnp.concatenate(idxs, axis=0)


def _outproj(o, sgu, sg, x, mod_l, w, g, tm, dn_alpha):
    t, d = x.shape
    n = t // tm
    seq = lambda i: (g.seq_of_tile(i, tm), 0, 0)
    row = lambda i: (i, 0)
    col = lambda i: (0, i)
    consts = [w["wfold"], w["wosgu"], w["wout"], w["ln1g"], w["ln1b"], w["rwt"], w["rb"]]
    return pl.pallas_call(
        functools.partial(_outproj_kernel, dn_alpha),
        grid=(n,),
        in_specs=[pl.BlockSpec((tm, N_HEADS * LAT), row),
                  pl.BlockSpec((tm, SGU_WIDTH), row),
                  pl.BlockSpec((tm, 2 * d), row),
                  pl.BlockSpec((tm, d), row),
                  pl.BlockSpec((None, mod_l.shape[1], d), seq)] + [_const_spec(c.shape) for c in consts],
        out_specs=[pl.BlockSpec((tm, d), row),
                   pl.BlockSpec((tm, d // 2), row),
                   pl.BlockSpec((TOP_K, tm), col),
                   pl.BlockSpec((WTS_ROWS, tm), col),
                   _const_spec((N_EXPERTS, LANES))],
        out_shape=[jax.ShapeDtypeStruct((t, d), F32),
                   jax.ShapeDtypeStruct((t, d // 2), I32),
                   jax.ShapeDtypeStruct((TOP_K, t), I32),
                   jax.ShapeDtypeStruct((WTS_ROWS, t), F32),
                   jax.ShapeDtypeStruct((N_EXPERTS, LANES), F32)],
        compiler_params=_cparams(("arbitrary",)),
        name="token_mixer_out",
    )(o, sgu, sg, x, mod_l, *consts)


def _one_hots(idx, tm):
    e_iota = lax.broadcasted_iota(I32, (N_EXPERTS, tm), 0)
    return [(e_iota == idx[k:k + 1, :]).astype(F32) for k in range(TOP_K)]


def _dest_kernel(idx_ref, start_ref, dest_ref, carry_ref):
    tm = idx_ref.shape[1]

    @pl.when(pl.program_id(0) == 0)
    def _():
        carry_ref[...] = start_ref[...]

    ohs = _one_hots(idx_ref[...], tm)
    oh = sum(ohs)
    r = lax.broadcasted_iota(I32, (tm, tm), 0)
    c = lax.broadcasted_iota(I32, (tm, tm), 1)
    before = (r < c).astype(BF16)
    rank = _dot(oh.astype(BF16), before) + carry_ref[:, 0:1]
    dest_ref[...] = jnp.concatenate(
        [jnp.sum(o * rank, axis=0, keepdims=True) for o in ohs], axis=0).astype(I32)
    carry_ref[...] += jnp.sum(oh, axis=1, keepdims=True)


def _route(idx, cnt, tm):
    k, t = idx.shape
    n = t // tm
    col = lambda i: (0, i)
    counts = cnt[:, 0].astype(I32)
    padded = (counts + MOE_ROWS - 1) // MOE_ROWS * MOE_ROWS
    pad_end = jnp.cumsum(padded)
    pad_start = pad_end - padded
    start = jnp.broadcast_to(pad_start.astype(F32)[:, None], (N_EXPERTS, LANES))
    dest = pl.pallas_call(
        _dest_kernel,
        grid=(n,),
        in_specs=[pl.BlockSpec((k, tm), col), _const_spec((N_EXPERTS, LANES))],
        out_specs=pl.BlockSpec((k, tm), col),
        out_shape=jax.ShapeDtypeStruct((k, t), I32),
        scratch_shapes=[pltpu.VMEM((N_EXPERTS, LANES), F32)],
        compiler_params=_cparams(("arbitrary",)),
        name="expert_dest_rows",
    )(idx, start)
    return dest, counts, padded, pad_start, pad_end


def _count_le(sorted_vals, v):
    return jnp.sum((sorted_vals[None, :] <= v[:, None]).astype(I32), axis=1)


def _padding_rows(counts, padded, pad_start, pad_end, n_pad):
    per = padded - counts
    cum = jnp.cumsum(per)
    offset = jnp.concatenate([pad_start + counts - (cum - per), pad_end[-1:] - cum[-1:]])
    j = jnp.arange(n_pad, dtype=I32)
    e = _count_le(cum, j)
    pick = (e[:, None] == jnp.arange(N_EXPERTS + 1, dtype=I32)[None, :]).astype(I32)
    return j + jnp.sum(pick * offset[None, :], axis=1)


def _sc_mesh():
    return plsc.VectorSubcoreMesh(core_axis_name="c", subcore_axis_name="s")


def _sc_scatter_rows(src, idx_main, idx_pad, n_out):
    nw, cpk, c = idx_main.shape
    width = src.shape[1]
    cpw = src.shape[0] // (nw * c)
    copies = cpk // cpw
    ppw = idx_pad.shape[1]

    @functools.partial(
        pl.kernel, mesh=_sc_mesh(),
        out_type=jax.ShapeDtypeStruct((n_out, width), src.dtype),
        scratch_types=[pltpu.VMEM((cpk, c), I32), pltpu.VMEM((ppw, c), I32),
                       pltpu.VMEM((c, width), src.dtype), pltpu.SemaphoreType.DMA],
        name="dispatch_rows")
    def run(src_hbm, idx_hbm, pad_hbm, out_hbm, idx_v, pad_v, rows_v, sem):
        wid = lax.axis_index("s") * SC_CORES + lax.axis_index("c")
        pltpu.sync_copy(idx_hbm.at[wid], idx_v)
        pltpu.sync_copy(pad_hbm.at[wid], pad_v)

        @pl.loop(0, cpw)
        def _(j):
            pltpu.sync_copy(src_hbm.at[pl.ds((wid * cpw + j) * c, c)], rows_v)
            cps = [pltpu.async_copy(rows_v, out_hbm.at[idx_v.at[j * copies + kk]], sem)
                   for kk in range(copies)]
            for cp in cps:
                cp.wait()

        n_src_chunks = src.shape[0] // c

        @pl.loop(0, ppw)
        def _(j):
            pltpu.sync_copy(src_hbm.at[pl.ds(((wid * ppw + j) % n_src_chunks) * c, c)], rows_v)
            pltpu.async_copy(rows_v, out_hbm.at[pad_v.at[j]], sem).wait()

    return run(src, idx_main, idx_pad)


def _sc_gather_rows(src, idx):
    nw, cpw, c = idx.shape
    width = src.shape[1]

    @functools.partial(
        pl.kernel, mesh=_sc_mesh(),
        out_type=jax.ShapeDtypeStruct((nw * cpw * c, width), src.dtype),
        scratch_types=[pltpu.VMEM((cpw, c), I32), pltpu.VMEM((c, width), src.dtype),
                       pltpu.SemaphoreType.DMA],
        name="collect_rows")
    def run(src_hbm, idx_hbm, out_hbm, idx_v, rows_v, sem):
        wid = lax.axis_index("s") * SC_CORES + lax.axis_index("c")
        pltpu.sync_copy(idx_hbm.at[wid], idx_v)

        @pl.loop(0, cpw)
        def _(j):
            pltpu.async_copy(src_hbm.at[idx_v.at[j]], rows_v, sem).wait()
            pltpu.sync_copy(rows_v, out_hbm.at[pl.ds((wid * cpw + j) * c, c)])

    return run(src, idx)


def _dispatch(h2, dest, pad_rows, n_rows):
    k, t = dest.shape
    cpw = t // (SC_WORKERS * SC_CHUNK)
    idx_main = dest.reshape(k, SC_WORKERS, cpw, SC_CHUNK).transpose(1, 2, 0, 3)
    idx_main = idx_main.reshape(SC_WORKERS, cpw * k, SC_CHUNK)
    idx_pad = pad_rows.reshape(SC_WORKERS, -1, SC_CHUNK)
    return _sc_scatter_rows(h2, idx_main, idx_pad, n_rows)


def _collect(y, dest):
    k, t = dest.shape
    idx = dest.reshape(SC_WORKERS, (k * t) // (SC_WORKERS * SC_CHUNK), SC_CHUNK)
    return _sc_gather_rows(y, idx).reshape(k, t, y.shape[1])


def _moe_kernel(be_ref, x_ref, wgu_ref, bgu_ref, wd_ref, bd_ref, y_ref):
    in_use = pl.program_id(0) < be_ref[pl.num_programs(0)]

    @pl.when(in_use)
    def _():
        lo, hi = _unpack_bf16_pairs(x_ref[...])
        x = jnp.concatenate([lo, hi], axis=1).astype(BF16)
        gu = _dot(x, wgu_ref[...]) + bgu_ref[...]
        acts = []
        for b in range(gu.shape[1] // GU_BLOCK):
            glu = jnp.minimum(gu[:, b * GU_BLOCK:b * GU_BLOCK + LANES], SWIGLU_LIMIT)
            lin = jnp.clip(gu[:, b * GU_BLOCK + LANES:(b + 1) * GU_BLOCK], -SWIGLU_LIMIT, SWIGLU_LIMIT)
            acts.append(glu * jax.nn.sigmoid(SWIGLU_ALPHA * glu) * (lin + 1.0))
        act = jnp.concatenate(acts, axis=1).astype(BF16)
        y_ref[...] = _pack_bf16_pairs(_dot(act, wd_ref[...].astype(BF16)) + bd_ref[...])

    @pl.when(jnp.logical_not(in_use))
    def _():
        y_ref[...] = jnp.zeros(y_ref.shape, y_ref.dtype)


def _moe_ffn(xs, block_e, w, l):
    n_rows, half = xs.shape
    d = 2 * half
    f2 = w["wgu"].shape[3]
    n_blocks = n_rows // MOE_ROWS
    row = lambda i, be: (i, 0)
    exp = lambda i, be: (l, be[i], 0, 0)
    return pl.pallas_call(
        _moe_kernel,
        grid_spec=pltpu.PrefetchScalarGridSpec(
            num_scalar_prefetch=1,
            grid=(n_blocks,),
            in_specs=[pl.BlockSpec((MOE_ROWS, half), row),
                      pl.BlockSpec((None, None, d, f2), exp), pl.BlockSpec((None, None, 1, f2), exp),
                      pl.BlockSpec((None, None, f2 // 2, d), exp), pl.BlockSpec((None, None, 1, d), exp)],
            out_specs=pl.BlockSpec((MOE_ROWS, half), row)),
        out_shape=jax.ShapeDtypeStruct((n_rows, half), I32),
        compiler_params=_cparams(("arbitrary",)),
        name="expert_ffn",
    )(block_e, xs, w["wgu"], w["bgu"], w["wdown"], w["bdown"])


def _combine_kernel(dn_alpha, yg_ref, wts_ref, x1_ref, mod_ref, lng_ref, lnb_ref, x2_ref):
    wts = wts_ref[...].T
    lo = hi = None
    for k in range(TOP_K):
        l_k, h_k = _unpack_bf16_pairs(yg_ref[k])
        w_k = wts[:, k:k + 1]
        lo = l_k * w_k if lo is None else lo + l_k * w_k
        hi = h_k * w_k if hi is None else hi + h_k * w_k
    ff = jnp.concatenate([lo, hi], axis=1)
    g2 = mod_ref[5:6, :]
    x2_ref[...] = _layer_norm(dn_alpha * x1_ref[...] + (1.0 + g2) * ff, lng_ref[...], lnb_ref[...])


def _combine(yg, wts, x1, mod_l, ln_g, ln_b, g, tm, dn_alpha):
    t, d = x1.shape
    row = lambda i: (i, 0)
    seq = lambda i: (g.seq_of_tile(i, tm), 0, 0)
    return pl.pallas_call(
        functools.partial(_combine_kernel, dn_alpha),
        grid=(t // tm,),
        in_specs=[pl.BlockSpec((TOP_K, tm, d // 2), lambda i: (0, i, 0)),
                  pl.BlockSpec((WTS_ROWS, tm), lambda i: (0, i)),
                  pl.BlockSpec((tm, d), row),
                  pl.BlockSpec((None, mod_l.shape[1], d), seq),
                  _const_spec(ln_g.shape), _const_spec(ln_b.shape)],
        out_specs=pl.BlockSpec((tm, d), row),
        out_shape=jax.ShapeDtypeStruct((t, d), F32),
        compiler_params=_cparams(("parallel",)),
        name="moe_combine",
    )(yg, wts, x1, mod_l, ln_g, ln_b)


def _deinterleave_kernel(w_ref, perm_ref, o_ref):
    w = w_ref[...].astype(BF16)
    for b in range(w.shape[1] // GU_BLOCK):
        cols = slice(b * GU_BLOCK, (b + 1) * GU_BLOCK)
        o_ref[:, cols] = _dot(w[:, cols], perm_ref[...]).astype(BF16)


def _deinterleave_gate_up(w_gate_up):
    depth, ne, d, f2 = w_gate_up.shape
    tn = _pick(f2, (1024, 512, GU_BLOCK))
    j = jnp.arange(GU_BLOCK)
    target = (j % 2) * LANES + j // 2
    perm = (target[:, None] == jnp.arange(GU_BLOCK)[None, :]).astype(BF16)
    out = pl.pallas_call(
        _deinterleave_kernel,
        grid=(depth * ne, f2 // tn),
        in_specs=[pl.BlockSpec((None, d, tn), lambda e, j: (e, 0, j)), _const_spec((GU_BLOCK, GU_BLOCK))],
        out_specs=pl.BlockSpec((None, d, tn), lambda e, j: (e, 0, j)),
        out_shape=jax.ShapeDtypeStruct((depth * ne, d, f2), BF16),
        compiler_params=_cparams(("parallel", "parallel")),
        name="gate_up_columns",
    )(w_gate_up.reshape(depth * ne, d, f2), perm)
    return out.reshape(depth, ne, d, f2)


def _deinterleave_bias(b_gate_up):
    depth, ne, f2 = b_gate_up.shape
    b = b_gate_up.reshape(depth, ne, f2 // GU_BLOCK, LANES, 2).transpose(0, 1, 2, 4, 3)
    return b.reshape(depth, ne, 1, f2)


def _rope_tables(seq_len):
    inv = 1.0 / (ROPE_BASE ** (jnp.arange(0, QK_ROPE, 2, dtype=F32) / QK_ROPE))
    ang = jnp.arange(seq_len, dtype=F32)[:, None] * inv[None, :]
    pad = jnp.zeros((seq_len, LANES - QK_ROPE), F32)
    cos2 = jnp.concatenate([jnp.cos(ang), jnp.cos(ang)], axis=1)
    sin2 = jnp.concatenate([jnp.sin(ang), jnp.sin(ang)], axis=1)
    return (jnp.concatenate([cos2, pad], axis=1), jnp.concatenate([sin2, pad], axis=1), cos2.T, sin2.T)


def _rot_pair(w):
    half = w.shape[-1] // 2
    return jnp.concatenate([-w[..., half:], w[..., :half]], axis=-1)


def _pad_lanes(w):
    return jnp.pad(w, [(0, 0)] * (w.ndim - 1) + [(0, LANES - w.shape[-1])])


def _prep_layer(l, p):
    d = p["w_in"].shape[1]
    w_in = p["w_in"][l]
    i1 = Q_RANK
    i2 = i1 + KV_RANK
    i3 = i2 + QK_ROPE
    i4 = i3 + 2 * SGU_WIDTH
    w_kr = w_in[:, i2:i3]
    w = {}
    w["wa"] = jnp.concatenate([w_in[:, :i2], _pad_lanes(w_kr), _pad_lanes(_rot_pair(w_kr))], axis=1).astype(BF16)
    w["wuv"] = w_in[:, i3:i4].astype(BF16)
    w["wg"] = w_in[:, i4:].astype(BF16)
    w["gq"] = p["q_norm_g"][l][None, :]
    w["gkv"] = p["kv_norm_g"][l][None, :]

    w_uq = p["w_uq"][l].reshape(Q_RANK, N_HEADS, QK_NOPE + QK_ROPE)
    w_ukv = p["w_ukv"][l].reshape(KV_RANK, N_HEADS, QK_NOPE + V_DIM)
    uq_nope = w_uq[:, :, :QK_NOPE].transpose(1, 0, 2)
    uk_t = w_ukv[:, :, :QK_NOPE].transpose(1, 2, 0)
    w["wqlt"] = _bmm(uq_nope, uk_t).transpose(0, 2, 1).reshape(N_HEADS * LAT, Q_RANK).astype(BF16)
    uq_rope = w_uq[:, :, QK_NOPE:]
    w["wqrt"] = uq_rope.reshape(Q_RANK, N_HEADS * QK_ROPE).T.astype(BF16)
    w["wqrst"] = _rot_pair(uq_rope).reshape(Q_RANK, N_HEADS * QK_ROPE).T.astype(BF16)
    uv = w_ukv[:, :, QK_NOPE:].transpose(1, 0, 2)
    wo = p["w_o_attn"][l].reshape(N_HEADS, V_DIM, d)
    w["wfold"] = _bmm(uv, wo).reshape(N_HEADS * LAT, d).astype(BF16)

    w["lng"] = p["sgu_ln_g"][l][None, :]
    w["lnb"] = p["sgu_ln_b"][l][None, :]
    gpl = LANES // (SGU_WIDTH // SGU_GROUPS)
    ws = p["w_s"][l].reshape(SGU_GROUPS // gpl, gpl, CHUNK, CHUNK)
    w["wsp"] = ws.transpose(0, 2, 1, 3).reshape(SGU_GROUPS // gpl, CHUNK, gpl * CHUNK).astype(BF16)
    w["bs"] = jnp.repeat(p["b_s"][l].T, SGU_WIDTH // SGU_GROUPS, axis=1)
    w["wosgu"] = p["w_o_sgu"][l].astype(BF16)
    w["wout"] = p["w_out"][l].astype(BF16)
    w["ln1g"] = p["ln1_g"][l][None, :]
    w["ln1b"] = p["ln1_b"][l][None, :]
    w["rwt"] = p["router_w"][l].T
    w["rb"] = p["router_b"][l][:, None]
    w["ln2g"] = p["ln2_g"][l][None, :]
    w["ln2b"] = p["ln2_b"][l][None, :]
    return w


def _route_and_dispatch(h2, idx, cnt):
    t = h2.shape[0]
    n_pad = N_EXPERTS * MOE_ROWS
    n_rows = TOP_K * t + n_pad
    dest, counts, padded, pad_start, pad_end = _route(idx, cnt, _pick(t, (512, 256, 128)))
    pad_rows = _padding_rows(counts, padded, pad_start, pad_end, n_pad)
    block_start = jnp.arange(n_rows // MOE_ROWS, dtype=I32) * MOE_ROWS
    block_e = jnp.minimum(_count_le(pad_end, block_start), N_EXPERTS - 1)
    blocks_in_use = pad_end[-1:] // MOE_ROWS
    return _dispatch(h2, dest, pad_rows, n_rows), jnp.concatenate([block_e, blocks_in_use]), dest


def kernel(x_prompt, x_sample, c_prompt, c_sample, w_mod, b_mod, w_in, q_norm_g, kv_norm_g, w_uq, w_ukv, w_o_attn, sgu_ln_g, sgu_ln_b, w_s, b_s, w_o_sgu, w_out, ln1_g, ln1_b, router_w, router_b, w_gate_up, b_gate_up, w_down, b_down, ln2_g, ln2_b):
    p = dict(w_in=w_in, q_norm_g=q_norm_g, kv_norm_g=kv_norm_g, w_uq=w_uq, w_ukv=w_ukv, w_o_attn=w_o_attn,
             sgu_ln_g=sgu_ln_g, sgu_ln_b=sgu_ln_b, w_s=w_s, b_s=b_s, w_o_sgu=w_o_sgu, w_out=w_out,
             ln1_g=ln1_g, ln1_b=ln1_b, router_w=router_w, router_b=router_b, ln2_g=ln2_g, ln2_b=ln2_b)
    depth = w_mod.shape[0]
    d = x_prompt.shape[-1]
    dn_alpha = (2 * depth) ** 0.25
    groups = [_Seqs(x_prompt.shape[0], x_prompt.shape[1], 0),
              _Seqs(x_sample.shape[0], x_sample.shape[1], x_prompt.shape[0])]
    xs = [x_prompt.reshape(-1, d), x_sample.reshape(-1, d)]
    for g in groups:
        assert g.t % (SC_WORKERS * SC_CHUNK) == 0
    assert (N_EXPERTS * MOE_ROWS) % (SC_WORKERS * SC_CHUNK) == 0

    n_seq = sum(g.b for g in groups)
    ns_pad = -(-n_seq // 8) * 8
    c_all = jnp.pad(jnp.concatenate([c_prompt, c_sample], axis=0), ((0, ns_pad - n_seq), (0, 0)))
    mod = _modulation(c_all, w_mod, b_mod)
    rope = _rope_tables(max(g.s for g in groups))
    moe_w = dict(wgu=_deinterleave_gate_up(w_gate_up), bgu=_deinterleave_bias(b_gate_up),
                 wdown=w_down, bdown=b_down[:, :, None, :])

    def tiles(g):
        return (_pick(g.s, (512, 256, 128)), _pick(g.s, (1024, 512, 256, 128)), _pick(g.s, (2048, 1024, 512, 256)))

    for l in range(depth):
        w = dict(_prep_layer(l, p), **moe_w)
        mixer_in = [_inproj(x, mod[l], rope, w, g, tiles(g)[0]) for x, g in zip(xs, groups)]
        routed = []
        for x, g, (qt, klat, vt, sgu, sg) in zip(xs, groups, mixer_in):
            tm, tq, tk = tiles(g)
            o = _attention(qt, klat, vt, g.b, g.s, tq, tk)
            x1, h2, idx, wts, cnt = _outproj(o, sgu, sg, x, mod[l], w, g, tm, dn_alpha)
            routed.append((x1, wts) + _route_and_dispatch(h2, idx, cnt))
        collected = [(x1, wts, _collect(_moe_ffn(rows, block_e, w, l), dest))
                     for x1, wts, rows, block_e, dest in routed]
        xs = [_combine(yg, wts, x1, mod[l], w["ln2g"], w["ln2b"], g, tiles(g)[0], dn_alpha)
              for (x1, wts, yg), g in zip(collected, groups)]

    return tuple(x.reshape(g.b, g.s, d) for x, g in zip(xs, groups))
```

```python
import functools
import math

import jax
import jax.numpy as jnp
from jax import lax
from jax.experimental import pallas as pl
from jax.experimental.pallas import tpu as pltpu
from jax.experimental.pallas import tpu_sc as plsc

F32 = jnp.float32
BF16 = jnp.bfloat16
I32 = jnp.int32

N_HEADS = 8
QK_NOPE = 64
QK_ROPE = 32
V_DIM = 64
Q_RANK = 256
KV_RANK = 128
ROPE_BASE = 10000.0
CHUNK = 128
SGU_WIDTH = 512
SGU_GROUPS = 8
N_EXPERTS = 32
TOP_K = 4
SWIGLU_LIMIT = 7.0
SWIGLU_ALPHA = 1.702
LN_EPS = 1e-5
RMS_EPS = 1e-6
ATTN_SCALE = 1.0 / math.sqrt(QK_NOPE + QK_ROPE)
Q_SCALE = ATTN_SCALE * math.log2(math.e)

LANES = 128
LAT = KV_RANK
QK_PAD = 2 * LANES
V_ROWS = LAT + 16
KEY_CHUNK = 256
WTS_ROWS = 8
MOE_ROWS = 1024
GU_BLOCK = 2 * LANES
VMEM_LIMIT = 56 * 1024 * 1024
SC_CORES = 2
SC_SUBCORES = 16
SC_WORKERS = SC_CORES * SC_SUBCORES
SC_CHUNK = 64


def _cparams(sem, vmem=VMEM_LIMIT):
    return pltpu.CompilerParams(dimension_semantics=sem, vmem_limit_bytes=vmem)


def _pick(n, prefs):
    for p in prefs:
        if n % p == 0:
            return p
    raise ValueError(f"no tile in {prefs} divides {n}")


def _const_spec(shape):
    nd = len(shape)
    return pl.BlockSpec(shape, lambda *_: (0,) * nd)


def _dot(a, b):
    return jnp.dot(a, b, preferred_element_type=F32)


def _layer_norm(y, g, b):
    mu = jnp.mean(y, axis=-1, keepdims=True)
    d = y - mu
    var = jnp.mean(d * d, axis=-1, keepdims=True)
    return d * lax.rsqrt(var + LN_EPS) * g + b


def _rms_norm(y, g):
    return y * lax.rsqrt(jnp.mean(y * y, axis=-1, keepdims=True) + RMS_EPS) * g


HIGH_HALF = -65536


def _pack_bf16_pairs(y):
    w = y.shape[1] // 2
    bits = lax.bitcast_convert_type(y.astype(BF16).astype(F32), I32)
    return (bits[:, w:] & HIGH_HALF) | lax.shift_right_logical(bits[:, :w], 16)


def _unpack_bf16_pairs(p):
    lo = lax.bitcast_convert_type(lax.shift_left(p, 16), F32)
    hi = lax.bitcast_convert_type(p & HIGH_HALF, F32)
    return lo, hi


class _Seqs:
    def __init__(self, b, s, seq0):
        self.b, self.s, self.seq0 = b, s, seq0
        self.t = b * s

    def seq_of_tile(self, i, tm):
        return self.seq0 + (i * tm) // self.s

    def pos_tile(self, i, tm):
        return i % (self.s // tm)


def _mod_kernel(c_ref, w_ref, b_ref, o_ref):
    c = c_ref[...]
    a = c * jax.nn.sigmoid(c)
    o_ref[0] = jnp.dot(a, w_ref[0], preferred_element_type=F32,
                       precision=lax.Precision.HIGHEST) + b_ref[0]


def _modulation(c_all, w_mod, b_mod):
    depth, d, six_d = w_mod.shape
    ns = c_all.shape[0]
    n_col = six_d // d
    out = pl.pallas_call(
        _mod_kernel,
        grid=(depth, n_col),
        in_specs=[
            pl.BlockSpec((ns, d), lambda l, j: (0, 0)),
            pl.BlockSpec((1, d, d), lambda l, j: (l, 0, j)),
            pl.BlockSpec((1, 1, d), lambda l, j: (l, 0, j)),
        ],
        out_specs=pl.BlockSpec((1, ns, d), lambda l, j: (l, 0, j)),
        out_shape=jax.ShapeDtypeStruct((depth, ns, six_d), F32),
        compiler_params=_cparams(("parallel", "parallel")),
        name="modulation",
    )(c_all, w_mod, b_mod.reshape(depth, 1, six_d))
    return out.reshape(depth, ns, n_col, d)


def _bmm_kernel(a_ref, b_ref, o_ref):
    o_ref[0] = jnp.dot(a_ref[0], b_ref[0], preferred_element_type=F32,
                       precision=lax.Precision.HIGHEST)


def _bmm(a, b):
    h, m, k = a.shape
    n = b.shape[2]
    return pl.pallas_call(
        _bmm_kernel,
        grid=(h,),
        in_specs=[pl.BlockSpec((1, m, k), lambda i: (i, 0, 0)),
                  pl.BlockSpec((1, k, n), lambda i: (i, 0, 0))],
        out_specs=pl.BlockSpec((1, m, n), lambda i: (i, 0, 0)),
        out_shape=jax.ShapeDtypeStruct((h, m, n), F32),
        compiler_params=_cparams(("parallel",)),
        name="weight_fold",
    )(a, b)


def _gelu_tanh(x):
    c = math.sqrt(2.0 / math.pi)
    return 0.5 * x * (1.0 + jnp.tanh(c * (x + 0.044715 * (x * x * x))))


def _inproj_kernel(x_ref, mod_ref, cos_ref, sin_ref, cost_ref, sint_ref, wa_ref, wuv_ref, wg_ref, gq_ref,
                   gkv_ref, wqlt_ref, wqrt_ref, wqrst_ref, lng_ref, lnb_ref, wsp_ref, bs_ref,
                   q_ref, klat_ref, vt_ref, sgu_ref, sg_ref):
    tm = x_ref.shape[0]
    x = x_ref[...]
    sh1 = mod_ref[0:1, :]
    sc1 = mod_ref[1:2, :]
    h = (x * (1.0 + sc1) + sh1).astype(BF16)

    za = _dot(h, wa_ref[...])
    cq = za[:, 0:Q_RANK]
    ckv = za[:, Q_RANK:Q_RANK + KV_RANK]
    kr = za[:, Q_RANK + KV_RANK:Q_RANK + KV_RANK + LANES]
    krs = za[:, Q_RANK + KV_RANK + LANES:Q_RANK + KV_RANK + 2 * LANES]
    cos = cos_ref[...]
    sin = sin_ref[...]
    cqn = _rms_norm(cq, gq_ref[...]).astype(BF16)
    ckvn = _rms_norm(ckv, gkv_ref[...])
    klat_ref[:, 0:LAT] = ckvn.astype(BF16)
    klat_ref[:, LAT:QK_PAD] = (kr * cos + krs * sin).astype(BF16)
    vt_ref[0:LAT, :] = ckvn.T.astype(BF16)
    extra = lax.broadcasted_iota(I32, (V_ROWS - LAT, tm), 0)
    vt_ref[LAT:V_ROWS, :] = jnp.where(extra == 0, 1.0, 0.0).astype(BF16)

    nt = (((1,), (1,)), ((), ()))
    qlt = lax.dot_general(wqlt_ref[...], cqn, nt, preferred_element_type=F32)
    qrt = lax.dot_general(wqrt_ref[...], cqn, nt, preferred_element_type=F32)
    qrst = lax.dot_general(wqrst_ref[...], cqn, nt, preferred_element_type=F32)
    cos_q = cost_ref[...] * Q_SCALE
    sin_q = sint_ref[...] * Q_SCALE
    pad = jnp.zeros((QK_PAD - LAT - QK_ROPE, tm), BF16)
    for hd in range(N_HEADS):
        q_ref[hd, 0:LAT, :] = (qlt[hd * LAT:(hd + 1) * LAT, :] * Q_SCALE).astype(BF16)
        rows = slice(hd * QK_ROPE, (hd + 1) * QK_ROPE)
        q_ref[hd, LAT:LAT + QK_ROPE, :] = (qrt[rows, :] * cos_q + qrst[rows, :] * sin_q).astype(BF16)
        q_ref[hd, LAT + QK_ROPE:QK_PAD, :] = pad

    sg_ref[...] = jax.nn.sigmoid(_dot(h, wg_ref[...])).astype(BF16)

    guv = _gelu_tanh(_dot(h, wuv_ref[...]))
    u = guv[:, :SGU_WIDTH]
    vn = _layer_norm(guv[:, SGU_WIDTH:], lng_ref[...], lnb_ref[...]).astype(BF16)
    lane = lax.broadcasted_iota(I32, (CHUNK, LANES), 1)
    low_half = lane < (LANES // 2)
    zero = jnp.zeros((CHUNK, LANES), BF16)
    for c in range(tm // CHUNK):
        rows = slice(c * CHUNK, (c + 1) * CHUNK)
        vc = vn[rows, :]
        mixed = []
        for j in range(SGU_WIDTH // LANES):
            blk = vc[:, j * LANES:(j + 1) * LANES]
            rhs = jnp.concatenate([jnp.where(low_half, blk, zero), jnp.where(low_half, zero, blk)], axis=0)
            mixed.append(_dot(wsp_ref[j], rhs))
        mixed = jnp.concatenate(mixed, axis=1) + bs_ref[...]
        sgu_ref[rows, :] = (u[rows, :] * mixed).astype(BF16)


def _inproj(x, mod_l, rope, w, g, tm):
    t, d = x.shape
    n = t // tm
    seq = lambda i: (g.seq_of_tile(i, tm), 0, 0)
    pos = lambda i: (g.pos_tile(i, tm), 0)
    pos_t = lambda i: (0, g.pos_tile(i, tm))
    row = lambda i: (i, 0)
    col = lambda i: (0, i)
    cos_t, sin_t, cos_tt, sin_tt = rope
    consts = [w["wa"], w["wuv"], w["wg"], w["gq"], w["gkv"], w["wqlt"], w["wqrt"], w["wqrst"],
              w["lng"], w["lnb"], w["wsp"], w["bs"]]
    return pl.pallas_call(
        _inproj_kernel,
        grid=(n,),
        in_specs=[pl.BlockSpec((tm, d), row),
                  pl.BlockSpec((None, mod_l.shape[1], d), seq),
                  pl.BlockSpec((tm, LANES), pos),
                  pl.BlockSpec((tm, LANES), pos),
                  pl.BlockSpec((QK_ROPE, tm), pos_t),
                  pl.BlockSpec((QK_ROPE, tm), pos_t)] + [_const_spec(c.shape) for c in consts],
        out_specs=[pl.BlockSpec((N_HEADS, QK_PAD, tm), lambda i: (0, 0, i)),
                   pl.BlockSpec((tm, QK_PAD), row),
                   pl.BlockSpec((V_ROWS, tm), col),
                   pl.BlockSpec((tm, SGU_WIDTH), row),
                   pl.BlockSpec((tm, 2 * d), row)],
        out_shape=[jax.ShapeDtypeStruct((N_HEADS, QK_PAD, t), BF16),
                   jax.ShapeDtypeStruct((t, QK_PAD), BF16),
                   jax.ShapeDtypeStruct((V_ROWS, t), BF16),
                   jax.ShapeDtypeStruct((t, SGU_WIDTH), BF16),
                   jax.ShapeDtypeStruct((t, 2 * d), BF16)],
        compiler_params=_cparams(("parallel",)),
        name="token_mixer_in",
    )(x, mod_l, cos_t, sin_t, cos_tt, sin_tt, *consts)


def _attn_kernel(qt_ref, k_ref, vt_ref, o_ref, m_ref, acc_ref, s_ref):
    ki = pl.program_id(2)
    nh = qt_ref.shape[0]

    @pl.when(ki == 0)
    def _():
        m_ref[...] = jnp.full(m_ref.shape, -jnp.inf, F32)
        acc_ref[...] = jnp.zeros(acc_ref.shape, F32)

    k = k_ref[...]
    tk = k.shape[0]
    def scores(hd):
        st = _dot(k, qt_ref[hd])
        s_ref[hd % 2] = st
        return jnp.max(st, axis=0, keepdims=True)

    mx_next = scores(0)
    for hd in range(nh):
        mx = mx_next
        if hd + 1 < nh:
            mx_next = scores(hd + 1)
        m_prev = m_ref[hd:hd + 1, :]
        m_new = jnp.maximum(m_prev, mx)
        alpha = jnp.exp2(m_prev - m_new)
        pv = None
        for kt in range(tk // KEY_CHUNK):
            keys = slice(kt * KEY_CHUNK, (kt + 1) * KEY_CHUNK)
            p = jnp.exp2(s_ref[hd % 2, keys, :] - m_new).astype(BF16)
            part = _dot(vt_ref[:, keys], p)
            pv = part if pv is None else pv + part
        acc_ref[hd] = alpha * acc_ref[hd] + pv
        m_ref[hd:hd + 1, :] = m_new

    @pl.when(ki == pl.num_programs(2) - 1)
    def _():
        for hd in range(nh):
            a = acc_ref[hd]
            o = a[0:LAT, :] / a[LAT:LAT + 1, :]
            o_ref[:, hd * LAT:(hd + 1) * LAT] = o.T.astype(BF16)


def _attention(qt, klat, vt, nb, s, tq, tk):
    nq, nk = s // tq, s // tk
    return pl.pallas_call(
        _attn_kernel,
        grid=(nb, nq, nk),
        in_specs=[pl.BlockSpec((N_HEADS, QK_PAD, tq), lambda b, i, j: (0, 0, b * nq + i)),
                  pl.BlockSpec((tk, QK_PAD), lambda b, i, j: (b * nk + j, 0)),
                  pl.BlockSpec((V_ROWS, tk), lambda b, i, j: (0, b * nk + j))],
        out_specs=pl.BlockSpec((tq, N_HEADS * LAT), lambda b, i, j: (b * nq + i, 0)),
        out_shape=jax.ShapeDtypeStruct((nb * s, N_HEADS * LAT), BF16),
        scratch_shapes=[pltpu.VMEM((N_HEADS, tq), F32), pltpu.VMEM((N_HEADS, V_ROWS, tq), F32),
                        pltpu.VMEM((2, tk, tq), F32)],
        compiler_params=_cparams(("parallel", "parallel", "arbitrary")),
        name="latent_attention",
    )(qt, klat, vt)


def _outproj_kernel(dn_alpha, o_ref, sgu_ref, sg_ref, x_ref, mod_ref, wfold_ref, wosgu_ref, wout_ref,
                    lng_ref, lnb_ref, rwt_ref, rb_ref, x1_ref, h2_ref, idx_ref, wts_ref, cnt_ref):
    d = x_ref.shape[1]

    @pl.when(pl.program_id(0) == 0)
    def _():
        cnt_ref[...] = jnp.zeros(cnt_ref.shape, F32)

    o_attn = _dot(o_ref[...], wfold_ref[...])
    o_sgu = _dot(sgu_ref[...], wosgu_ref[...])
    sg = sg_ref[...].astype(F32)
    merged = (sg[:, :d] * o_attn + sg[:, d:] * o_sgu).astype(BF16)
    mix = _dot(merged, wout_ref[...])
    g1 = mod_ref[2:3, :]
    sh2 = mod_ref[3:4, :]
    sc2 = mod_ref[4:5, :]
    x1 = _layer_norm(dn_alpha * x_ref[...] + (1.0 + g1) * mix, lng_ref[...], lnb_ref[...])
    x1_ref[...] = x1
    h2 = x1 * (1.0 + sc2) + sh2
    h2_ref[...] = _pack_bf16_pairs(h2)

    nt = (((1,), (1,)), ((), ()))
    h_hi = h2.astype(BF16)
    h_lo = (h2 - h_hi.astype(F32)).astype(BF16)
    rw = rwt_ref[...]
    w_hi = rw.astype(BF16)
    w_lo = (rw - w_hi.astype(F32)).astype(BF16)
    logits = (lax.dot_general(w_hi, h_hi, nt, preferred_element_type=F32)
              + lax.dot_general(w_lo, h_hi, nt, preferred_element_type=F32)
              + lax.dot_general(w_hi, h_lo, nt, preferred_element_type=F32)) + rb_ref[...]
    e_iota = lax.broadcasted_iota(I32, logits.shape, 0)
    vals, idxs = [], []
    chosen = jnp.zeros(logits.shape, F32)
    for _ in range(TOP_K):
        mx = jnp.max(logits, axis=0, keepdims=True)
        ix = jnp.min(jnp.where(logits == mx, e_iota, N_EXPERTS), axis=0, keepdims=True)
        vals.append(mx)
        idxs.append(ix)
        chosen = jnp.where(e_iota == ix, 1.0, chosen)
        logits = jnp.where(e_iota == ix, -jnp.inf, logits)
    cnt_ref[...] += jnp.sum(chosen, axis=1, keepdims=True)
    v = jnp.concatenate(vals, axis=0)
    w = jnp.exp(v - v[0:1, :])
    wn = w / jnp.sum(w, axis=0, keepdims=True)
    wts_ref[...] = jnp.concatenate([wn, jnp.zeros((WTS_ROWS - TOP_K, wn.shape[1]), F32)], axis=0)
    idx_ref[...] = jnp.concatenate(idxs, axis=0)


def _outproj(o, sgu, sg, x, mod_l, w, g, tm, dn_alpha):
    t, d = x.shape
    n = t // tm
    seq = lambda i: (g.seq_of_tile(i, tm), 0, 0)
    row = lambda i: (i, 0)
    col = lambda i: (0, i)
    consts = [w["wfold"], w["wosgu"], w["wout"], w["ln1g"], w["ln1b"], w["rwt"], w["rb"]]
    return pl.pallas_call(
        functools.partial(_outproj_kernel, dn_alpha),
        grid=(n,),
        in_specs=[pl.BlockSpec((tm, N_HEADS * LAT), row),
                  pl.BlockSpec((tm, SGU_WIDTH), row),
                  pl.BlockSpec((tm, 2 * d), row),
                  pl.BlockSpec((tm, d), row),
                  pl.BlockSpec((None, mod_l.shape[1], d), seq)] + [_const_spec(c.shape) for c in consts],
        out_specs=[pl.BlockSpec((tm, d), row),
                   pl.BlockSpec((tm, d // 2), row),
                   pl.BlockSpec((TOP_K, tm), col),
                   pl.BlockSpec((WTS_ROWS, tm), col),
                   _const_spec((N_EXPERTS, LANES))],
        out_shape=[jax.ShapeDtypeStruct((t, d), F32),
                   jax.ShapeDtypeStruct((t, d // 2), I32),
                   jax.ShapeDtypeStruct((TOP_K, t), I32),
                   jax.ShapeDtypeStruct((WTS_ROWS, t), F32),
                   jax.ShapeDtypeStruct((N_EXPERTS, LANES), F32)],
        compiler_params=_cparams(("arbitrary",)),
        name="token_mixer_out",
    )(o, sgu, sg, x, mod_l, *consts)


def _one_hots(idx, tm):
    e_iota = lax.broadcasted_iota(I32, (N_EXPERTS, tm), 0)
    return [(e_iota == idx[k:k + 1, :]).astype(F32) for k in range(TOP_K)]


def _dest_kernel(idx_ref, start_ref, dest_ref, carry_ref):
    tm = idx_ref.shape[1]

    @pl.when(pl.program_id(0) == 0)
    def _():
        carry_ref[...] = start_ref[...]

    ohs = _one_hots(idx_ref[...], tm)
    oh = sum(ohs)
    r = lax.broadcasted_iota(I32, (tm, tm), 0)
    c = lax.broadcasted_iota(I32, (tm, tm), 1)
    before = (r < c).astype(BF16)
    rank = _dot(oh.astype(BF16), before) + carry_ref[:, 0:1]
    dest_ref[...] = jnp.concatenate(
        [jnp.sum(o * rank, axis=0, keepdims=True) for o in ohs], axis=0).astype(I32)
    carry_ref[...] += jnp.sum(oh, axis=1, keepdims=True)


def _route(idx, cnt, tm):
    k, t = idx.shape
    n = t // tm
    col = lambda i: (0, i)
    counts = cnt[:, 0].astype(I32)
    padded = (counts + MOE_ROWS - 1) // MOE_ROWS * MOE_ROWS
    pad_end = jnp.cumsum(padded)
    pad_start = pad_end - padded
    start = jnp.broadcast_to(pad_start.astype(F32)[:, None], (N_EXPERTS, LANES))
    dest = pl.pallas_call(
        _dest_kernel,
        grid=(n,),
        in_specs=[pl.BlockSpec((k, tm), col), _const_spec((N_EXPERTS, LANES))],
        out_specs=pl.BlockSpec((k, tm), col),
        out_shape=jax.ShapeDtypeStruct((k, t), I32),
        scratch_shapes=[pltpu.VMEM((N_EXPERTS, LANES), F32)],
        compiler_params=_cparams(("arbitrary",)),
        name="expert_dest_rows",
    )(idx, start)
    return dest, counts, padded, pad_start, pad_end


def _count_le(sorted_vals, v):
    return jnp.sum((sorted_vals[None, :] <= v[:, None]).astype(I32), axis=1)


def _padding_rows(counts, padded, pad_start, pad_end, n_pad):
    per = padded - counts
    cum = jnp.cumsum(per)
    offset = jnp.concatenate([pad_start + counts - (cum - per), pad_end[-1:] - cum[-1:]])
    j = jnp.arange(n_pad, dtype=I32)
    e = _count_le(cum, j)
    pick = (e[:, None] == jnp.arange(N_EXPERTS + 1, dtype=I32)[None, :]).astype(I32)
    return j + jnp.sum(pick * offset[None, :], axis=1)


def _sc_mesh():
    return plsc.VectorSubcoreMesh(core_axis_name="c", subcore_axis_name="s")


def _sc_scatter_rows(src, idx_main, idx_pad, n_out):
    nw, cpk, c = idx_main.shape
    width = src.shape[1]
    cpw = src.shape[0] // (nw * c)
    copies = cpk // cpw
    ppw = idx_pad.shape[1]

    @functools.partial(
        pl.kernel, mesh=_sc_mesh(),
        out_type=jax.ShapeDtypeStruct((n_out, width), src.dtype),
        scratch_types=[pltpu.VMEM((cpk, c), I32), pltpu.VMEM((ppw, c), I32),
                       pltpu.VMEM((c, width), src.dtype), pltpu.SemaphoreType.DMA],
        name="dispatch_rows")
    def run(src_hbm, idx_hbm, pad_hbm, out_hbm, idx_v, pad_v, rows_v, sem):
        wid = lax.axis_index("s") * SC_CORES + lax.axis_index("c")
        pltpu.sync_copy(idx_hbm.at[wid], idx_v)
        pltpu.sync_copy(pad_hbm.at[wid], pad_v)

        @pl.loop(0, cpw)
        def _(j):
            pltpu.sync_copy(src_hbm.at[pl.ds((wid * cpw + j) * c, c)], rows_v)
            cps = [pltpu.async_copy(rows_v, out_hbm.at[idx_v.at[j * copies + kk]], sem)
                   for kk in range(copies)]
            for cp in cps:
                cp.wait()

        n_src_chunks = src.shape[0] // c

        @pl.loop(0, ppw)
        def _(j):
            pltpu.sync_copy(src_hbm.at[pl.ds(((wid * ppw + j) % n_src_chunks) * c, c)], rows_v)
            pltpu.async_copy(rows_v, out_hbm.at[pad_v.at[j]], sem).wait()

    return run(src, idx_main, idx_pad)


def _sc_gather_rows(src, idx):
    nw, cpw, c = idx.shape
    width = src.shape[1]

    @functools.partial(
        pl.kernel, mesh=_sc_mesh(),
        out_type=jax.ShapeDtypeStruct((nw * cpw * c, width), src.dtype),
        scratch_types=[pltpu.VMEM((cpw, c), I32), pltpu.VMEM((c, width), src.dtype),
                       pltpu.SemaphoreType.DMA],
        name="collect_rows")
    def run(src_hbm, idx_hbm, out_hbm, idx_v, rows_v, sem):
        wid = lax.axis_index("s") * SC_CORES + lax.axis_index("c")
        pltpu.sync_copy(idx_hbm.at[wid], idx_v)

        @pl.loop(0, cpw)
        def _(j):
            pltpu.async_copy(src_hbm.at[idx_v.at[j]], rows_v, sem).wait()
            pltpu.sync_copy(rows_v, out_hbm.at[pl.ds((wid * cpw + j) * c, c)])

    return run(src, idx)


def _dispatch(h2, dest, pad_rows, n_rows):
    k, t = dest.shape
    cpw = t // (SC_WORKERS * SC_CHUNK)
    idx_main = dest.reshape(k, SC_WORKERS, cpw, SC_CHUNK).transpose(1, 2, 0, 3)
    idx_main = idx_main.reshape(SC_WORKERS, cpw * k, SC_CHUNK)
    idx_pad = pad_rows.reshape(SC_WORKERS, -1, SC_CHUNK)
    return _sc_scatter_rows(h2, idx_main, idx_pad, n_rows)


def _collect(y, dest):
    k, t = dest.shape
    idx = dest.reshape(SC_WORKERS, (k * t) // (SC_WORKERS * SC_CHUNK), SC_CHUNK)
    return _sc_gather_rows(y, idx).reshape(k, t, y.shape[1])


def _moe_kernel(be_ref, x_ref, wgu_ref, bgu_ref, wd_ref, bd_ref, y_ref):
    in_use = pl.program_id(0) < be_ref[pl.num_programs(0)]

    @pl.when(in_use)
    def _():
        lo, hi = _unpack_bf16_pairs(x_ref[...])
        x = jnp.concatenate([lo, hi], axis=1).astype(BF16)
        gu = _dot(x, wgu_ref[...]) + bgu_ref[...]
        acts = []
        for b in range(gu.shape[1] // GU_BLOCK):
            glu = jnp.minimum(gu[:, b * GU_BLOCK:b * GU_BLOCK + LANES], SWIGLU_LIMIT)
            lin = jnp.clip(gu[:, b * GU_BLOCK + LANES:(b + 1) * GU_BLOCK], -SWIGLU_LIMIT, SWIGLU_LIMIT)
            acts.append(glu * jax.nn.sigmoid(SWIGLU_ALPHA * glu) * (lin + 1.0))
        act = jnp.concatenate(acts, axis=1).astype(BF16)
        y_ref[...] = _pack_bf16_pairs(_dot(act, wd_ref[...].astype(BF16)) + bd_ref[...])

    @pl.when(jnp.logical_not(in_use))
    def _():
        y_ref[...] = jnp.zeros(y_ref.shape, y_ref.dtype)


def _moe_ffn(xs, block_e, w, l):
    n_rows, half = xs.shape
    d = 2 * half
    f2 = w["wgu"].shape[3]
    n_blocks = n_rows // MOE_ROWS
    row = lambda i, be: (i, 0)
    exp = lambda i, be: (l, be[i], 0, 0)
    return pl.pallas_call(
        _moe_kernel,
        grid_spec=pltpu.PrefetchScalarGridSpec(
            num_scalar_prefetch=1,
            grid=(n_blocks,),
            in_specs=[pl.BlockSpec((MOE_ROWS, half), row),
                      pl.BlockSpec((None, None, d, f2), exp), pl.BlockSpec((None, None, 1, f2), exp),
                      pl.BlockSpec((None, None, f2 // 2, d), exp), pl.BlockSpec((None, None, 1, d), exp)],
            out_specs=pl.BlockSpec((MOE_ROWS, half), row)),
        out_shape=jax.ShapeDtypeStruct((n_rows, half), I32),
        compiler_params=_cparams(("arbitrary",)),
        name="expert_ffn",
    )(block_e, xs, w["wgu"], w["bgu"], w["wdown"], w["bdown"])


def _combine_kernel(dn_alpha, yg_ref, wts_ref, x1_ref, mod_ref, lng_ref, lnb_ref, x2_ref):
    wts = wts_ref[...].T
    lo = hi = None
    for k in range(TOP_K):
        l_k, h_k = _unpack_bf16_pairs(yg_ref[k])
        w_k = wts[:, k:k + 1]
        lo = l_k * w_k if lo is None else lo + l_k * w_k
        hi = h_k * w_k if hi is None else hi + h_k * w_k
    ff = jnp.concatenate([lo, hi], axis=1)
    g2 = mod_ref[5:6, :]
    x2_ref[...] = _layer_norm(dn_alpha * x1_ref[...] + (1.0 + g2) * ff, lng_ref[...], lnb_ref[...])


def _combine(yg, wts, x1, mod_l, ln_g, ln_b, g, tm, dn_alpha):
    t, d = x1.shape
    row = lambda i: (i, 0)
    seq = lambda i: (g.seq_of_tile(i, tm), 0, 0)
    return pl.pallas_call(
        functools.partial(_combine_kernel, dn_alpha),
        grid=(t // tm,),
        in_specs=[pl.BlockSpec((TOP_K, tm, d // 2), lambda i: (0, i, 0)),
                  pl.BlockSpec((WTS_ROWS, tm), lambda i: (0, i)),
                  pl.BlockSpec((tm, d), row),
                  pl.BlockSpec((None, mod_l.shape[1], d), seq),
                  _const_spec(ln_g.shape), _const_spec(ln_b.shape)],
        out_specs=pl.BlockSpec((tm, d), row),
        out_shape=jax.ShapeDtypeStruct((t, d), F32),
        compiler_params=_cparams(("parallel",)),
        name="moe_combine",
    )(yg, wts, x1, mod_l, ln_g, ln_b)


def _deinterleave_kernel(w_ref, perm_ref, o_ref):
    w = w_ref[...].astype(BF16)
    for b in range(w.shape[1] // GU_BLOCK):
        cols = slice(b * GU_BLOCK, (b + 1) * GU_BLOCK)
        o_ref[:, cols] = _dot(w[:, cols], perm_ref[...]).astype(BF16)


def _deinterleave_gate_up(w_gate_up):
    depth, ne, d, f2 = w_gate_up.shape
    tn = _pick(f2, (1024, 512, GU_BLOCK))
    j = jnp.arange(GU_BLOCK)
    target = (j % 2) * LANES + j // 2
    perm = (target[:, None] == jnp.arange(GU_BLOCK)[None, :]).astype(BF16)
    out = pl.pallas_call(
        _deinterleave_kernel,
        grid=(depth * ne, f2 // tn),
        in_specs=[pl.BlockSpec((None, d, tn), lambda e, j: (e, 0, j)), _const_spec((GU_BLOCK, GU_BLOCK))],
        out_specs=pl.BlockSpec((None, d, tn), lambda e, j: (e, 0, j)),
        out_shape=jax.ShapeDtypeStruct((depth * ne, d, f2), BF16),
        compiler_params=_cparams(("parallel", "parallel")),
        name="gate_up_columns",
    )(w_gate_up.reshape(depth * ne, d, f2), perm)
    return out.reshape(depth, ne, d, f2)


def _deinterleave_bias(b_gate_up):
    depth, ne, f2 = b_gate_up.shape
    b = b_gate_up.reshape(depth, ne, f2 // GU_BLOCK, LANES, 2).transpose(0, 1, 2, 4, 3)
    return b.reshape(depth, ne, 1, f2)


def _rope_tables(seq_len):
    inv = 1.0 / (ROPE_BASE ** (jnp.arange(0, QK_ROPE, 2, dtype=F32) / QK_ROPE))
    ang = jnp.arange(seq_len, dtype=F32)[:, None] * inv[None, :]
    pad = jnp.zeros((seq_len, LANES - QK_ROPE), F32)
    cos2 = jnp.concatenate([jnp.cos(ang), jnp.cos(ang)], axis=1)
    sin2 = jnp.concatenate([jnp.sin(ang), jnp.sin(ang)], axis=1)
    return (jnp.concatenate([cos2, pad], axis=1), jnp.concatenate([sin2, pad], axis=1), cos2.T, sin2.T)


def _rot_pair(w):
    half = w.shape[-1] // 2
    return jnp.concatenate([-w[..., half:], w[..., :half]], axis=-1)


def _pad_lanes(w):
    return jnp.pad(w, [(0, 0)] * (w.ndim - 1) + [(0, LANES - w.shape[-1])])


def _prep_layer(l, p):
    d = p["w_in"].shape[1]
    w_in = p["w_in"][l]
    i1 = Q_RANK
    i2 = i1 + KV_RANK
    i3 = i2 + QK_ROPE
    i4 = i3 + 2 * SGU_WIDTH
    w_kr = w_in[:, i2:i3]
    w = {}
    w["wa"] = jnp.concatenate([w_in[:, :i2], _pad_lanes(w_kr), _pad_lanes(_rot_pair(w_kr))], axis=1).astype(BF16)
    w["wuv"] = w_in[:, i3:i4].astype(BF16)
    w["wg"] = w_in[:, i4:].astype(BF16)
    w["gq"] = p["q_norm_g"][l][None, :]
    w["gkv"] = p["kv_norm_g"][l][None, :]

    w_uq = p["w_uq"][l].reshape(Q_RANK, N_HEADS, QK_NOPE + QK_ROPE)
    w_ukv = p["w_ukv"][l].reshape(KV_RANK, N_HEADS, QK_NOPE + V_DIM)
    uq_nope = w_uq[:, :, :QK_NOPE].transpose(1, 0, 2)
    uk_t = w_ukv[:, :, :QK_NOPE].transpose(1, 2, 0)
    w["wqlt"] = _bmm(uq_nope, uk_t).transpose(0, 2, 1).reshape(N_HEADS * LAT, Q_RANK).astype(BF16)
    uq_rope = w_uq[:, :, QK_NOPE:]
    w["wqrt"] = uq_rope.reshape(Q_RANK, N_HEADS * QK_ROPE).T.astype(BF16)
    w["wqrst"] = _rot_pair(uq_rope).reshape(Q_RANK, N_HEADS * QK_ROPE).T.astype(BF16)
    uv = w_ukv[:, :, QK_NOPE:].transpose(1, 0, 2)
    wo = p["w_o_attn"][l].reshape(N_HEADS, V_DIM, d)
    w["wfold"] = _bmm(uv, wo).reshape(N_HEADS * LAT, d).astype(BF16)

    w["lng"] = p["sgu_ln_g"][l][None, :]
    w["lnb"] = p["sgu_ln_b"][l][None, :]
    gpl = LANES // (SGU_WIDTH // SGU_GROUPS)
    ws = p["w_s"][l].reshape(SGU_GROUPS // gpl, gpl, CHUNK, CHUNK)
    w["wsp"] = ws.transpose(0, 2, 1, 3).reshape(SGU_GROUPS // gpl, CHUNK, gpl * CHUNK).astype(BF16)
    w["bs"] = jnp.repeat(p["b_s"][l].T, SGU_WIDTH // SGU_GROUPS, axis=1)
    w["wosgu"] = p["w_o_sgu"][l].astype(BF16)
    w["wout"] = p["w_out"][l].astype(BF16)
    w["ln1g"] = p["ln1_g"][l][None, :]
    w["ln1b"] = p["ln1_b"][l][None, :]
    w["rwt"] = p["router_w"][l].T
    w["rb"] = p["router_b"][l][:, None]
    w["ln2g"] = p["ln2_g"][l][None, :]
    w["ln2b"] = p["ln2_b"][l][None, :]
    return w


def _route_and_dispatch(h2, idx, cnt):
    t = h2.shape[0]
    n_pad = N_EXPERTS * MOE_ROWS
    n_rows = TOP_K * t + n_pad
    dest, counts, padded, pad_start, pad_end = _route(idx, cnt, _pick(t, (512, 256, 128)))
    pad_rows = _padding_rows(counts, padded, pad_start, pad_end, n_pad)
    block_start = jnp.arange(n_rows // MOE_ROWS, dtype=I32) * MOE_ROWS
    block_e = jnp.minimum(_count_le(pad_end, block_start), N_EXPERTS - 1)
    blocks_in_use = pad_end[-1:] // MOE_ROWS
    return _dispatch(h2, dest, pad_rows, n_rows), jnp.concatenate([block_e, blocks_in_use]), dest


def kernel(x_prompt, x_sample, c_prompt, c_sample, w_mod, b_mod, w_in, q_norm_g, kv_norm_g, w_uq, w_ukv, w_o_attn, sgu_ln_g, sgu_ln_b, w_s, b_s, w_o_sgu, w_out, ln1_g, ln1_b, router_w, router_b, w_gate_up, b_gate_up, w_down, b_down, ln2_g, ln2_b):
    p = dict(w_in=w_in, q_norm_g=q_norm_g, kv_norm_g=kv_norm_g, w_uq=w_uq, w_ukv=w_ukv, w_o_attn=w_o_attn,
             sgu_ln_g=sgu_ln_g, sgu_ln_b=sgu_ln_b, w_s=w_s, b_s=b_s, w_o_sgu=w_o_sgu, w_out=w_out,
             ln1_g=ln1_g, ln1_b=ln1_b, router_w=router_w, router_b=router_b, ln2_g=ln2_g, ln2_b=ln2_b)
    depth = w_mod.shape[0]
    d = x_prompt.shape[-1]
    dn_alpha = (2 * depth) ** 0.25
    groups = [_Seqs(x_prompt.shape[0], x_prompt.shape[1], 0),
              _Seqs(x_sample.shape[0], x_sample.shape[1], x_prompt.shape[0])]
    xs = [x_prompt.reshape(-1, d), x_sample.reshape(-1, d)]
    for g in groups:
        assert g.t % (SC_WORKERS * SC_CHUNK) == 0
    assert (N_EXPERTS * MOE_ROWS) % (SC_WORKERS * SC_CHUNK) == 0

    n_seq = sum(g.b for g in groups)
    ns_pad = -(-n_seq // 8) * 8
    c_all = jnp.pad(jnp.concatenate([c_prompt, c_sample], axis=0), ((0, ns_pad - n_seq), (0, 0)))
    mod = _modulation(c_all, w_mod, b_mod)
    rope = _rope_tables(max(g.s for g in groups))
    moe_w = dict(wgu=_deinterleave_gate_up(w_gate_up), bgu=_deinterleave_bias(b_gate_up),
                 wdown=w_down, bdown=b_down[:, :, None, :])

    def tiles(g):
        return (_pick(g.s, (512, 256, 128)), _pick(g.s, (1024, 512, 256, 128)), _pick(g.s, (2048, 1024, 512, 256)))

    for l in range(depth):
        w = dict(_prep_layer(l, p), **moe_w)
        mixer_in = [_inproj(x, mod[l], rope, w, g, tiles(g)[0]) for x, g in zip(xs, groups)]
        routed = []
        for x, g, (qt, klat, vt, sgu, sg) in zip(xs, groups, mixer_in):
            tm, tq, tk = tiles(g)
            o = _attention(qt, klat, vt, g.b, g.s, tq, tk)
            x1, h2, idx, wts, cnt = _outproj(o, sgu, sg, x, mod[l], w, g, tm, dn_alpha)
            routed.append((x1, wts) + _route_and_dispatch(h2, idx, cnt))
        collected = [(x1, wts, _collect(_moe_ffn(rows, block_e, w, l), dest))
                     for x1, wts, rows, block_e, dest in routed]
        xs = [_combine(yg, wts, x1, mod[l], w["ln2g"], w["ln2b"], g, tiles(g)[0], dn_alpha)
              for (x1, wts, yg), g in zip(collected, groups)]

    return tuple(x.reshape(g.b, g.s, d) for x, g in zip(xs, groups))
```
